```python
import jax, jax.numpy as jnp
from jax import lax
import numpy as np

D_MODEL = 1024
BATCH = 16
SEQ = 256
DEPTH = 1
DEC_BATCH = 4
DEC_SEQ = 4096
PAST_LEN = 256

GRID_W = 64
H_A = 4
DK_A = 128
DV_A = 128
D_A = H_A * DV_A
CONV_K = 5
CHUNK_A = 64
H_B = 4
DK_B = 64
DV_B = 128
D_B = H_B * DV_B
CHUNK_B = 128
ROPE_BASE = 10000.0
D_MIX = D_A + D_B
D_IN = 3 * D_A + D_A + 2 * H_A + 2 * H_A + 2 * H_B * DK_B + D_B + D_B
D_FF = 4 * D_MODEL
ALPHA = (2.0 * DEPTH) ** 0.25
BETA_DN = (8.0 * DEPTH) ** -0.25

kernel_name = "hybrid_deltanet_retention_dit_step"


def layer_norm(x, w=None, b=None, eps=1e-6):
    xf = x.astype(jnp.float32)
    mu = jnp.mean(xf, -1, keepdims=True)
    var = jnp.mean(jnp.square(xf - mu), -1, keepdims=True)
    y = (xf - mu) * lax.rsqrt(var + eps)
    if w is not None:
        y = y * w.astype(jnp.float32) + b.astype(jnp.float32)
    return y.astype(x.dtype)


def l2norm(x, eps=1e-6):
    return x * lax.rsqrt(jnp.sum(jnp.square(x), -1, keepdims=True) + eps)


def heads(t, n):
    return t.reshape(t.shape[0], t.shape[1], n, -1).transpose(0, 2, 1, 3)


def merge_heads(t):
    b, n, s, d = t.shape
    return t.transpose(0, 2, 1, 3).reshape(b, s, n * d)


def flip_t(t):
    return jnp.flip(t, axis=2)


def short_conv(x, w):
    T = x.shape[1]
    pad = (CONV_K - 1) // 2
    xp = jnp.pad(x, ((0, 0), (pad, CONV_K - 1 - pad), (0, 0)))
    y = sum(xp[:, k:k + T] * w[k] for k in range(CONV_K))
    return jax.nn.silu(y)


def axial_rope(x):
    T = x.shape[2]
    rows = T // GRID_W
    r = jnp.repeat(jnp.arange(rows, dtype=jnp.float32), GRID_W)
    col = jnp.tile(jnp.arange(GRID_W, dtype=jnp.float32), rows)
    nf = DK_B // 4
    inv = ROPE_BASE ** (-jnp.arange(nf, dtype=jnp.float32) / nf)
    ang = jnp.concatenate([r[:, None] * inv, col[:, None] * inv], -1)
    cos, sin = jnp.cos(ang), jnp.sin(ang)
    x1, x2 = jnp.split(x, 2, axis=-1)
    return jnp.concatenate([x1 * cos - x2 * sin, x1 * sin + x2 * cos], -1)


def gated_delta_chunked(q, k, v, g, beta, s0):
    Bt, H, T, DK = q.shape
    DV = v.shape[-1]
    C = CHUNK_A
    N = T // C
    qc = q.reshape(Bt, H, N, C, DK)
    kc = k.reshape(Bt, H, N, C, DK)
    vc = v.reshape(Bt, H, N, C, DV)
    gc = jnp.cumsum(g.reshape(Bt, H, N, C), -1)
    bc = beta.reshape(Bt, H, N, C)
    tri = jnp.tril(jnp.ones((C, C), bool))
    strict = jnp.tril(jnp.ones((C, C), bool), -1)
    diff = gc[..., :, None] - gc[..., None, :]
    L = jnp.where(tri, jnp.exp(jnp.where(tri, diff, 0.0)), 0.0)
    kb = kc * bc[..., None]
    M = jnp.where(strict, jnp.einsum('bhnid,bhnjd->bhnij', kb, kc) * L, 0.0)
    A = M + jnp.eye(C, dtype=M.dtype)
    rhs = jnp.concatenate([vc * bc[..., None], kb * jnp.exp(gc)[..., None]], -1)
    sol = lax.linalg.triangular_solve(A, rhs, left_side=True, lower=True, unit_diagonal=True)
    u, w = sol[..., :DV], sol[..., DV:]
    qk = jnp.where(tri, jnp.einsum('bhnid,bhnjd->bhnij', qc, kc) * L, 0.0)
    q_dec = qc * jnp.exp(gc)[..., None]
    k_dec = kc * jnp.exp(gc[..., -1:] - gc)[..., None]
    g_last = jnp.exp(gc[..., -1])

    def step(S, xs):
        u_i, w_i, qk_i, qd_i, kd_i, gl_i = xs
        v_new = u_i - jnp.einsum('bhcd,bhde->bhce', w_i, S)
        o = jnp.einsum('bhcd,bhde->bhce', qd_i, S) + jnp.einsum('bhij,bhje->bhie', qk_i, v_new)
        S = S * gl_i[..., None, None] + jnp.einsum('bhcd,bhce->bhde', kd_i, v_new)
        return S, o

    xs = (jnp.moveaxis(u, 2, 0), jnp.moveaxis(w, 2, 0), jnp.moveaxis(qk, 2, 0),
          jnp.moveaxis(q_dec, 2, 0), jnp.moveaxis(k_dec, 2, 0), jnp.moveaxis(g_last, 2, 0))
    s_final, o = lax.scan(step, s0, xs)
    o = jnp.moveaxis(o, 0, 2).reshape(Bt, H, T, DV)
    return o, s_final


def retention_log_decay():
    return jnp.log1p(-jnp.exp2(-5.0 - jnp.arange(H_B, dtype=jnp.float32)))


def retention_chunked(q, k, v, log_gamma, s0):
    Bt, H, T, DK = q.shape
    DV = v.shape[-1]
    C = CHUNK_B
    N = T // C
    pos = jnp.arange(C, dtype=jnp.float32)
    diff = pos[:, None] - pos[None, :]
    d_mask = jnp.where(diff >= 0, jnp.exp(log_gamma[:, None, None] * jnp.maximum(diff, 0.0)), 0.0)
    xi = jnp.exp(log_gamma[:, None] * (pos + 1.0))
    zeta = jnp.exp(log_gamma[:, None] * (C - 1.0 - pos))
    g_chunk = jnp.exp(log_gamma * C)
    qc = q.reshape(Bt, H, N, C, DK)
    kc = k.reshape(Bt, H, N, C, DK)
    vc = v.reshape(Bt, H, N, C, DV)
    scores = jnp.einsum('bhnid,bhnjd->bhnij', qc, kc) * d_mask[:, None]
    inner = jnp.einsum('bhnij,bhnje->bhnie', scores, vc)
    ds = jnp.einsum('bhncd,bhnce->bhnde', kc * zeta[:, None, :, None], vc)

    def step(s, ds_n):
        return s * g_chunk[:, None, None] + ds_n, s

    s_final, s_prev = lax.scan(step, s0, jnp.moveaxis(ds, 2, 0))
    s_prev = jnp.moveaxis(s_prev, 0, 2)
    cross = jnp.einsum('bhncd,bhnde->bhnce', qc * xi[:, None, :, None], s_prev)
    return (inner + cross).reshape(Bt, H, T, DV), s_final


def mixer(h, s_delta0, s_ret0, use_rope, w_in, conv_w, a_log, dt_bias, norm_a_w, gn_w, gn_b, w_o):
    Bt, T, _ = h.shape
    f32 = jnp.float32
    p = jnp.einsum('btd,de->bte', h, w_in).astype(f32)
    sizes = (3 * D_A, D_A, 2 * H_A, 2 * H_A, H_B * DK_B, H_B * DK_B, D_B, D_B)
    split_pts = np.cumsum(sizes)[:-1].tolist()
    qkv_a, gate_a, alpha_raw, beta_raw, q_b, k_b, v_b, gate_b = jnp.split(p, split_pts, axis=-1)

    qkv_a = short_conv(qkv_a, conv_w.astype(f32))
    qa, ka, va = jnp.split(qkv_a, 3, axis=-1)
    qa = l2norm(heads(qa, H_A)) * (DK_A ** -0.5)
    ka = l2norm(heads(ka, H_A))
    va = heads(va, H_A)
    g_log = -jnp.exp(a_log.astype(f32)) * jax.nn.softplus(alpha_raw.reshape(Bt, T, 2, H_A) + dt_bias.astype(f32))
    g_log = g_log.transpose(2, 0, 3, 1)
    beta = jax.nn.sigmoid(beta_raw.reshape(Bt, T, 2, H_A)).transpose(2, 0, 3, 1)
    sd0 = s_delta0.astype(f32)
    o_f, sd_f = gated_delta_chunked(qa, ka, va, g_log[0], beta[0], sd0[:, 0])
    o_bw, sd_b = gated_delta_chunked(flip_t(qa), flip_t(ka), flip_t(va), flip_t(g_log[1]), flip_t(beta[1]), sd0[:, 1])
    o_a = o_f + flip_t(o_bw)
    o_a = o_a * lax.rsqrt(jnp.mean(jnp.square(o_a), -1, keepdims=True) + 1e-6) * norm_a_w.astype(f32)
    o_a = merge_heads(o_a) * jax.nn.silu(gate_a)

    qb = heads(q_b, H_B) * (DK_B ** -0.5)
    kb = heads(k_b, H_B)
    vb = heads(v_b, H_B)
    if use_rope:
        qb = axial_rope(qb)
        kb = axial_rope(kb)
    lg = retention_log_decay()
    sr0 = s_ret0.astype(f32)
    r_f, sr_f = retention_chunked(qb, kb, vb, lg, sr0[:, 0])
    r_bw, sr_b = retention_chunked(flip_t(qb), flip_t(kb), flip_t(vb), lg[::-1], sr0[:, 1])
    o_r = r_f + flip_t(r_bw)
    o_r = merge_heads(layer_norm(o_r, eps=1e-5)) * gn_w.astype(f32) + gn_b.astype(f32)
    o_r = o_r * jax.nn.silu(gate_b)

    y = jnp.einsum('bte,ed->btd', jnp.concatenate([o_a, o_r], -1), w_o.astype(f32)).astype(h.dtype)
    return y, jnp.stack([sd_f, sd_b], 1), jnp.stack([sr_f, sr_b], 1)


def trunk(x, cond, s_delta, s_ret, use_rope, w_mod, b_mod, w_in, conv_w, a_log, dt_bias, norm_a_w,
          gn_w, gn_b, w_o, ln1_w, ln1_b, w_ff1, b_ff1, w_ff2, b_ff2, ln2_w, ln2_b):
    new_sd, new_sr = [], []
    for l in range(DEPTH):
        mod = jnp.einsum('bd,de->be', jax.nn.silu(cond), w_mod[l]) + b_mod[l]
        sh1, sc1, g1, sh2, sc2, g2 = jnp.split(mod[:, None, :], 6, axis=-1)
        h = layer_norm(x) * (1.0 + sc1) + sh1
        y, sd, sr = mixer(h, s_delta[:, l], s_ret[:, l], use_rope, w_in[l], conv_w[l], a_log[l], dt_bias[l],
                          norm_a_w[l], gn_w[l], gn_b[l], w_o[l])
        x = layer_norm(ALPHA * x + g1 * y, ln1_w[l], ln1_b[l])
        h = layer_norm(x) * (1.0 + sc2) + sh2
        f = jnp.einsum('btf,fd->btd', jnp.square(jax.nn.relu(jnp.einsum('btd,df->btf', h, w_ff1[l]) + b_ff1[l])),
                       w_ff2[l]) + b_ff2[l]
        x = layer_norm(ALPHA * x + g2 * f, ln2_w[l], ln2_b[l])
        new_sd.append(sd)
        new_sr.append(sr)
    return x, jnp.stack(new_sd, 1), jnp.stack(new_sr, 1)


def setup_inputs(seed: int = 0) -> dict:
    key = jax.random.key(seed)
    ks = jax.random.split(key, 24)
    nrm = jax.random.normal
    f32 = jnp.float32
    dt = jnp.exp(jax.random.uniform(ks[9], (DEPTH, 2, H_A), f32, np.log(1e-3), np.log(1e-1)))
    return {
        "x_prompt": nrm(ks[0], (BATCH, SEQ, D_MODEL), f32),
        "x_sample": nrm(ks[1], (DEC_BATCH, DEC_SEQ, D_MODEL), f32),
        "c": nrm(ks[2], (DEC_BATCH, D_MODEL), f32),
        "state_delta": 0.1 * nrm(ks[3], (DEC_BATCH, DEPTH, 2, H_A, DK_A, DV_A), f32),
        "state_ret": 0.5 * nrm(ks[4], (DEC_BATCH, DEPTH, 2, H_B, DK_B, DV_B), f32),
        "c_ctx": nrm(ks[5], (D_MODEL,), f32),
        "w_mod": 0.5 * D_MODEL ** -0.5 * nrm(ks[6], (DEPTH, D_MODEL, 6 * D_MODEL), f32),
        "b_mod": 0.02 * nrm(ks[7], (DEPTH, 6 * D_MODEL), f32),
        "w_in": D_MODEL ** -0.5 * nrm(ks[8], (DEPTH, D_MODEL, D_IN), f32),
        "conv_w": CONV_K ** -0.5 * nrm(ks[10], (DEPTH, CONV_K, 3 * D_A), f32),
        "a_log": jnp.log(jax.random.uniform(ks[11], (DEPTH, 2, H_A), f32, 1.0, 16.0)),
        "dt_bias": dt + jnp.log(-jnp.expm1(-dt)),
        "norm_a_w": 1.0 + 0.02 * nrm(ks[12], (DEPTH, DV_A), f32),
        "gn_w": 1.0 + 0.02 * nrm(ks[13], (DEPTH, D_B), f32),
        "gn_b": 0.02 * nrm(ks[14], (DEPTH, D_B), f32),
        "w_o": BETA_DN * D_MIX ** -0.5 * nrm(ks[15], (DEPTH, D_MIX, D_MODEL), f32),
        "ln1_w": 1.0 + 0.02 * nrm(ks[16], (DEPTH, D_MODEL), f32),
        "ln1_b": 0.02 * nrm(ks[17], (DEPTH, D_MODEL), f32),
        "w_ff1": D_MODEL ** -0.5 * nrm(ks[18], (DEPTH, D_MODEL, D_FF), f32),
        "b_ff1": 0.02 * nrm(ks[19], (DEPTH, D_FF), f32),
        "w_ff2": BETA_DN * D_FF ** -0.5 * nrm(ks[20], (DEPTH, D_FF, D_MODEL), f32),
        "b_ff2": 0.02 * nrm(ks[21], (DEPTH, D_MODEL), f32),
        "ln2_w": 1.0 + 0.02 * nrm(ks[22], (DEPTH, D_MODEL), f32),
        "ln2_b": 0.02 * nrm(ks[23], (DEPTH, D_MODEL), f32),
    }


def reference(x_prompt, x_sample, c, state_delta, state_ret, c_ctx, w_mod, b_mod, w_in, conv_w, a_log, dt_bias,
              norm_a_w, gn_w, gn_b, w_o, ln1_w, ln1_b, w_ff1, b_ff1, w_ff2, b_ff2, ln2_w, ln2_b):
    weights = (w_mod, b_mod, w_in, conv_w, a_log, dt_bias, norm_a_w, gn_w, gn_b, w_o,
               ln1_w, ln1_b, w_ff1, b_ff1, w_ff2, b_ff2, ln2_w, ln2_b)
    bp = x_prompt.shape[0]
    zero_sd = jnp.zeros((bp, DEPTH, 2, H_A, DK_A, DV_A), jnp.float32)
    zero_sr = jnp.zeros((bp, DEPTH, 2, H_B, DK_B, DV_B), jnp.float32)
    y_prompt, new_state_delta, new_state_ret = trunk(x_prompt, c_ctx[None, :], zero_sd, zero_sr, False, *weights)
    y_sample, _, _ = trunk(x_sample, c, state_delta, state_ret, True, *weights)
    return (y_prompt, y_sample, new_state_delta.astype(x_prompt.dtype), new_state_ret.astype(x_prompt.dtype))
```

```python
import functools
import math

import jax
import jax.numpy as jnp
import numpy as np
from jax import lax
from jax.experimental import pallas as pl
from jax.experimental.pallas import tpu as pltpu

F32 = jnp.float32
BF16 = jnp.bfloat16

D_MODEL = 1024
H_A, DK_A, DV_A = 4, 128, 128
D_A = H_A * DV_A
CONV_K = 5
CHUNK_A = 64
H_B, DK_B, DV_B = 4, 64, 128
D_B = H_B * DV_B
CHUNK_B = 128
GRID_W = 64
ROPE_BASE = 10000.0
D_FF = 4 * D_MODEL
DEPTH = 1
ALPHA = (2.0 * DEPTH) ** 0.25

LANES = 128
SUBLANES = 8
VMEM_LIMIT = 56 * 1024 * 1024

QKV_W = 3 * D_A
COL_GATE_A = QKV_W
COL_QK_B = COL_GATE_A + D_A
COL_V_B = COL_QK_B + 2 * H_B * DK_B
COL_GATE_B = COL_V_B + D_B
COL_AB = COL_GATE_B + D_B
P_W = COL_AB + LANES

TIME_BLOCK = 256
HALO = SUBLANES


def _dot(a, b):
    return jnp.dot(a.astype(BF16), b.astype(BF16), preferred_element_type=F32)


def _dot_nt(a, b):
    return lax.dot_general(a.astype(BF16), b.astype(BF16), (((1,), (1,)), ((), ())),
                           preferred_element_type=F32)


def _dot_tn(a, b):
    return lax.dot_general(a.astype(BF16), b.astype(BF16), (((0,), (0,)), ((), ())),
                           preferred_element_type=F32)


def _split3(x):
    hi = x.astype(BF16)
    r = x - hi.astype(F32)
    mid = r.astype(BF16)
    lo = (r - mid.astype(F32)).astype(BF16)
    return hi, mid, lo


def _silu(x):
    return x * (1.0 / (1.0 + jnp.exp(-x)))


def _layer_norm(x, eps):
    mu = jnp.mean(x, -1, keepdims=True)
    xc = x - mu
    var = jnp.mean(xc * xc, -1, keepdims=True)
    return xc * lax.rsqrt(var + eps)


def _mod_kernel(c_ref, w_ref, b_ref, o_ref):
    o_ref[...] = _dot(_silu(c_ref[...]), w_ref[...]) + b_ref[...]


def _modulation(cond, w_mod, b_mod):
    rows = cond.shape[0]
    n = w_mod.shape[1]
    bn = 512
    return pl.pallas_call(
        _mod_kernel,
        grid=(n // bn,),
        in_specs=[pl.BlockSpec((rows, D_MODEL), lambda j: (0, 0)),
                  pl.BlockSpec((D_MODEL, bn), lambda j: (0, j)),
                  pl.BlockSpec((1, bn), lambda j: (0, j))],
        out_specs=pl.BlockSpec((rows, bn), lambda j: (0, j)),
        out_shape=jax.ShapeDtypeStruct((rows, n), F32),
        compiler_params=pltpu.CompilerParams(dimension_semantics=("arbitrary",)),
        name="modulation",
    )(cond, w_mod, b_mod.reshape(1, n))


def _proj_kernel(x_ref, mod_ref, w_ref, o_ref):
    m = mod_ref[0]
    sh1 = m[:, 0:D_MODEL]
    sc1 = m[:, D_MODEL:2 * D_MODEL]
    h = _layer_norm(x_ref[0], 1e-6) * (1.0 + sc1) + sh1
    o_ref[0] = jnp.dot(h.astype(BF16), w_ref[...], preferred_element_type=F32)


def _projection(x, mod3, row0, row_stride, w_in_p, tm):
    bt, t, _ = x.shape
    return pl.pallas_call(
        _proj_kernel,
        grid=(bt, t // tm),
        in_specs=[pl.BlockSpec((1, tm, D_MODEL), lambda b, i: (b, i, 0)),
                  pl.BlockSpec((1, 1, 6 * D_MODEL), lambda b, i: (row0 + row_stride * b, 0, 0)),
                  pl.BlockSpec((D_MODEL, P_W), lambda b, i: (0, 0))],
        out_specs=pl.BlockSpec((1, tm, P_W), lambda b, i: (b, i, 0)),
        out_shape=jax.ShapeDtypeStruct((bt, t, P_W), F32),
        compiler_params=pltpu.CompilerParams(dimension_semantics=("arbitrary", "arbitrary"),
                                             vmem_limit_bytes=VMEM_LIMIT),
        name="projection",
    )(x, mod3, w_in_p)


def _post_kernel(o_ref, x_ref, mod_ref, wo_ref, ln1w_ref, ln1b_ref, w1_ref, b1_ref, w2_ref, b2_ref,
                 ln2w_ref, ln2b_ref, y_ref):
    m = mod_ref[0]
    g1 = m[:, 2 * D_MODEL:3 * D_MODEL]
    sh2 = m[:, 3 * D_MODEL:4 * D_MODEL]
    sc2 = m[:, 4 * D_MODEL:5 * D_MODEL]
    g2 = m[:, 5 * D_MODEL:6 * D_MODEL]
    y = jnp.dot(o_ref[0], wo_ref[...], preferred_element_type=F32)
    x1 = _layer_norm(ALPHA * x_ref[0] + g1 * y, 1e-6) * ln1w_ref[...] + ln1b_ref[...]
    h = _layer_norm(x1, 1e-6) * (1.0 + sc2) + sh2
    a = jnp.dot(h.astype(BF16), w1_ref[...], preferred_element_type=F32) + b1_ref[...]
    a = jnp.square(jnp.maximum(a, 0.0))
    f = jnp.dot(a.astype(BF16), w2_ref[...], preferred_element_type=F32) + b2_ref[...]
    y_ref[0] = _layer_norm(ALPHA * x1 + g2 * f, 1e-6) * ln2w_ref[...] + ln2b_ref[...]


def _post(o, x, mod3, row0, row_stride, wo, ln1w, ln1b, w1, b1, w2, b2, ln2w, ln2b, tm):
    bt, t, _ = x.shape
    const = lambda b, i: (0, 0)
    return pl.pallas_call(
        _post_kernel,
        grid=(bt, t // tm),
        in_specs=[pl.BlockSpec((1, tm, D_MODEL), lambda b, i: (b, i, 0)),
                  pl.BlockSpec((1, tm, D_MODEL), lambda b, i: (b, i, 0)),
                  pl.BlockSpec((1, 1, 6 * D_MODEL), lambda b, i: (row0 + row_stride * b, 0, 0)),
                  pl.BlockSpec((D_MODEL, D_MODEL), const),
                  pl.BlockSpec((1, D_MODEL), const),
                  pl.BlockSpec((1, D_MODEL), const),
                  pl.BlockSpec((D_MODEL, D_FF), const),
                  pl.BlockSpec((1, D_FF), const),
                  pl.BlockSpec((D_FF, D_MODEL), const),
                  pl.BlockSpec((1, D_MODEL), const),
                  pl.BlockSpec((1, D_MODEL), const),
                  pl.BlockSpec((1, D_MODEL), const)],
        out_specs=pl.BlockSpec((1, tm, D_MODEL), lambda b, i: (b, i, 0)),
        out_shape=jax.ShapeDtypeStruct((bt, t, D_MODEL), F32),
        compiler_params=pltpu.CompilerParams(dimension_semantics=("arbitrary", "arbitrary"),
                                             vmem_limit_bytes=VMEM_LIMIT),
        name="post",
    )(o, x, mod3, wo, ln1w, ln1b, w1, b1, w2, b2, ln2w, ln2b)


_LOG_GAMMA = [math.log1p(-(2.0 ** (-5.0 - h))) for h in range(H_B)]


SOLVE_BASE = 8


def _unit_triangular_inverse(m):
    i = lax.broadcasted_iota(jnp.int32, (CHUNK_A, CHUNK_A), 0)
    j = lax.broadcasted_iota(jnp.int32, (CHUNK_A, CHUNK_A), 1)
    same = lambda s: (i // s) == (j // s)
    p = jnp.where(same(SOLVE_BASE), -m, 0.0)
    t = jnp.where(i == j, 1.0, 0.0) + p
    for _ in range(int(math.log2(SOLVE_BASE)) - 1):
        p = _dot(p, p)
        t = t + _dot(t, p)
    s = SOLVE_BASE
    while s < CHUNK_A:
        off = jnp.where(same(2 * s) & jnp.logical_not(same(s)), m, 0.0)
        t = t - _dot(t, _dot(off, t))
        s *= 2
    return t


def _mixer_direction(rev, use_rope, has_init, emit_state, tb, j, nblk,
                     qkv_ref, prev_ref, next_ref, ga_ref, qkb_ref, vb_ref, gb_ref, ab_ref,
                     convw_ref, par_ref, naw_ref, gnw_ref, gnb_ref, rope_ref, sd0_ref, sr0_ref,
                     o_ref, sd_ref, sr_ref, sa_scr, sr_scr, of_scr, xe_scr):
    TB = TIME_BLOCK
    d = 1 if rev else 0

    @pl.when(j == 0)
    def _():
        if has_init:
            sa_scr[...] = sd0_ref[0, 0, 0]
            for h in range(H_B):
                lo = (h % 2) * DK_B
                sr_scr[h] = jnp.zeros((LANES, DV_B), F32)
                sr_scr[h, lo:lo + DK_B, :] = sr0_ref[0, 0, 0, h]
        else:
            sa_scr[...] = jnp.zeros(sa_scr.shape, F32)
            sr_scr[...] = jnp.zeros(sr_scr.shape, F32)

    prev = jnp.where(tb > 0, prev_ref[0], 0.0)
    nxt = jnp.where(tb < nblk - 1, next_ref[0], 0.0)
    xe_scr[0:HALO, :] = prev
    xe_scr[HALO:HALO + TB, :] = qkv_ref[0]
    xe_scr[HALO + TB:HALO + TB + HALO, :] = nxt
    pad = (CONV_K - 1) // 2
    acc = None
    for kk in range(CONV_K):
        off = HALO - pad + kk
        term = xe_scr[off:off + TB, :] * convw_ref[kk:kk + 1, :]
        acc = term if acc is None else acc + term
    qkv = _silu(acc)

    def l2n(x):
        return x * lax.rsqrt(jnp.sum(x * x, -1, keepdims=True) + 1e-6)

    qa = [l2n(qkv[:, h * DK_A:(h + 1) * DK_A]) * (DK_A ** -0.5) for h in range(H_A)]
    ka = [l2n(qkv[:, D_A + h * DK_A:D_A + (h + 1) * DK_A]) for h in range(H_A)]
    va = [qkv[:, 2 * D_A + h * DV_A:2 * D_A + (h + 1) * DV_A] for h in range(H_A)]

    ab = ab_ref[0]
    a_log = par_ref[0:1, :]
    dt_b = par_ref[1:2, :]
    z = ab + dt_b
    softplus = jnp.maximum(z, 0.0) + jnp.log1p(jnp.exp(-jnp.abs(z)))
    g_all = -jnp.exp(a_log) * softplus
    beta_all = 1.0 / (1.0 + jnp.exp(-ab))

    ri = lax.broadcasted_iota(jnp.int32, (TB, TB), 0)
    ci = lax.broadcasted_iota(jnp.int32, (TB, TB), 1)
    same = (ri // CHUNK_A) == (ci // CHUNK_A)
    cum = jnp.where(same & ((ci >= ri) if rev else (ci <= ri)), 1.0, 0.0).astype(BF16)
    gc = None
    for part in _split3(g_all):
        t_ = jnp.dot(cum, part, preferred_element_type=F32)
        gc = t_ if gc is None else gc + t_
    gct = None
    for part in _split3(g_all.T):
        t_ = lax.dot_general(part, cum, (((1,), (1,)), ((), ())), preferred_element_type=F32)
        gct = t_ if gct is None else gct + t_

    i64 = lax.broadcasted_iota(jnp.int32, (CHUNK_A, CHUNK_A), 0)
    j64 = lax.broadcasted_iota(jnp.int32, (CHUNK_A, CHUNK_A), 1)
    tri = (i64 <= j64) if rev else (i64 >= j64)
    strict = (i64 < j64) if rev else (i64 > j64)

    row0 = pl.multiple_of(tb * TB, TB)
    nca = TB // CHUNK_A
    order_a = range(nca - 1, -1, -1) if rev else range(nca)

    def finish_a(r0, h, o_bwd):
        o = of_scr[pl.ds(row0 + r0, CHUNK_A), h * DV_A:(h + 1) * DV_A] + o_bwd
        o = o * lax.rsqrt(jnp.mean(o * o, -1, keepdims=True) + 1e-6) * naw_ref[...]
        gate = ga_ref[0, r0:r0 + CHUNK_A, h * DV_A:(h + 1) * DV_A]
        o_ref[0, r0:r0 + CHUNK_A, h * DV_A:(h + 1) * DV_A] = (o * _silu(gate)).astype(BF16)

    for n in order_a:
        r0 = n * CHUNK_A
        last = r0 if rev else r0 + CHUNK_A - 1
        for h in range(H_A):
            col = d * H_A + h
            q_h = qa[h][r0:r0 + CHUNK_A]
            k_h = ka[h][r0:r0 + CHUNK_A]
            v_h = va[h][r0:r0 + CHUNK_A]
            gcc = gc[r0:r0 + CHUNK_A, col:col + 1]
            gcr = gct[col:col + 1, r0:r0 + CHUNK_A]
            tot = gct[col:col + 1, last:last + 1]
            bcol = beta_all[r0:r0 + CHUNK_A, 2 * H_A + col:2 * H_A + col + 1]
            decay = jnp.where(tri, jnp.exp(jnp.where(tri, gcc - gcr, 0.0)), 0.0)
            kb = k_h * bcol
            m = jnp.where(strict, _dot_nt(kb, k_h) * decay, 0.0)
            eg = jnp.exp(gcc)
            sol = _dot(_unit_triangular_inverse(m), jnp.concatenate([v_h * bcol, kb * eg], -1))
            u = sol[:, :DV_A]
            w = sol[:, DV_A:]
            qk = jnp.where(tri, _dot_nt(q_h, k_h) * decay, 0.0)
            q_dec = q_h * eg
            k_dec = k_h * jnp.exp(tot - gcc)
            s = sa_scr[h]
            v_new = u - _dot(w, s)
            o = _dot(q_dec, s) + _dot(qk, v_new)
            sa_scr[h] = s * jnp.exp(tot) + _dot_tn(k_dec, v_new)
            if rev:
                finish_a(r0, h, o)
            else:
                of_scr[pl.ds(row0 + r0, CHUNK_A), h * DV_A:(h + 1) * DV_A] = o

    qkb = qkb_ref[0]
    if use_rope:
        cos_t = rope_ref[0]
        sin_up = rope_ref[1]
        sin_dn = rope_ref[2]
        cols = []
        for c4 in range(4):
            x = qkb[:, c4 * LANES:(c4 + 1) * LANES]
            cols.append(x * cos_t + pltpu.roll(x, LANES - DK_B // 2, 1) * sin_up
                        + pltpu.roll(x, DK_B // 2, 1) * sin_dn)
        qkb = jnp.concatenate(cols, -1)
    lane = lax.broadcasted_iota(jnp.int32, (CHUNK_B, LANES), 1)
    pi = lax.broadcasted_iota(jnp.int32, (CHUNK_B, CHUNK_B), 0)
    pj = lax.broadcasted_iota(jnp.int32, (CHUNK_B, CHUNK_B), 1)
    pos = lax.broadcasted_iota(jnp.int32, (CHUNK_B, 1), 0)
    if rev:
        pi, pj, pos = CHUNK_B - 1 - pi, CHUNK_B - 1 - pj, CHUNK_B - 1 - pos
    pdiff = (pi - pj).astype(F32)
    posf = pos.astype(F32)

    def finish_b(r0, h, o_bwd):
        c0 = D_A + h * DV_B
        o = of_scr[pl.ds(row0 + r0, CHUNK_B), c0:c0 + DV_B] + o_bwd
        o = _layer_norm(o, 1e-5) * gnw_ref[:, h * DV_B:(h + 1) * DV_B] + gnb_ref[:, h * DV_B:(h + 1) * DV_B]
        gate = gb_ref[0, r0:r0 + CHUNK_B, h * DV_B:(h + 1) * DV_B]
        o_ref[0, r0:r0 + CHUNK_B, c0:c0 + DV_B] = (o * _silu(gate)).astype(BF16)

    ncb = TB // CHUNK_B
    order_b = range(ncb - 1, -1, -1) if rev else range(ncb)
    for h in range(H_B):
        lg = _LOG_GAMMA[H_B - 1 - h] if rev else _LOG_GAMMA[h]
        d_mask = jnp.where(pdiff >= 0, jnp.exp(lg * jnp.maximum(pdiff, 0.0)), 0.0)
        xi = jnp.exp(lg * (posf + 1.0))
        zeta = jnp.exp(lg * (CHUNK_B - 1.0 - posf))
        g_chunk = math.exp(lg * CHUNK_B)
        grp = (h // 2) * LANES
        in_head = (lane // DK_B) == (h % 2)
        for n in order_b:
            r0 = n * CHUNK_B
            q_h = jnp.where(in_head, qkb[r0:r0 + CHUNK_B, grp:grp + LANES], 0.0) * (DK_B ** -0.5)
            k_h = jnp.where(in_head, qkb[r0:r0 + CHUNK_B, H_B * DK_B + grp:H_B * DK_B + grp + LANES], 0.0)
            v_h = vb_ref[0, r0:r0 + CHUNK_B, h * DV_B:(h + 1) * DV_B]
            scores = _dot_nt(q_h, k_h) * d_mask
            s = sr_scr[h]
            o = _dot(scores, v_h) + _dot(q_h * xi, s)
            sr_scr[h] = s * g_chunk + _dot_tn(k_h * zeta, v_h)
            if rev:
                finish_b(r0, h, o)
            else:
                of_scr[pl.ds(row0 + r0, CHUNK_B), D_A + h * DV_B:D_A + (h + 1) * DV_B] = o

    if emit_state:
        @pl.when(j == nblk - 1)
        def _():
            sd_ref[0, 0, 0] = sa_scr[...]
            for h in range(H_B):
                lo = (h % 2) * DK_B
                sr_ref[0, 0, 0, h] = sr_scr[h, lo:lo + DK_B, :]


def _mixer_kernel(use_rope, has_init, emit_state, nblk, *refs):
    ph = pl.program_id(1)
    j = pl.program_id(2)
    n_in = 13 + (1 if use_rope else 0) + (2 if has_init else 0)
    ins = list(refs[:n_in])
    rest = refs[n_in:]
    (qkv_ref, prev_ref, next_ref, ga_ref, qkb_ref, vb_ref, gb_ref, ab_ref,
     convw_ref, par_ref, naw_ref, gnw_ref, gnb_ref) = ins[:13]
    pos_ = 13
    rope_ref = None
    if use_rope:
        rope_ref = ins[pos_]
        pos_ += 1
    sd0_ref = sr0_ref = None
    if has_init:
        sd0_ref, sr0_ref = ins[pos_], ins[pos_ + 1]
    o_ref = rest[0]
    k_ = 1
    sd_ref = sr_ref = None
    if emit_state:
        sd_ref, sr_ref = rest[1], rest[2]
        k_ = 3
    sa_scr, sr_scr, of_scr, xe_scr = rest[k_:k_ + 4]

    for rev in (False, True):
        @pl.when(ph == (1 if rev else 0))
        def _(rev=rev):
            tb = (nblk - 1 - j) if rev else j
            _mixer_direction(rev, use_rope, has_init, emit_state, tb, j, nblk,
                             qkv_ref, prev_ref, next_ref, ga_ref, qkb_ref, vb_ref, gb_ref, ab_ref,
                             convw_ref, par_ref, naw_ref, gnw_ref, gnb_ref, rope_ref, sd0_ref, sr0_ref,
                             o_ref, sd_ref, sr_ref, sa_scr, sr_scr, of_scr, xe_scr)


def _mixer(p, convw, par, naw, gnw, gnb, rope, sd0, sr0, emit_state):
    bt, t, _ = p.shape
    TB = TIME_BLOCK
    nblk = t // TB
    use_rope = rope is not None
    has_init = sd0 is not None
    hpb = TB // HALO
    nh = t // HALO

    def tb_of(ph, j):
        return j + ph * (nblk - 1 - 2 * j)

    def tbo_of(ph, j):
        return (nblk - 1) - ph * j

    const2 = lambda b, ph, j: (0, 0)
    in_specs = [
        pl.BlockSpec((1, TB, QKV_W), lambda b, ph, j: (b, tb_of(ph, j), 0)),
        pl.BlockSpec((1, HALO, QKV_W), lambda b, ph, j: (b, jnp.maximum(tb_of(ph, j) * hpb - 1, 0), 0)),
        pl.BlockSpec((1, HALO, QKV_W), lambda b, ph, j: (b, jnp.minimum((tb_of(ph, j) + 1) * hpb, nh - 1), 0)),
        pl.BlockSpec((1, TB, D_A), lambda b, ph, j: (b, tbo_of(ph, j), COL_GATE_A // D_A)),
        pl.BlockSpec((1, TB, 2 * H_B * DK_B), lambda b, ph, j: (b, tb_of(ph, j), COL_QK_B // (2 * H_B * DK_B))),
        pl.BlockSpec((1, TB, D_B), lambda b, ph, j: (b, tb_of(ph, j), COL_V_B // D_B)),
        pl.BlockSpec((1, TB, D_B), lambda b, ph, j: (b, tbo_of(ph, j), COL_GATE_B // D_B)),
        pl.BlockSpec((1, TB, LANES), lambda b, ph, j: (b, tb_of(ph, j), COL_AB // LANES)),
        pl.BlockSpec((SUBLANES, QKV_W), const2),
        pl.BlockSpec((SUBLANES, LANES), const2),
        pl.BlockSpec((1, DV_A), const2),
        pl.BlockSpec((1, D_B), const2),
        pl.BlockSpec((1, D_B), const2),
    ]
    args = [p, p, p, p, p, p, p, p, convw, par, naw, gnw, gnb]
    if use_rope:
        in_specs.append(pl.BlockSpec((3, TB, LANES), lambda b, ph, j: (0, tb_of(ph, j), 0)))
        args.append(rope)
    if has_init:
        in_specs.append(pl.BlockSpec((1, 1, 1, H_A, DK_A, DV_A), lambda b, ph, j: (b, 0, ph, 0, 0, 0)))
        in_specs.append(pl.BlockSpec((1, 1, 1, H_B, DK_B, DV_B), lambda b, ph, j: (b, 0, ph, 0, 0, 0)))
        args += [sd0, sr0]
    out_specs = [pl.BlockSpec((1, TB, D_A + D_B), lambda b, ph, j: (b, tbo_of(ph, j), 0))]
    out_shape = [jax.ShapeDtypeStruct((bt, t, D_A + D_B), BF16)]
    if emit_state:
        out_specs.append(pl.BlockSpec((1, 1, 1, H_A, DK_A, DV_A), lambda b, ph, j: (b, 0, ph, 0, 0, 0)))
        out_specs.append(pl.BlockSpec((1, 1, 1, H_B, DK_B, DV_B), lambda b, ph, j: (b, 0, ph, 0, 0, 0)))
        out_shape.append(jax.ShapeDtypeStruct((bt, DEPTH, 2, H_A, DK_A, DV_A), F32))
        out_shape.append(jax.ShapeDtypeStruct((bt, DEPTH, 2, H_B, DK_B, DV_B), F32))
    scratch = [pltpu.VMEM((H_A, DK_A, DV_A), F32),
               pltpu.VMEM((H_B, LANES, DV_B), F32),
               pltpu.VMEM((t, D_A + D_B), F32),
               pltpu.VMEM((TB + 2 * HALO, QKV_W), F32)]
    res = pl.pallas_call(
        functools.partial(_mixer_kernel, use_rope, has_init, emit_state, nblk),
        grid=(bt, 2, nblk),
        in_specs=in_specs,
        out_specs=out_specs,
        out_shape=out_shape,
        scratch_shapes=scratch,
        compiler_params=pltpu.CompilerParams(dimension_semantics=("arbitrary", "arbitrary", "arbitrary"),
                                             vmem_limit_bytes=VMEM_LIMIT),
        name="mixer",
    )(*args)
    return res


def _rope_tables(t):
    rows = t // GRID_W
    r = np.repeat(np.arange(rows, dtype=np.float32), GRID_W)
    col = np.tile(np.arange(GRID_W, dtype=np.float32), rows)
    nf = DK_B // 4
    inv = (np.float32(ROPE_BASE) ** (-np.arange(nf, dtype=np.float32) / np.float32(nf))).astype(np.float32)
    ang = jnp.asarray(np.concatenate([r[:, None] * inv, col[:, None] * inv], -1).astype(np.float32))
    cos, sin = jnp.cos(ang), jnp.sin(ang)
    zero = jnp.zeros_like(sin)
    cos_t = jnp.tile(cos, (1, 4))
    sin_up = jnp.tile(jnp.concatenate([-sin, zero], -1), (1, 2))
    sin_dn = jnp.tile(jnp.concatenate([zero, sin], -1), (1, 2))
    return jnp.stack([cos_t, sin_up, sin_dn], 0)


def kernel(x_prompt, x_sample, c, state_delta, state_ret, c_ctx, w_mod, b_mod, w_in, conv_w, a_log, dt_bias,
           norm_a_w, gn_w, gn_b, w_o, ln1_w, ln1_b, w_ff1, b_ff1, w_ff2, b_ff2, ln2_w, ln2_b):
    assert w_mod.shape[0] == DEPTH == 1
    n_dec = c.shape[0]
    rows = -(-(1 + n_dec) // SUBLANES) * SUBLANES
    cond = jnp.zeros((rows, D_MODEL), F32).at[0].set(c_ctx).at[1:1 + n_dec].set(c)
    mod = _modulation(cond, w_mod[0], b_mod[0])
    mod3 = mod.reshape(rows, 1, 6 * D_MODEL)

    w = w_in[0]
    n_ab = 4 * H_A
    c_ab = QKV_W + D_A
    w_in_p = jnp.concatenate([w[:, :c_ab], w[:, c_ab + n_ab:], w[:, c_ab:c_ab + n_ab],
                              jnp.zeros((D_MODEL, LANES - n_ab), F32)], -1).astype(BF16)
    convw = jnp.zeros((SUBLANES, QKV_W), F32).at[:CONV_K].set(conv_w[0])
    par = (jnp.zeros((SUBLANES, LANES), F32).at[0, :2 * H_A].set(a_log[0].reshape(-1))
           .at[1, :2 * H_A].set(dt_bias[0].reshape(-1)))
    naw = norm_a_w[0].reshape(1, DV_A)
    gnw = gn_w[0].reshape(1, D_B)
    gnb = gn_b[0].reshape(1, D_B)
    wo = w_o[0].astype(BF16)
    w1 = w_ff1[0].astype(BF16)
    w2 = w_ff2[0].astype(BF16)
    row = lambda v: v[0].reshape(1, -1)

    def trunk(x, row0, row_stride, rope, sd0, sr0, emit_state, tm):
        p = _projection(x, mod3, row0, row_stride, w_in_p, tm)
        res = _mixer(p, convw, par, naw, gnw, gnb, rope, sd0, sr0, emit_state)
        y = _post(res[0], x, mod3, row0, row_stride, wo, row(ln1_w), row(ln1_b), w1, row(b_ff1), w2, row(b_ff2),
                  row(ln2_w), row(ln2_b), tm)
        return y, res[1:]

    y_prompt, (new_sd, new_sr) = trunk(x_prompt, 0, 0, None, None, None, True, 256)
    y_sample, _ = trunk(x_sample, 1, 1, _rope_tables(x_sample.shape[1]), state_delta, state_ret, False, 512)
    return y_prompt, y_sample, new_sd, new_sr
```

```python
import functools
import math

import jax
import jax.numpy as jnp
import numpy as np
from jax import lax
from jax.experimental import pallas as pl
from jax.experimental.pallas import tpu as pltpu

F32 = jnp.float32
BF16 = jnp.bfloat16

D_MODEL = 1024
H_A, DK_A, DV_A = 4, 128, 128
D_A = H_A * DV_A
CONV_K = 5
CHUNK_A = 64
H_B, DK_B, DV_B = 4, 64, 128
D_B = H_B * DV_B
CHUNK_B = 128
GRID_W = 64
ROPE_BASE = 10000.0
D_FF = 4 * D_MODEL
DEPTH = 1
ALPHA = (2.0 * DEPTH) ** 0.25

LANES = 128
SUBLANES = 8
VMEM_LIMIT = 56 * 1024 * 1024

QKV_W = 3 * D_A
COL_GATE_A = QKV_W
COL_QK_B = COL_GATE_A + D_A
COL_V_B = COL_QK_B + 2 * H_B * DK_B
COL_GATE_B = COL_V_B + D_B
COL_AB = COL_GATE_B + D_B
P_W = COL_AB + LANES

TIME_BLOCK = 256
HALO = SUBLANES


def _dot(a, b):
    return jnp.dot(a.astype(BF16), b.astype(BF16), preferred_element_type=F32)


def _dot_nt(a, b):
    return lax.dot_general(a.astype(BF16), b.astype(BF16), (((1,), (1,)), ((), ())),
                           preferred_element_type=F32)


def _dot_tn(a, b):
    return lax.dot_general(a.astype(BF16), b.astype(BF16), (((0,), (0,)), ((), ())),
                           preferred_element_type=F32)


def _split3(x):
    hi = x.astype(BF16)
    r = x - hi.astype(F32)
    mid = r.astype(BF16)
    lo = (r - mid.astype(F32)).astype(BF16)
    return hi, mid, lo


def _silu(x):
    return x * (1.0 / (1.0 + jnp.exp(-x)))


def _layer_norm(x, eps):
    mu = jnp.mean(x, -1, keepdims=True)
    xc = x - mu
    var = jnp.mean(xc * xc, -1, keepdims=True)
    return xc * lax.rsqrt(var + eps)


def _mod_kernel(c_ref, w_ref, b_ref, o_ref):
    o_ref[...] = _dot(_silu(c_ref[...]), w_ref[...]) + b_ref[...]


def _modulation(cond, w_mod, b_mod):
    rows = cond.shape[0]
    n = w_mod.shape[1]
    bn = 512
    return pl.pallas_call(
        _mod_kernel,
        grid=(n // bn,),
        in_specs=[pl.BlockSpec((rows, D_MODEL), lambda j: (0, 0)),
                  pl.BlockSpec((D_MODEL, bn), lambda j: (0, j)),
                  pl.BlockSpec((1, bn), lambda j: (0, j))],
        out_specs=pl.BlockSpec((rows, bn), lambda j: (0, j)),
        out_shape=jax.ShapeDtypeStruct((rows, n), F32),
        compiler_params=pltpu.CompilerParams(dimension_semantics=("arbitrary",)),
        name="modulation",
    )(cond, w_mod, b_mod.reshape(1, n))


def _proj_kernel(x_ref, mod_ref, w_ref, o_ref):
    m = mod_ref[0]
    sh1 = m[:, 0:D_MODEL]
    sc1 = m[:, D_MODEL:2 * D_MODEL]
    h = _layer_norm(x_ref[0], 1e-6) * (1.0 + sc1) + sh1
    o_ref[0] = jnp.dot(h.astype(BF16), w_ref[...], preferred_element_type=F32)


def _projection(x, mod3, row0, row_stride, w_in_p, tm):
    bt, t, _ = x.shape
    return pl.pallas_call(
        _proj_kernel,
        grid=(bt, t // tm),
        in_specs=[pl.BlockSpec((1, tm, D_MODEL), lambda b, i: (b, i, 0)),
                  pl.BlockSpec((1, 1, 6 * D_MODEL), lambda b, i: (row0 + row_stride * b, 0, 0)),
                  pl.BlockSpec((D_MODEL, P_W), lambda b, i: (0, 0))],
        out_specs=pl.BlockSpec((1, tm, P_W), lambda b, i: (b, i, 0)),
        out_shape=jax.ShapeDtypeStruct((bt, t, P_W), F32),
        compiler_params=pltpu.CompilerParams(dimension_semantics=("arbitrary", "arbitrary"),
                                             vmem_limit_bytes=VMEM_LIMIT),
        name="projection",
    )(x, mod3, w_in_p)


def _post_kernel(o_ref, x_ref, mod_ref, wo_ref, ln1w_ref, ln1b_ref, w1_ref, b1_ref, w2_ref, b2_ref,
                 ln2w_ref, ln2b_ref, y_ref):
    m = mod_ref[0]
    g1 = m[:, 2 * D_MODEL:3 * D_MODEL]
    sh2 = m[:, 3 * D_MODEL:4 * D_MODEL]
    sc2 = m[:, 4 * D_MODEL:5 * D_MODEL]
    g2 = m[:, 5 * D_MODEL:6 * D_MODEL]
    y = jnp.dot(o_ref[0], wo_ref[...], preferred_element_type=F32)
    x1 = _layer_norm(ALPHA * x_ref[0] + g1 * y, 1e-6) * ln1w_ref[...] + ln1b_ref[...]
    h = _layer_norm(x1, 1e-6) * (1.0 + sc2) + sh2
    a = jnp.dot(h.astype(BF16), w1_ref[...], preferred_element_type=F32) + b1_ref[...]
    a = jnp.square(jnp.maximum(a, 0.0))
    f = jnp.dot(a.astype(BF16), w2_ref[...], preferred_element_type=F32) + b2_ref[...]
    y_ref[0] = _layer_norm(ALPHA * x1 + g2 * f, 1e-6) * ln2w_ref[...] + ln2b_ref[...]


def _post(o, x, mod3, row0, row_stride, wo, ln1w, ln1b, w1, b1, w2, b2, ln2w, ln2b, tm):
    bt, t, _ = x.shape
    const = lambda b, i: (0, 0)
    return pl.pallas_call(
        _post_kernel,
        grid=(bt, t // tm),
        in_specs=[pl.BlockSpec((1, tm, D_MODEL), lambda b, i: (b, i, 0)),
                  pl.BlockSpec((1, tm, D_MODEL), lambda b, i: (b, i, 0)),
                  pl.BlockSpec((1, 1, 6 * D_MODEL), lambda b, i: (row0 + row_stride * b, 0, 0)),
                  pl.BlockSpec((D_MODEL, D_MODEL), const),
                  pl.BlockSpec((1, D_MODEL), const),
                  pl.BlockSpec((1, D_MODEL), const),
                  pl.BlockSpec((D_MODEL, D_FF), const),
                  pl.BlockSpec((1, D_FF), const),
                  pl.BlockSpec((D_FF, D_MODEL), const),
                  pl.BlockSpec((1, D_MODEL), const),
                  pl.BlockSpec((1, D_MODEL), const),
                  pl.BlockSpec((1, D_MODEL), const)],
        out_specs=pl.BlockSpec((1, tm, D_MODEL), lambda b, i: (b, i, 0)),
        out_shape=jax.ShapeDtypeStruct((bt, t, D_MODEL), F32),
        compiler_params=pltpu.CompilerParams(dimension_semantics=("arbitrary", "arbitrary"),
                                             vmem_limit_bytes=VMEM_LIMIT),
        name="post",
    )(o, x, mod3, wo, ln1w, ln1b, w1, b1, w2, b2, ln2w, ln2b)


_LOG_GAMMA = [math.log1p(-(2.0 ** (-5.0 - h))) for h in range(H_B)]


SOLVE_BASE = 8


def _unit_triangular_inverses(ms):
    i = lax.broadcasted_iota(jnp.int32, (CHUNK_A, CHUNK_A), 0)
    j = lax.broadcasted_iota(jnp.int32, (CHUNK_A, CHUNK_A), 1)
    same = lambda s: (i // s) == (j // s)
    eye = jnp.where(i == j, 1.0, 0.0)
    ps = [jnp.where(same(SOLVE_BASE), -m, 0.0) for m in ms]
    ts = [eye + p for p in ps]
    for _ in range(int(math.log2(SOLVE_BASE)) - 1):
        ps = [_dot(p, p) for p in ps]
        ts = [t + _dot(t, p) for t, p in zip(ts, ps)]
    s = SOLVE_BASE
    while s < CHUNK_A:
        sel = same(2 * s) & jnp.logical_not(same(s))
        ys = [_dot(jnp.where(sel, m, 0.0), t) for m, t in zip(ms, ts)]
        ts = [t - _dot(t, y) for t, y in zip(ts, ys)]
        s *= 2
    return ts


def _mixer_direction(rev, use_rope, has_init, emit_state, tb, j, nblk,
                     qkv_ref, prev_ref, next_ref, ga_ref, qkb_ref, vb_ref, gb_ref, ab_ref,
                     convw_ref, par_ref, naw_ref, gnw_ref, gnb_ref, rope_ref, sd0_ref, sr0_ref,
                     o_ref, sd_ref, sr_ref, sa_scr, sr_scr, of_scr, xe_scr):
    TB = TIME_BLOCK
    d = 1 if rev else 0

    @pl.when(j == 0)
    def _():
        if has_init:
            sa_scr[...] = sd0_ref[0, 0, 0]
            for h in range(H_B):
                lo = (h % 2) * DK_B
                sr_scr[h] = jnp.zeros((LANES, DV_B), F32)
                sr_scr[h, lo:lo + DK_B, :] = sr0_ref[0, 0, 0, h]
        else:
            sa_scr[...] = jnp.zeros(sa_scr.shape, F32)
            sr_scr[...] = jnp.zeros(sr_scr.shape, F32)

    prev = jnp.where(tb > 0, prev_ref[0], 0.0)
    nxt = jnp.where(tb < nblk - 1, next_ref[0], 0.0)
    xe_scr[0:HALO, :] = prev
    xe_scr[HALO:HALO + TB, :] = qkv_ref[0]
    xe_scr[HALO + TB:HALO + TB + HALO, :] = nxt
    pad = (CONV_K - 1) // 2
    acc = None
    for kk in range(CONV_K):
        off = HALO - pad + kk
        term = xe_scr[off:off + TB, :] * convw_ref[kk:kk + 1, :]
        acc = term if acc is None else acc + term
    qkv = _silu(acc)

    def l2n(x):
        return x * lax.rsqrt(jnp.sum(x * x, -1, keepdims=True) + 1e-6)

    qa = [l2n(qkv[:, h * DK_A:(h + 1) * DK_A]) * (DK_A ** -0.5) for h in range(H_A)]
    ka = [l2n(qkv[:, D_A + h * DK_A:D_A + (h + 1) * DK_A]) for h in range(H_A)]
    va = [qkv[:, 2 * D_A + h * DV_A:2 * D_A + (h + 1) * DV_A] for h in range(H_A)]

    ab = ab_ref[0]
    a_log = par_ref[0:1, :]
    dt_b = par_ref[1:2, :]
    z = ab + dt_b
    softplus = jnp.maximum(z, 0.0) + jnp.log1p(jnp.exp(-jnp.abs(z)))
    g_all = -jnp.exp(a_log) * softplus
    beta_all = 1.0 / (1.0 + jnp.exp(-ab))

    ri = lax.broadcasted_iota(jnp.int32, (TB, TB), 0)
    ci = lax.broadcasted_iota(jnp.int32, (TB, TB), 1)
    same = (ri // CHUNK_A) == (ci // CHUNK_A)
    cum = jnp.where(same & ((ci >= ri) if rev else (ci <= ri)), 1.0, 0.0).astype(BF16)
    gc = None
    for part in _split3(g_all):
        t_ = jnp.dot(cum, part, preferred_element_type=F32)
        gc = t_ if gc is None else gc + t_
    gct = None
    for part in _split3(g_all.T):
        t_ = lax.dot_general(part, cum, (((1,), (1,)), ((), ())), preferred_element_type=F32)
        gct = t_ if gct is None else gct + t_

    i64 = lax.broadcasted_iota(jnp.int32, (CHUNK_A, CHUNK_A), 0)
    j64 = lax.broadcasted_iota(jnp.int32, (CHUNK_A, CHUNK_A), 1)
    tri = (i64 <= j64) if rev else (i64 >= j64)
    strict = (i64 < j64) if rev else (i64 > j64)

    row0 = pl.multiple_of(tb * TB, TB)
    nca = TB // CHUNK_A
    order_a = range(nca - 1, -1, -1) if rev else range(nca)

    def finish_a(r0, h, o_bwd):
        o = of_scr[pl.ds(row0 + r0, CHUNK_A), h * DV_A:(h + 1) * DV_A] + o_bwd
        o = o * lax.rsqrt(jnp.mean(o * o, -1, keepdims=True) + 1e-6) * naw_ref[...]
        gate = ga_ref[0, r0:r0 + CHUNK_A, h * DV_A:(h + 1) * DV_A]
        o_ref[0, r0:r0 + CHUNK_A, h * DV_A:(h + 1) * DV_A] = (o * _silu(gate)).astype(BF16)

    items = [(n, h) for n in order_a for h in range(H_A)]
    st = []
    for n, h in items:
        r0 = n * CHUNK_A
        last = r0 if rev else r0 + CHUNK_A - 1
        col = d * H_A + h
        it = dict(r0=r0, h=h)
        it["q"] = qa[h][r0:r0 + CHUNK_A]
        it["k"] = ka[h][r0:r0 + CHUNK_A]
        gcc = gc[r0:r0 + CHUNK_A, col:col + 1]
        gcr = gct[col:col + 1, r0:r0 + CHUNK_A]
        it["gcc"] = gcc
        it["tot"] = gct[col:col + 1, last:last + 1]
        bcol = beta_all[r0:r0 + CHUNK_A, 2 * H_A + col:2 * H_A + col + 1]
        it["decay"] = jnp.where(tri, jnp.exp(jnp.where(tri, gcc - gcr, 0.0)), 0.0)
        it["kb"] = it["k"] * bcol
        it["eg"] = jnp.exp(gcc)
        it["rhs"] = jnp.concatenate([va[h][r0:r0 + CHUNK_A] * bcol, it["kb"] * it["eg"]], -1)
        st.append(it)
    for it in st:
        it["m"] = jnp.where(strict, _dot_nt(it["kb"], it["k"]) * it["decay"], 0.0)
    for it in st:
        it["qk"] = jnp.where(tri, _dot_nt(it["q"], it["k"]) * it["decay"], 0.0)
    inv = _unit_triangular_inverses([it["m"] for it in st])
    for it, t in zip(st, inv):
        sol = _dot(t, it["rhs"])
        it["u"] = sol[:, :DV_A]
        it["w"] = sol[:, DV_A:]
    for it in st:
        it["q_dec"] = it["q"] * it["eg"]
        it["k_dec"] = it["k"] * jnp.exp(it["tot"] - it["gcc"])

    for c in range(nca):
        grp = st[c * H_A:(c + 1) * H_A]
        s_old = [sa_scr[it["h"]] for it in grp]
        ws = [_dot(it["w"], s) for it, s in zip(grp, s_old)]
        qs = [_dot(it["q_dec"], s) for it, s in zip(grp, s_old)]
        v_new = [it["u"] - x for it, x in zip(grp, ws)]
        upd = [_dot_tn(it["k_dec"], v) for it, v in zip(grp, v_new)]
        for it, s, x in zip(grp, s_old, upd):
            sa_scr[it["h"]] = s * jnp.exp(it["tot"]) + x
        outs = [x + _dot(it["qk"], v) for it, x, v in zip(grp, qs, v_new)]
        for it, o in zip(grp, outs):
            if rev:
                finish_a(it["r0"], it["h"], o)
            else:
                of_scr[pl.ds(row0 + it["r0"], CHUNK_A), it["h"] * DV_A:(it["h"] + 1) * DV_A] = o

    qkb = qkb_ref[0]
    if use_rope:
        cos_t = rope_ref[0]
        sin_up = rope_ref[1]
        sin_dn = rope_ref[2]
        cols = []
        for c4 in range(4):
            x = qkb[:, c4 * LANES:(c4 + 1) * LANES]
            cols.append(x * cos_t + pltpu.roll(x, LANES - DK_B // 2, 1) * sin_up
                        + pltpu.roll(x, DK_B // 2, 1) * sin_dn)
        qkb = jnp.concatenate(cols, -1)
    lane = lax.broadcasted_iota(jnp.int32, (CHUNK_B, LANES), 1)
    pi = lax.broadcasted_iota(jnp.int32, (CHUNK_B, CHUNK_B), 0)
    pj = lax.broadcasted_iota(jnp.int32, (CHUNK_B, CHUNK_B), 1)
    pos = lax.broadcasted_iota(jnp.int32, (CHUNK_B, 1), 0)
    if rev:
        pi, pj, pos = CHUNK_B - 1 - pi, CHUNK_B - 1 - pj, CHUNK_B - 1 - pos
    pdiff = (pi - pj).astype(F32)
    posf = pos.astype(F32)

    def finish_b(r0, h, o_bwd):
        c0 = D_A + h * DV_B
        o = of_scr[pl.ds(row0 + r0, CHUNK_B), c0:c0 + DV_B] + o_bwd
        o = _layer_norm(o, 1e-5) * gnw_ref[:, h * DV_B:(h + 1) * DV_B] + gnb_ref[:, h * DV_B:(h + 1) * DV_B]
        gate = gb_ref[0, r0:r0 + CHUNK_B, h * DV_B:(h + 1) * DV_B]
        o_ref[0, r0:r0 + CHUNK_B, c0:c0 + DV_B] = (o * _silu(gate)).astype(BF16)

    ncb = TB // CHUNK_B
    order_b = range(ncb - 1, -1, -1) if rev else range(ncb)
    for h in range(H_B):
        lg = _LOG_GAMMA[H_B - 1 - h] if rev else _LOG_GAMMA[h]
        d_mask = jnp.where(pdiff >= 0, jnp.exp(lg * jnp.maximum(pdiff, 0.0)), 0.0)
        xi = jnp.exp(lg * (posf + 1.0))
        zeta = jnp.exp(lg * (CHUNK_B - 1.0 - posf))
        g_chunk = math.exp(lg * CHUNK_B)
        grp = (h // 2) * LANES
        in_head = (lane // DK_B) == (h % 2)
        for n in order_b:
            r0 = n * CHUNK_B
            q_h = jnp.where(in_head, qkb[r0:r0 + CHUNK_B, grp:grp + LANES], 0.0) * (DK_B ** -0.5)
            k_h = jnp.where(in_head, qkb[r0:r0 + CHUNK_B, H_B * DK_B + grp:H_B * DK_B + grp + LANES], 0.0)
            v_h = vb_ref[0, r0:r0 + CHUNK_B, h * DV_B:(h + 1) * DV_B]
            scores = _dot_nt(q_h, k_h) * d_mask
            s = sr_scr[h]
            o = _dot(scores, v_h) + _dot(q_h * xi, s)
            sr_scr[h] = s * g_chunk + _dot_tn(k_h * zeta, v_h)
            if rev:
                finish_b(r0, h, o)
            else:
                of_scr[pl.ds(row0 + r0, CHUNK_B), D_A + h * DV_B:D_A + (h + 1) * DV_B] = o

    if emit_state:
        @pl.when(j == nblk - 1)
        def _():
            sd_ref[0, 0, 0] = sa_scr[...]
            for h in range(H_B):
                lo = (h % 2) * DK_B
                sr_ref[0, 0, 0, h] = sr_scr[h, lo:lo + DK_B, :]


def _mixer_kernel(use_rope, has_init, emit_state, nblk, *refs):
    ph = pl.program_id(1)
    j = pl.program_id(2)
    n_in = 13 + (1 if use_rope else 0) + (2 if has_init else 0)
    ins = list(refs[:n_in])
    rest = refs[n_in:]
    (qkv_ref, prev_ref, next_ref, ga_ref, qkb_ref, vb_ref, gb_ref, ab_ref,
     convw_ref, par_ref, naw_ref, gnw_ref, gnb_ref) = ins[:13]
    pos_ = 13
    rope_ref = None
    if use_rope:
        rope_ref = ins[pos_]
        pos_ += 1
    sd0_ref = sr0_ref = None
    if has_init:
        sd0_ref, sr0_ref = ins[pos_], ins[pos_ + 1]
    o_ref = rest[0]
    k_ = 1
    sd_ref = sr_ref = None
    if emit_state:
        sd_ref, sr_ref = rest[1], rest[2]
        k_ = 3
    sa_scr, sr_scr, of_scr, xe_scr = rest[k_:k_ + 4]

    for rev in (False, True):
        @pl.when(ph == (1 if rev else 0))
        def _(rev=rev):
            tb = (nblk - 1 - j) if rev else j
            _mixer_direction(rev, use_rope, has_init, emit_state, tb, j, nblk,
                             qkv_ref, prev_ref, next_ref, ga_ref, qkb_ref, vb_ref, gb_ref, ab_ref,
                             convw_ref, par_ref, naw_ref, gnw_ref, gnb_ref, rope_ref, sd0_ref, sr0_ref,
                             o_ref, sd_ref, sr_ref, sa_scr, sr_scr, of_scr, xe_scr)


def _mixer(p, convw, par, naw, gnw, gnb, rope, sd0, sr0, emit_state):
    bt, t, _ = p.shape
    TB = TIME_BLOCK
    nblk = t // TB
    use_rope = rope is not None
    has_init = sd0 is not None
    hpb = TB // HALO
    nh = t // HALO

    def tb_of(ph, j):
        return j + ph * (nblk - 1 - 2 * j)

    def tbo_of(ph, j):
        return (nblk - 1) - ph * j

    const2 = lambda b, ph, j: (0, 0)
    in_specs = [
        pl.BlockSpec((1, TB, QKV_W), lambda b, ph, j: (b, tb_of(ph, j), 0)),
        pl.BlockSpec((1, HALO, QKV_W), lambda b, ph, j: (b, jnp.maximum(tb_of(ph, j) * hpb - 1, 0), 0)),
        pl.BlockSpec((1, HALO, QKV_W), lambda b, ph, j: (b, jnp.minimum((tb_of(ph, j) + 1) * hpb, nh - 1), 0)),
        pl.BlockSpec((1, TB, D_A), lambda b, ph, j: (b, tbo_of(ph, j), COL_GATE_A // D_A)),
        pl.BlockSpec((1, TB, 2 * H_B * DK_B), lambda b, ph, j: (b, tb_of(ph, j), COL_QK_B // (2 * H_B * DK_B))),
        pl.BlockSpec((1, TB, D_B), lambda b, ph, j: (b, tb_of(ph, j), COL_V_B // D_B)),
        pl.BlockSpec((1, TB, D_B), lambda b, ph, j: (b, tbo_of(ph, j), COL_GATE_B // D_B)),
        pl.BlockSpec((1, TB, LANES), lambda b, ph, j: (b, tb_of(ph, j), COL_AB // LANES)),
        pl.BlockSpec((SUBLANES, QKV_W), const2),
        pl.BlockSpec((SUBLANES, LANES), const2),
        pl.BlockSpec((1, DV_A), const2),
        pl.BlockSpec((1, D_B), const2),
        pl.BlockSpec((1, D_B), const2),
    ]
    args = [p, p, p, p, p, p, p, p, convw, par, naw, gnw, gnb]
    if use_rope:
        in_specs.append(pl.BlockSpec((3, TB, LANES), lambda b, ph, j: (0, tb_of(ph, j), 0)))
        args.append(rope)
    if has_init:
        in_specs.append(pl.BlockSpec((1, 1, 1, H_A, DK_A, DV_A), lambda b, ph, j: (b, 0, ph, 0, 0, 0)))
        in_specs.append(pl.BlockSpec((1, 1, 1, H_B, DK_B, DV_B), lambda b, ph, j: (b, 0, ph, 0, 0, 0)))
        args += [sd0, sr0]
    out_specs = [pl.BlockSpec((1, TB, D_A + D_B), lambda b, ph, j: (b, tbo_of(ph, j), 0))]
    out_shape = [jax.ShapeDtypeStruct((bt, t, D_A + D_B), BF16)]
    if emit_state:
        out_specs.append(pl.BlockSpec((1, 1, 1, H_A, DK_A, DV_A), lambda b, ph, j: (b, 0, ph, 0, 0, 0)))
        out_specs.append(pl.BlockSpec((1, 1, 1, H_B, DK_B, DV_B), lambda b, ph, j: (b, 0, ph, 0, 0, 0)))
        out_shape.append(jax.ShapeDtypeStruct((bt, DEPTH, 2, H_A, DK_A, DV_A), F32))
        out_shape.append(jax.ShapeDtypeStruct((bt, DEPTH, 2, H_B, DK_B, DV_B), F32))
    scratch = [pltpu.VMEM((H_A, DK_A, DV_A), F32),
               pltpu.VMEM((H_B, LANES, DV_B), F32),
               pltpu.VMEM((t, D_A + D_B), F32),
               pltpu.VMEM((TB + 2 * HALO, QKV_W), F32)]
    res = pl.pallas_call(
        functools.partial(_mixer_kernel, use_rope, has_init, emit_state, nblk),
        grid=(bt, 2, nblk),
        in_specs=in_specs,
        out_specs=out_specs,
        out_shape=out_shape,
        scratch_shapes=scratch,
        compiler_params=pltpu.CompilerParams(dimension_semantics=("arbitrary", "arbitrary", "arbitrary"),
                                             vmem_limit_bytes=VMEM_LIMIT),
        name="mixer",
    )(*args)
    return res


def _rope_tables(t):
    rows = t // GRID_W
    r = np.repeat(np.arange(rows, dtype=np.float32), GRID_W)
    col = np.tile(np.arange(GRID_W, dtype=np.float32), rows)
    nf = DK_B // 4
    inv = (np.float32(ROPE_BASE) ** (-np.arange(nf, dtype=np.float32) / np.float32(nf))).astype(np.float32)
    ang = jnp.asarray(np.concatenate([r[:, None] * inv, col[:, None] * inv], -1).astype(np.float32))
    cos, sin = jnp.cos(ang), jnp.sin(ang)
    zero = jnp.zeros_like(sin)
    cos_t = jnp.tile(cos, (1, 4))
    sin_up = jnp.tile(jnp.concatenate([-sin, zero], -1), (1, 2))
    sin_dn = jnp.tile(jnp.concatenate([zero, sin], -1), (1, 2))
    return jnp.stack([cos_t, sin_up, sin_dn], 0)


def kernel(x_prompt, x_sample, c, state_delta, state_ret, c_ctx, w_mod, b_mod, w_in, conv_w, a_log, dt_bias,
           norm_a_w, gn_w, gn_b, w_o, ln1_w, ln1_b, w_ff1, b_ff1, w_ff2, b_ff2, ln2_w, ln2_b):
    assert w_mod.shape[0] == DEPTH == 1
    n_dec = c.shape[0]
    rows = -(-(1 + n_dec) // SUBLANES) * SUBLANES
    cond = jnp.zeros((rows, D_MODEL), F32).at[0].set(c_ctx).at[1:1 + n_dec].set(c)
    mod = _modulation(cond, w_mod[0], b_mod[0])
    mod3 = mod.reshape(rows, 1, 6 * D_MODEL)

    w = w_in[0]
    n_ab = 4 * H_A
    c_ab = QKV_W + D_A
    w_in_p = jnp.concatenate([w[:, :c_ab], w[:, c_ab + n_ab:], w[:, c_ab:c_ab + n_ab],
                              jnp.zeros((D_MODEL, LANES - n_ab), F32)], -1).astype(BF16)
    convw = jnp.zeros((SUBLANES, QKV_W), F32).at[:CONV_K].set(conv_w[0])
    par = (jnp.zeros((SUBLANES, LANES), F32).at[0, :2 * H_A].set(a_log[0].reshape(-1))
           .at[1, :2 * H_A].set(dt_bias[0].reshape(-1)))
    naw = norm_a_w[0].reshape(1, DV_A)
    gnw = gn_w[0].reshape(1, D_B)
    gnb = gn_b[0].reshape(1, D_B)
    wo = w_o[0].astype(BF16)
    w1 = w_ff1[0].astype(BF16)
    w2 = w_ff2[0].astype(BF16)
    row = lambda v: v[0].reshape(1, -1)

    def trunk(x, row0, row_stride, rope, sd0, sr0, emit_state, tm):
        p = _projection(x, mod3, row0, row_stride, w_in_p, tm)
        res = _mixer(p, convw, par, naw, gnw, gnb, rope, sd0, sr0, emit_state)
        y = _post(res[0], x, mod3, row0, row_stride, wo, row(ln1_w), row(ln1_b), w1, row(b_ff1), w2, row(b_ff2),
                  row(ln2_w), row(ln2_b), tm)
        return y, res[1:]

    y_prompt, (new_sd, new_sr) = trunk(x_prompt, 0, 0, None, None, None, True, 256)
    y_sample, _ = trunk(x_sample, 1, 1, _rope_tables(x_sample.shape[1]), state_delta, state_ret, False, 512)
    return y_prompt, y_sample, new_sd, new_sr
```

```python
import functools
import math

import jax
import jax.numpy as jnp
import numpy as np
from jax import lax
from jax.experimental import pallas as pl
from jax.experimental.pallas import tpu as pltpu

F32 = jnp.float32
BF16 = jnp.bfloat16

D_MODEL = 1024
H_A, DK_A, DV_A = 4, 128, 128
D_A = H_A * DV_A
CONV_K = 5
CHUNK_A = 64
H_B, DK_B, DV_B = 4, 64, 128
D_B = H_B * DV_B
CHUNK_B = 128
GRID_W = 64
ROPE_BASE = 10000.0
D_FF = 4 * D_MODEL
DEPTH = 1
ALPHA = (2.0 * DEPTH) ** 0.25

LANES = 128
SUBLANES = 8
VMEM_LIMIT = 56 * 1024 * 1024

QKV_W = 3 * D_A
COL_GATE_A = QKV_W
COL_QK_B = COL_GATE_A + D_A
QKB_W = 2 * H_B * DK_B
COL_V_B = COL_QK_B + QKB_W
COL_GATE_B = COL_V_B + D_B
COL_AB = COL_GATE_B + D_B
P_W = COL_AB + LANES
N_AB = 2 * H_A

TIME_BLOCK = 256
HALO = SUBLANES


def _dot(a, b):
    return jnp.dot(a.astype(BF16), b.astype(BF16), preferred_element_type=F32)


def _dot_nt(a, b):
    return lax.dot_general(a.astype(BF16), b.astype(BF16), (((1,), (1,)), ((), ())),
                           preferred_element_type=F32)


def _dot_tn(a, b):
    return lax.dot_general(a.astype(BF16), b.astype(BF16), (((0,), (0,)), ((), ())),
                           preferred_element_type=F32)


def _split3(x):
    hi = x.astype(BF16)
    r = x - hi.astype(F32)
    mid = r.astype(BF16)
    lo = (r - mid.astype(F32)).astype(BF16)
    return hi, mid, lo


def _silu(x):
    return x * (1.0 / (1.0 + jnp.exp(-x)))


def _layer_norm(x, eps):
    mu = jnp.mean(x, -1, keepdims=True)
    xc = x - mu
    var = jnp.mean(xc * xc, -1, keepdims=True)
    return xc * lax.rsqrt(var + eps)


def _resident(shape):
    return pl.BlockSpec(shape, lambda *_: (0,) * len(shape), pipeline_mode=pl.Buffered(1))


def _mod_kernel(c_ref, w_ref, b_ref, o_ref):
    o_ref[...] = _dot(_silu(c_ref[...]), w_ref[...]) + b_ref[...]


def _modulation(cond, w_mod, b_mod):
    rows = cond.shape[0]
    n = w_mod.shape[1]
    bn = 512
    return pl.pallas_call(
        _mod_kernel,
        grid=(n // bn,),
        in_specs=[pl.BlockSpec((rows, D_MODEL), lambda j: (0, 0)),
                  pl.BlockSpec((D_MODEL, bn), lambda j: (0, j)),
                  pl.BlockSpec((1, bn), lambda j: (0, j))],
        out_specs=pl.BlockSpec((rows, bn), lambda j: (0, j)),
        out_shape=jax.ShapeDtypeStruct((rows, n), F32),
        compiler_params=pltpu.CompilerParams(dimension_semantics=("arbitrary",)),
        name="modulation",
    )(cond, w_mod, b_mod.reshape(1, n))


def _proj_kernel(use_rope, tm, nt, *refs):
    if use_rope:
        (x_ref, xp_ref, xn_ref, mod_ref, w_ref, convw_ref, par_ref, rope_ref,
         qkv_ref, qkb_ref, vb_ref, gates_ref, gb_ref, xe_scr) = refs
    else:
        (x_ref, xp_ref, xn_ref, mod_ref, w_ref, convw_ref, par_ref,
         qkv_ref, qkb_ref, vb_ref, gates_ref, gb_ref, xe_scr) = refs
    i = pl.program_id(1)
    m = mod_ref[0]
    sh1 = m[:, 0:D_MODEL]
    sc1 = m[:, D_MODEL:2 * D_MODEL]

    def modulated(x):
        return _layer_norm(x, 1e-6) * (1.0 + sc1) + sh1

    h = jnp.concatenate([modulated(xp_ref[0]), modulated(x_ref[0]), modulated(xn_ref[0])], 0)
    p = jnp.dot(h.astype(BF16), w_ref[...], preferred_element_type=F32)

    xe_scr[0:HALO, :] = jnp.where(i > 0, p[0:HALO, 0:QKV_W], 0.0)
    xe_scr[HALO:HALO + tm, :] = p[HALO:HALO + tm, 0:QKV_W]
    xe_scr[HALO + tm:, :] = jnp.where(i < nt - 1, p[HALO + tm:, 0:QKV_W], 0.0)
    pad = (CONV_K - 1) // 2
    acc = None
    for kk in range(CONV_K):
        off = HALO - pad + kk
        term = xe_scr[off:off + tm, :] * convw_ref[kk:kk + 1, :]
        acc = term if acc is None else acc + term
    qkv = _silu(acc)

    def l2n(x):
        return x * lax.rsqrt(jnp.sum(x * x, -1, keepdims=True) + 1e-6)

    for hd in range(H_A):
        c = hd * DK_A
        qkv_ref[0, :, c:c + DK_A] = l2n(qkv[:, c:c + DK_A]) * (DK_A ** -0.5)
        qkv_ref[0, :, D_A + c:D_A + c + DK_A] = l2n(qkv[:, D_A + c:D_A + c + DK_A])
    qkv_ref[0, :, 2 * D_A:] = qkv[:, 2 * D_A:]

    rest = p[HALO:HALO + tm]
    gates_ref[0, :, 0:D_A] = _silu(rest[:, COL_GATE_A:COL_GATE_A + D_A])
    gates_ref[0, :, D_A:] = _silu(rest[:, COL_GATE_B:COL_GATE_B + D_B])
    vb_ref[0] = rest[:, COL_V_B:COL_V_B + D_B].astype(BF16)

    for c4 in range(QKB_W // LANES):
        x = rest[:, COL_QK_B + c4 * LANES:COL_QK_B + (c4 + 1) * LANES]
        if use_rope:
            x = (x * rope_ref[0] + pltpu.roll(x, LANES - DK_B // 2, 1) * rope_ref[1]
                 + pltpu.roll(x, DK_B // 2, 1) * rope_ref[2])
        if c4 < H_B * DK_B // LANES:
            x = x * (DK_B ** -0.5)
        qkb_ref[0, :, c4 * LANES:(c4 + 1) * LANES] = x

    ab = rest[:, COL_AB:COL_AB + LANES]
    z = ab + par_ref[1:2, :]
    softplus = jnp.maximum(z, 0.0) + jnp.log1p(jnp.exp(-jnp.abs(z)))
    g_all = -jnp.exp(par_ref[0:1, :]) * softplus
    beta_all = 1.0 / (1.0 + jnp.exp(-ab))
    lane = lax.broadcasted_iota(jnp.int32, (tm, LANES), 1)
    gb_ref[0] = jnp.where(lane < N_AB, g_all, jnp.where(lane < 2 * N_AB, beta_all, 0.0))


def _projection(x, mod3, row0, row_stride, w_in_p, convw, par, rope, tm):
    bt, t, _ = x.shape
    nt = t // tm
    hpt = tm // HALO
    nh = t // HALO
    use_rope = rope is not None
    in_specs = [pl.BlockSpec((1, tm, D_MODEL), lambda b, i: (b, i, 0)),
                pl.BlockSpec((1, HALO, D_MODEL), lambda b, i: (b, jnp.maximum(i * hpt - 1, 0), 0)),
                pl.BlockSpec((1, HALO, D_MODEL), lambda b, i: (b, jnp.minimum((i + 1) * hpt, nh - 1), 0)),
                pl.BlockSpec((1, 1, 6 * D_MODEL), lambda b, i: (row0 + row_stride * b, 0, 0)),
                _resident((D_MODEL, P_W)),
                _resident((SUBLANES, QKV_W)),
                _resident((SUBLANES, LANES))]
    args = [x, x, x, mod3, w_in_p, convw, par]
    if use_rope:
        in_specs.append(pl.BlockSpec((3, tm, LANES), lambda b, i: (0, i, 0)))
        args.append(rope)
    widths = (QKV_W, QKB_W, D_B, D_A + D_B, LANES)
    dtypes = (F32, F32, BF16, F32, F32)
    return pl.pallas_call(
        functools.partial(_proj_kernel, use_rope, tm, nt),
        grid=(bt, nt),
        in_specs=in_specs,
        out_specs=[pl.BlockSpec((1, tm, w), lambda b, i: (b, i, 0)) for w in widths],
        out_shape=[jax.ShapeDtypeStruct((bt, t, w), dt) for w, dt in zip(widths, dtypes)],
        scratch_shapes=[pltpu.VMEM((tm + 2 * HALO, QKV_W), F32)],
        compiler_params=pltpu.CompilerParams(dimension_semantics=("arbitrary", "arbitrary"),
                                             vmem_limit_bytes=VMEM_LIMIT),
        name="projection",
    )(*args)


def _post_kernel(o_ref, x_ref, mod_ref, wo_ref, ln1w_ref, ln1b_ref, w1_ref, b1_ref, w2_ref, b2_ref,
                 ln2w_ref, ln2b_ref, y_ref):
    m = mod_ref[0]
    g1 = m[:, 2 * D_MODEL:3 * D_MODEL]
    sh2 = m[:, 3 * D_MODEL:4 * D_MODEL]
    sc2 = m[:, 4 * D_MODEL:5 * D_MODEL]
    g2 = m[:, 5 * D_MODEL:6 * D_MODEL]
    y = jnp.dot(o_ref[0], wo_ref[...], preferred_element_type=F32)
    x1 = _layer_norm(ALPHA * x_ref[0] + g1 * y, 1e-6) * ln1w_ref[...] + ln1b_ref[...]
    h = _layer_norm(x1, 1e-6) * (1.0 + sc2) + sh2
    a = jnp.dot(h.astype(BF16), w1_ref[...], preferred_element_type=F32) + b1_ref[...]
    a = jnp.square(jnp.maximum(a, 0.0))
    f = jnp.dot(a.astype(BF16), w2_ref[...], preferred_element_type=F32) + b2_ref[...]
    y_ref[0] = _layer_norm(ALPHA * x1 + g2 * f, 1e-6) * ln2w_ref[...] + ln2b_ref[...]


def _post(o, x, mod3, row0, row_stride, wo, ln1w, ln1b, w1, b1, w2, b2, ln2w, ln2b, tm):
    bt, t, _ = x.shape
    return pl.pallas_call(
        _post_kernel,
        grid=(bt, t // tm),
        in_specs=[pl.BlockSpec((1, tm, D_MODEL), lambda b, i: (b, i, 0)),
                  pl.BlockSpec((1, tm, D_MODEL), lambda b, i: (b, i, 0)),
                  pl.BlockSpec((1, 1, 6 * D_MODEL), lambda b, i: (row0 + row_stride * b, 0, 0)),
                  _resident((D_MODEL, D_MODEL)),
                  _resident((1, D_MODEL)),
                  _resident((1, D_MODEL)),
                  _resident((D_MODEL, D_FF)),
                  _resident((1, D_FF)),
                  _resident((D_FF, D_MODEL)),
                  _resident((1, D_MODEL)),
                  _resident((1, D_MODEL)),
                  _resident((1, D_MODEL))],
        out_specs=pl.BlockSpec((1, tm, D_MODEL), lambda b, i: (b, i, 0)),
        out_shape=jax.ShapeDtypeStruct((bt, t, D_MODEL), F32),
        compiler_params=pltpu.CompilerParams(dimension_semantics=("arbitrary", "arbitrary"),
                                             vmem_limit_bytes=VMEM_LIMIT),
        name="post",
    )(o, x, mod3, wo, ln1w, ln1b, w1, b1, w2, b2, ln2w, ln2b)


_LOG_GAMMA = [math.log1p(-(2.0 ** (-5.0 - h))) for h in range(H_B)]

SOLVE_BASE = 8


def _unit_triangular_inverses(ms):
    i = lax.broadcasted_iota(jnp.int32, (CHUNK_A, CHUNK_A), 0)
    j = lax.broadcasted_iota(jnp.int32, (CHUNK_A, CHUNK_A), 1)
    same = lambda s: (i // s) == (j // s)
    eye = jnp.where(i == j, 1.0, 0.0)
    ps = [jnp.where(same(SOLVE_BASE), -m, 0.0) for m in ms]
    ts = [eye + p for p in ps]
    for _ in range(int(math.log2(SOLVE_BASE)) - 1):
        ps = [_dot(p, p) for p in ps]
        ts = [t + _dot(t, p) for t, p in zip(ts, ps)]
    s = SOLVE_BASE
    while s < CHUNK_A:
        sel = same(2 * s) & jnp.logical_not(same(s))
        ys = [_dot(jnp.where(sel, m, 0.0), t) for m, t in zip(ms, ts)]
        ts = [t - _dot(t, y) for t, y in zip(ts, ys)]
        s *= 2
    return ts


def _mixer_direction(rev, has_init, emit_state, tb, j, nblk,
                     qkv_ref, qkb_ref, vb_ref, gates_ref, gb_ref, naw_ref, gnw_ref, gnb_ref, sd0_ref, sr0_ref,
                     o_ref, sd_ref, sr_ref, sa_scr, sr_scr, of_scr):
    TB = TIME_BLOCK
    d = 1 if rev else 0

    @pl.when(j == 0)
    def _():
        if has_init:
            sa_scr[...] = sd0_ref[0, 0, 0]
            for h in range(H_B):
                lo = (h % 2) * DK_B
                sr_scr[h] = jnp.zeros((LANES, DV_B), F32)
                sr_scr[h, lo:lo + DK_B, :] = sr0_ref[0, 0, 0, h]
        else:
            sa_scr[...] = jnp.zeros(sa_scr.shape, F32)
            sr_scr[...] = jnp.zeros(sr_scr.shape, F32)

    gbk = gb_ref[0]

    ri = lax.broadcasted_iota(jnp.int32, (TB, TB), 0)
    ci = lax.broadcasted_iota(jnp.int32, (TB, TB), 1)
    same = (ri // CHUNK_A) == (ci // CHUNK_A)
    cum = jnp.where(same & ((ci >= ri) if rev else (ci <= ri)), 1.0, 0.0).astype(BF16)
    gc = None
    for part in _split3(gbk):
        t_ = jnp.dot(cum, part, preferred_element_type=F32)
        gc = t_ if gc is None else gc + t_
    gct = None
    for part in _split3(gbk.T):
        t_ = lax.dot_general(part, cum, (((1,), (1,)), ((), ())), preferred_element_type=F32)
        gct = t_ if gct is None else gct + t_

    i64 = lax.broadcasted_iota(jnp.int32, (CHUNK_A, CHUNK_A), 0)
    j64 = lax.broadcasted_iota(jnp.int32, (CHUNK_A, CHUNK_A), 1)
    tri = (i64 <= j64) if rev else (i64 >= j64)
    strict = (i64 < j64) if rev else (i64 > j64)

    row0 = pl.multiple_of(tb * TB, TB)
    nca = TB // CHUNK_A
    order_a = range(nca - 1, -1, -1) if rev else range(nca)

    def finish_a(r0, h, o_bwd):
        o = of_scr[pl.ds(row0 + r0, CHUNK_A), h * DV_A:(h + 1) * DV_A] + o_bwd
        o = o * lax.rsqrt(jnp.mean(o * o, -1, keepdims=True) + 1e-6) * naw_ref[...]
        gate = gates_ref[0, r0:r0 + CHUNK_A, h * DV_A:(h + 1) * DV_A]
        o_ref[0, r0:r0 + CHUNK_A, h * DV_A:(h + 1) * DV_A] = (o * gate).astype(BF16)

    items = [(n, h) for n in order_a for h in range(H_A)]
    st = []
    for n, h in items:
        r0 = n * CHUNK_A
        last = r0 if rev else r0 + CHUNK_A - 1
        col = d * H_A + h
        it = dict(r0=r0, h=h)
        it["q"] = qkv_ref[0, r0:r0 + CHUNK_A, h * DK_A:(h + 1) * DK_A]
        it["k"] = qkv_ref[0, r0:r0 + CHUNK_A, D_A + h * DK_A:D_A + (h + 1) * DK_A]
        v_h = qkv_ref[0, r0:r0 + CHUNK_A, 2 * D_A + h * DV_A:2 * D_A + (h + 1) * DV_A]
        gcc = gc[r0:r0 + CHUNK_A, col:col + 1]
        gcr = gct[col:col + 1, r0:r0 + CHUNK_A]
        it["gcc"] = gcc
        it["tot"] = gct[col:col + 1, last:last + 1]
        bcol = gbk[r0:r0 + CHUNK_A, N_AB + col:N_AB + col + 1]
        it["decay"] = jnp.where(tri, jnp.exp(jnp.where(tri, gcc - gcr, 0.0)), 0.0)
        it["kb"] = it["k"] * bcol
        it["eg"] = jnp.exp(gcc)
        it["rhs"] = jnp.concatenate([v_h * bcol, it["kb"] * it["eg"]], -1)
        st.append(it)
    for it in st:
        it["m"] = jnp.where(strict, _dot_nt(it["kb"], it["k"]) * it["decay"], 0.0)
    for it in st:
        it["qk"] = jnp.where(tri, _dot_nt(it["q"], it["k"]) * it["decay"], 0.0)
    inv = _unit_triangular_inverses([it["m"] for it in st])
    for it, t in zip(st, inv):
        sol = _dot(t, it["rhs"])
        it["u"] = sol[:, :DV_A]
        it["w"] = sol[:, DV_A:]
    for it in st:
        it["q_dec"] = it["q"] * it["eg"]
        it["k_dec"] = it["k"] * jnp.exp(it["tot"] - it["gcc"])

    lane = lax.broadcasted_iota(jnp.int32, (CHUNK_B, LANES), 1)
    pi = lax.broadcasted_iota(jnp.int32, (CHUNK_B, CHUNK_B), 0)
    pj = lax.broadcasted_iota(jnp.int32, (CHUNK_B, CHUNK_B), 1)
    pos = lax.broadcasted_iota(jnp.int32, (CHUNK_B, 1), 0)
    if rev:
        pi, pj, pos = CHUNK_B - 1 - pi, CHUNK_B - 1 - pj, CHUNK_B - 1 - pos
    pdiff = (pi - pj).astype(F32)
    posf = pos.astype(F32)
    ncb = TB // CHUNK_B
    order_b = range(ncb - 1, -1, -1) if rev else range(ncb)
    rt = []
    for h in range(H_B):
        lg = _LOG_GAMMA[H_B - 1 - h] if rev else _LOG_GAMMA[h]
        d_mask = jnp.where(pdiff >= 0, jnp.exp(lg * jnp.maximum(pdiff, 0.0)), 0.0)
        xi = jnp.exp(lg * (posf + 1.0))
        zeta = jnp.exp(lg * (CHUNK_B - 1.0 - posf))
        grp = (h // 2) * LANES
        in_head = (lane // DK_B) == (h % 2)
        for n in order_b:
            r0 = n * CHUNK_B
            it = dict(r0=r0, h=h, g_chunk=math.exp(lg * CHUNK_B), d_mask=d_mask)
            it["q"] = jnp.where(in_head, qkb_ref[0, r0:r0 + CHUNK_B, grp:grp + LANES], 0.0)
            k_h = jnp.where(in_head, qkb_ref[0, r0:r0 + CHUNK_B, H_B * DK_B + grp:H_B * DK_B + grp + LANES], 0.0)
            it["k"] = k_h
            it["kz"] = k_h * zeta
            it["qx"] = it["q"] * xi
            it["v"] = vb_ref[0, r0:r0 + CHUNK_B, h * DV_B:(h + 1) * DV_B]
            rt.append(it)
    for it in rt:
        it["scores"] = _dot_nt(it["q"], it["k"]) * it["d_mask"]
    for it in rt:
        it["ds"] = _dot_tn(it["kz"], it["v"])
    for it in rt:
        it["inner"] = _dot(it["scores"], it["v"])

    for c in range(nca):
        grp = st[c * H_A:(c + 1) * H_A]
        s_old = [sa_scr[it["h"]] for it in grp]
        ws = [_dot(it["w"], s) for it, s in zip(grp, s_old)]
        qs = [_dot(it["q_dec"], s) for it, s in zip(grp, s_old)]
        v_new = [it["u"] - x for it, x in zip(grp, ws)]
        upd = [_dot_tn(it["k_dec"], v) for it, v in zip(grp, v_new)]
        for it, s, x in zip(grp, s_old, upd):
            sa_scr[it["h"]] = s * jnp.exp(it["tot"]) + x
        outs = [x + _dot(it["qk"], v) for it, x, v in zip(grp, qs, v_new)]
        for it, o in zip(grp, outs):
            if rev:
                finish_a(it["r0"], it["h"], o)
            else:
                of_scr[pl.ds(row0 + it["r0"], CHUNK_A), it["h"] * DV_A:(it["h"] + 1) * DV_A] = o

    def finish_b(r0, h, o_bwd):
        c0 = D_A + h * DV_B
        o = of_scr[pl.ds(row0 + r0, CHUNK_B), c0:c0 + DV_B] + o_bwd
        o = _layer_norm(o, 1e-5) * gnw_ref[:, h * DV_B:(h + 1) * DV_B] + gnb_ref[:, h * DV_B:(h + 1) * DV_B]
        gate = gates_ref[0, r0:r0 + CHUNK_B, c0:c0 + DV_B]
        o_ref[0, r0:r0 + CHUNK_B, c0:c0 + DV_B] = (o * gate).astype(BF16)

    for it in rt:
        h = it["h"]
        s = sr_scr[h]
        o = it["inner"] + _dot(it["qx"], s)
        sr_scr[h] = s * it["g_chunk"] + it["ds"]
        if rev:
            finish_b(it["r0"], h, o)
        else:
            of_scr[pl.ds(row0 + it["r0"], CHUNK_B), D_A + h * DV_B:D_A + (h + 1) * DV_B] = o

    if emit_state:
        @pl.when(j == nblk - 1)
        def _():
            sd_ref[0, 0, 0] = sa_scr[...]
            for h in range(H_B):
                lo = (h % 2) * DK_B
                sr_ref[0, 0, 0, h] = sr_scr[h, lo:lo + DK_B, :]


def _mixer_kernel(has_init, emit_state, nblk, *refs):
    ph = pl.program_id(1)
    j = pl.program_id(2)
    qkv_ref, qkb_ref, vb_ref, gates_ref, gb_ref, naw_ref, gnw_ref, gnb_ref = refs[:8]
    pos_ = 8
    sd0_ref = sr0_ref = None
    if has_init:
        sd0_ref, sr0_ref = refs[pos_], refs[pos_ + 1]
        pos_ += 2
    o_ref = refs[pos_]
    pos_ += 1
    sd_ref = sr_ref = None
    if emit_state:
        sd_ref, sr_ref = refs[pos_], refs[pos_ + 1]
        pos_ += 2
    sa_scr, sr_scr, of_scr = refs[pos_:pos_ + 3]

    for rev in (False, True):
        @pl.when(ph == (1 if rev else 0))
        def _(rev=rev):
            tb = (nblk - 1 - j) if rev else j
            _mixer_direction(rev, has_init, emit_state, tb, j, nblk,
                             qkv_ref, qkb_ref, vb_ref, gates_ref, gb_ref, naw_ref, gnw_ref, gnb_ref,
                             sd0_ref, sr0_ref, o_ref, sd_ref, sr_ref, sa_scr, sr_scr, of_scr)


def _mixer(qkv, qkb, vb, gates, gb, naw, gnw, gnb, sd0, sr0, emit_state):
    bt, t, _ = qkv.shape
    TB = TIME_BLOCK
    nblk = t // TB
    has_init = sd0 is not None

    def tb_of(ph, j):
        return j + ph * (nblk - 1 - 2 * j)

    def tbo_of(ph, j):
        return (nblk - 1) - ph * j

    in_specs = [
        pl.BlockSpec((1, TB, QKV_W), lambda b, ph, j: (b, tb_of(ph, j), 0)),
        pl.BlockSpec((1, TB, QKB_W), lambda b, ph, j: (b, tb_of(ph, j), 0)),
        pl.BlockSpec((1, TB, D_B), lambda b, ph, j: (b, tb_of(ph, j), 0)),
        pl.BlockSpec((1, TB, D_A + D_B), lambda b, ph, j: (b, tbo_of(ph, j), 0)),
        pl.BlockSpec((1, TB, LANES), lambda b, ph, j: (b, tb_of(ph, j), 0)),
        _resident((1, DV_A)),
        _resident((1, D_B)),
        _resident((1, D_B)),
    ]
    args = [qkv, qkb, vb, gates, gb, naw, gnw, gnb]
    if has_init:
        in_specs.append(pl.BlockSpec((1, 1, 1, H_A, DK_A, DV_A), lambda b, ph, j: (b, 0, ph, 0, 0, 0)))
        in_specs.append(pl.BlockSpec((1, 1, 1, H_B, DK_B, DV_B), lambda b, ph, j: (b, 0, ph, 0, 0, 0)))
        args += [sd0, sr0]
    out_specs = [pl.BlockSpec((1, TB, D_A + D_B), lambda b, ph, j: (b, tbo_of(ph, j), 0))]
    out_shape = [jax.ShapeDtypeStruct((bt, t, D_A + D_B), BF16)]
    if emit_state:
        out_specs.append(pl.BlockSpec((1, 1, 1, H_A, DK_A, DV_A), lambda b, ph, j: (b, 0, ph, 0, 0, 0)))
        out_specs.append(pl.BlockSpec((1, 1, 1, H_B, DK_B, DV_B), lambda b, ph, j: (b, 0, ph, 0, 0, 0)))
        out_shape.append(jax.ShapeDtypeStruct((bt, DEPTH, 2, H_A, DK_A, DV_A), F32))
        out_shape.append(jax.ShapeDtypeStruct((bt, DEPTH, 2, H_B, DK_B, DV_B), F32))
    scratch = [pltpu.VMEM((H_A, DK_A, DV_A), F32),
               pltpu.VMEM((H_B, LANES, DV_B), F32),
               pltpu.VMEM((t, D_A + D_B), F32)]
    return pl.pallas_call(
        functools.partial(_mixer_kernel, has_init, emit_state, nblk),
        grid=(bt, 2, nblk),
        in_specs=in_specs,
        out_specs=out_specs,
        out_shape=out_shape,
        scratch_shapes=scratch,
        compiler_params=pltpu.CompilerParams(dimension_semantics=("arbitrary", "arbitrary", "arbitrary"),
                                             vmem_limit_bytes=VMEM_LIMIT),
        name="mixer",
    )(*args)


def _rope_tables(t):
    rows = t // GRID_W
    r = np.repeat(np.arange(rows, dtype=np.float32), GRID_W)
    col = np.tile(np.arange(GRID_W, dtype=np.float32), rows)
    nf = DK_B // 4
    inv = (np.float32(ROPE_BASE) ** (-np.arange(nf, dtype=np.float32) / np.float32(nf))).astype(np.float32)
    ang = jnp.asarray(np.concatenate([r[:, None] * inv, col[:, None] * inv], -1).astype(np.float32))
    cos, sin = jnp.cos(ang), jnp.sin(ang)
    zero = jnp.zeros_like(sin)
    cos_t = jnp.tile(cos, (1, 4))
    sin_up = jnp.tile(jnp.concatenate([-sin, zero], -1), (1, 2))
    sin_dn = jnp.tile(jnp.concatenate([zero, sin], -1), (1, 2))
    return jnp.stack([cos_t, sin_up, sin_dn], 0)


def kernel(x_prompt, x_sample, c, state_delta, state_ret, c_ctx, w_mod, b_mod, w_in, conv_w, a_log, dt_bias,
           norm_a_w, gn_w, gn_b, w_o, ln1_w, ln1_b, w_ff1, b_ff1, w_ff2, b_ff2, ln2_w, ln2_b):
    assert w_mod.shape[0] == DEPTH == 1
    n_dec = c.shape[0]
    rows = -(-(1 + n_dec) // SUBLANES) * SUBLANES
    cond = jnp.zeros((rows, D_MODEL), F32).at[0].set(c_ctx).at[1:1 + n_dec].set(c)
    mod = _modulation(cond, w_mod[0], b_mod[0])
    mod3 = mod.reshape(rows, 1, 6 * D_MODEL)

    w = w_in[0]
    c_ab = QKV_W + D_A
    w_in_p = jnp.concatenate([w[:, :c_ab], w[:, c_ab + 2 * N_AB:], w[:, c_ab:c_ab + 2 * N_AB],
                              jnp.zeros((D_MODEL, LANES - 2 * N_AB), F32)], -1).astype(BF16)
    convw = jnp.zeros((SUBLANES, QKV_W), F32).at[:CONV_K].set(conv_w[0])
    par = (jnp.zeros((SUBLANES, LANES), F32).at[0, :N_AB].set(a_log[0].reshape(-1))
           .at[1, :N_AB].set(dt_bias[0].reshape(-1)))
    naw = norm_a_w[0].reshape(1, DV_A)
    gnw = gn_w[0].reshape(1, D_B)
    gnb = gn_b[0].reshape(1, D_B)
    wo = w_o[0].astype(BF16)
    w1 = w_ff1[0].astype(BF16)
    w2 = w_ff2[0].astype(BF16)
    row = lambda v: v[0].reshape(1, -1)

    def trunk(x, row0, row_stride, rope, sd0, sr0, emit_state, tm):
        qkv, qkb, vb, gates, gb = _projection(x, mod3, row0, row_stride, w_in_p, convw, par, rope, tm)
        res = _mixer(qkv, qkb, vb, gates, gb, naw, gnw, gnb, sd0, sr0, emit_state)
        y = _post(res[0], x, mod3, row0, row_stride, wo, row(ln1_w), row(ln1_b), w1, row(b_ff1), w2, row(b_ff2),
                  row(ln2_w), row(ln2_b), tm)
        return y, res[1:]

    y_prompt, (new_sd, new_sr) = trunk(x_prompt, 0, 0, None, None, None, True, 256)
    y_sample, _ = trunk(x_sample, 1, 1, _rope_tables(x_sample.shape[1]), state_delta, state_ret, False, 512)
    return y_prompt, y_sample, new_sd, new_sr
```

```python
import functools
import math

import jax
import jax.numpy as jnp
import numpy as np
from jax import lax
from jax.experimental import pallas as pl
from jax.experimental.pallas import tpu as pltpu

F32 = jnp.float32
BF16 = jnp.bfloat16

D_MODEL = 1024
H_A, DK_A, DV_A = 4, 128, 128
D_A = H_A * DV_A
CONV_K = 5
CHUNK_A = 64
H_B, DK_B, DV_B = 4, 64, 128
D_B = H_B * DV_B
CHUNK_B = 128
GRID_W = 64
ROPE_BASE = 10000.0
D_FF = 4 * D_MODEL
DEPTH = 1
ALPHA = (2.0 * DEPTH) ** 0.25

LANES = 128
SUBLANES = 8
VMEM_LIMIT = 56 * 1024 * 1024

QKV_W = 3 * D_A
COL_GATE_A = QKV_W
COL_QK_B = COL_GATE_A + D_A
QKB_W = 2 * H_B * DK_B
COL_V_B = COL_QK_B + QKB_W
COL_GATE_B = COL_V_B + D_B
COL_AB = COL_GATE_B + D_B
P_W = COL_AB + LANES
N_AB = 2 * H_A

TIME_BLOCK = 256
HALO = SUBLANES


def _dot(a, b):
    return jnp.dot(a.astype(BF16), b.astype(BF16), preferred_element_type=F32)


def _dot_nt(a, b):
    return lax.dot_general(a.astype(BF16), b.astype(BF16), (((1,), (1,)), ((), ())),
                           preferred_element_type=F32)


def _dot_tn(a, b):
    return lax.dot_general(a.astype(BF16), b.astype(BF16), (((0,), (0,)), ((), ())),
                           preferred_element_type=F32)


def _split3(x):
    hi = x.astype(BF16)
    r = x - hi.astype(F32)
    mid = r.astype(BF16)
    lo = (r - mid.astype(F32)).astype(BF16)
    return hi, mid, lo


def _silu(x):
    return x * (1.0 / (1.0 + jnp.exp(-x)))


def _layer_norm(x, eps):
    mu = jnp.mean(x, -1, keepdims=True)
    xc = x - mu
    var = jnp.mean(xc * xc, -1, keepdims=True)
    return xc * lax.rsqrt(var + eps)


def _resident(shape):
    return pl.BlockSpec(shape, lambda *_: (0,) * len(shape), pipeline_mode=pl.Buffered(1))


def _mod_kernel(c_ref, w_ref, b_ref, o_ref):
    o_ref[...] = _dot(_silu(c_ref[...]), w_ref[...]) + b_ref[...]


def _modulation(cond, w_mod, b_mod):
    rows = cond.shape[0]
    n = w_mod.shape[1]
    bn = 512
    return pl.pallas_call(
        _mod_kernel,
        grid=(n // bn,),
        in_specs=[pl.BlockSpec((rows, D_MODEL), lambda j: (0, 0)),
                  pl.BlockSpec((D_MODEL, bn), lambda j: (0, j)),
                  pl.BlockSpec((1, bn), lambda j: (0, j))],
        out_specs=pl.BlockSpec((rows, bn), lambda j: (0, j)),
        out_shape=jax.ShapeDtypeStruct((rows, n), F32),
        compiler_params=pltpu.CompilerParams(dimension_semantics=("arbitrary",)),
        name="modulation",
    )(cond, w_mod, b_mod.reshape(1, n))


def _proj_kernel(use_rope, tm, nt, *refs):
    if use_rope:
        (x_ref, xp_ref, xn_ref, mod_ref, w_ref, convw_ref, par_ref, rope_ref,
         qkv_ref, qkb_ref, vb_ref, gates_ref, gb_ref, xe_scr) = refs
    else:
        (x_ref, xp_ref, xn_ref, mod_ref, w_ref, convw_ref, par_ref,
         qkv_ref, qkb_ref, vb_ref, gates_ref, gb_ref, xe_scr) = refs
    i = pl.program_id(1)
    m = mod_ref[0]
    sh1 = m[:, 0:D_MODEL]
    sc1 = m[:, D_MODEL:2 * D_MODEL]

    def modulated(x):
        return _layer_norm(x, 1e-6) * (1.0 + sc1) + sh1

    h = jnp.concatenate([modulated(xp_ref[0]), modulated(x_ref[0]), modulated(xn_ref[0])], 0)
    p = jnp.dot(h.astype(BF16), w_ref[...], preferred_element_type=F32)

    xe_scr[0:HALO, :] = jnp.where(i > 0, p[0:HALO, 0:QKV_W], 0.0)
    xe_scr[HALO:HALO + tm, :] = p[HALO:HALO + tm, 0:QKV_W]
    xe_scr[HALO + tm:, :] = jnp.where(i < nt - 1, p[HALO + tm:, 0:QKV_W], 0.0)
    pad = (CONV_K - 1) // 2
    acc = None
    for kk in range(CONV_K):
        off = HALO - pad + kk
        term = xe_scr[off:off + tm, :] * convw_ref[kk:kk + 1, :]
        acc = term if acc is None else acc + term
    qkv = _silu(acc)

    def l2n(x):
        return x * lax.rsqrt(jnp.sum(x * x, -1, keepdims=True) + 1e-6)

    for hd in range(H_A):
        c = hd * DK_A
        qkv_ref[0, :, c:c + DK_A] = l2n(qkv[:, c:c + DK_A]) * (DK_A ** -0.5)
        qkv_ref[0, :, D_A + c:D_A + c + DK_A] = l2n(qkv[:, D_A + c:D_A + c + DK_A])
    qkv_ref[0, :, 2 * D_A:] = qkv[:, 2 * D_A:]

    rest = p[HALO:HALO + tm]
    gates_ref[0, :, 0:D_A] = _silu(rest[:, COL_GATE_A:COL_GATE_A + D_A])
    gates_ref[0, :, D_A:] = _silu(rest[:, COL_GATE_B:COL_GATE_B + D_B])
    vb_ref[0] = rest[:, COL_V_B:COL_V_B + D_B].astype(BF16)

    for c4 in range(QKB_W // LANES):
        x = rest[:, COL_QK_B + c4 * LANES:COL_QK_B + (c4 + 1) * LANES]
        if use_rope:
            x = (x * rope_ref[0] + pltpu.roll(x, LANES - DK_B // 2, 1) * rope_ref[1]
                 + pltpu.roll(x, DK_B // 2, 1) * rope_ref[2])
        if c4 < H_B * DK_B // LANES:
            x = x * (DK_B ** -0.5)
        qkb_ref[0, :, c4 * LANES:(c4 + 1) * LANES] = x

    ab = rest[:, COL_AB:COL_AB + LANES]
    z = ab + par_ref[1:2, :]
    softplus = jnp.maximum(z, 0.0) + jnp.log1p(jnp.exp(-jnp.abs(z)))
    g_all = -jnp.exp(par_ref[0:1, :]) * softplus
    beta_all = 1.0 / (1.0 + jnp.exp(-ab))
    lane = lax.broadcasted_iota(jnp.int32, (tm, LANES), 1)
    gb_ref[0] = jnp.where(lane < N_AB, g_all, jnp.where(lane < 2 * N_AB, beta_all, 0.0))


def _projection(x, mod3, row0, row_stride, w_in_p, convw, par, rope, tm):
    bt, t, _ = x.shape
    nt = t // tm
    hpt = tm // HALO
    nh = t // HALO
    use_rope = rope is not None
    in_specs = [pl.BlockSpec((1, tm, D_MODEL), lambda b, i: (b, i, 0)),
                pl.BlockSpec((1, HALO, D_MODEL), lambda b, i: (b, jnp.maximum(i * hpt - 1, 0), 0)),
                pl.BlockSpec((1, HALO, D_MODEL), lambda b, i: (b, jnp.minimum((i + 1) * hpt, nh - 1), 0)),
                pl.BlockSpec((1, 1, 6 * D_MODEL), lambda b, i: (row0 + row_stride * b, 0, 0)),
                _resident((D_MODEL, P_W)),
                _resident((SUBLANES, QKV_W)),
                _resident((SUBLANES, LANES))]
    args = [x, x, x, mod3, w_in_p, convw, par]
    if use_rope:
        in_specs.append(pl.BlockSpec((3, tm, LANES), lambda b, i: (0, i, 0)))
        args.append(rope)
    widths = (QKV_W, QKB_W, D_B, D_A + D_B, LANES)
    dtypes = (F32, F32, BF16, F32, F32)
    return pl.pallas_call(
        functools.partial(_proj_kernel, use_rope, tm, nt),
        grid=(bt, nt),
        in_specs=in_specs,
        out_specs=[pl.BlockSpec((1, tm, w), lambda b, i: (b, i, 0)) for w in widths],
        out_shape=[jax.ShapeDtypeStruct((bt, t, w), dt) for w, dt in zip(widths, dtypes)],
        scratch_shapes=[pltpu.VMEM((tm + 2 * HALO, QKV_W), F32)],
        compiler_params=pltpu.CompilerParams(dimension_semantics=("arbitrary", "arbitrary"),
                                             vmem_limit_bytes=VMEM_LIMIT),
        name="projection",
    )(*args)


def _post_kernel(o_ref, x_ref, mod_ref, wo_ref, ln1w_ref, ln1b_ref, w1_ref, b1_ref, w2_ref, b2_ref,
                 ln2w_ref, ln2b_ref, y_ref):
    m = mod_ref[0]
    g1 = m[:, 2 * D_MODEL:3 * D_MODEL]
    sh2 = m[:, 3 * D_MODEL:4 * D_MODEL]
    sc2 = m[:, 4 * D_MODEL:5 * D_MODEL]
    g2 = m[:, 5 * D_MODEL:6 * D_MODEL]
    y = jnp.dot(o_ref[0], wo_ref[...], preferred_element_type=F32)
    x1 = _layer_norm(ALPHA * x_ref[0] + g1 * y, 1e-6) * ln1w_ref[...] + ln1b_ref[...]
    h = _layer_norm(x1, 1e-6) * (1.0 + sc2) + sh2
    a = jnp.dot(h.astype(BF16), w1_ref[...], preferred_element_type=F32) + b1_ref[...]
    a = jnp.square(jnp.maximum(a, 0.0))
    f = jnp.dot(a.astype(BF16), w2_ref[...], preferred_element_type=F32) + b2_ref[...]
    y_ref[0] = _layer_norm(ALPHA * x1 + g2 * f, 1e-6) * ln2w_ref[...] + ln2b_ref[...]


def _post(o, x, mod3, row0, row_stride, wo, ln1w, ln1b, w1, b1, w2, b2, ln2w, ln2b, tm):
    bt, t, _ = x.shape
    return pl.pallas_call(
        _post_kernel,
        grid=(bt, t // tm),
        in_specs=[pl.BlockSpec((1, tm, D_MODEL), lambda b, i: (b, i, 0)),
                  pl.BlockSpec((1, tm, D_MODEL), lambda b, i: (b, i, 0)),
                  pl.BlockSpec((1, 1, 6 * D_MODEL), lambda b, i: (row0 + row_stride * b, 0, 0)),
                  _resident((D_MODEL, D_MODEL)),
                  _resident((1, D_MODEL)),
                  _resident((1, D_MODEL)),
                  _resident((D_MODEL, D_FF)),
                  _resident((1, D_FF)),
                  _resident((D_FF, D_MODEL)),
                  _resident((1, D_MODEL)),
                  _resident((1, D_MODEL)),
                  _resident((1, D_MODEL))],
        out_specs=pl.BlockSpec((1, tm, D_MODEL), lambda b, i: (b, i, 0)),
        out_shape=jax.ShapeDtypeStruct((bt, t, D_MODEL), F32),
        compiler_params=pltpu.CompilerParams(dimension_semantics=("arbitrary", "arbitrary"),
                                             vmem_limit_bytes=VMEM_LIMIT),
        name="post",
    )(o, x, mod3, wo, ln1w, ln1b, w1, b1, w2, b2, ln2w, ln2b)


_LOG_GAMMA = [math.log1p(-(2.0 ** (-5.0 - h))) for h in range(H_B)]

SOLVE_BASE = 8
SEQ_PER_STEP = 2
CAT_W = H_A * CHUNK_A


def _head_blockdiag(x, mask):
    return jnp.where(mask, jnp.concatenate([x] * H_A, 0), 0.0).astype(BF16)


def _cat_inverses(ms, bd_mask):
    i = lax.broadcasted_iota(jnp.int32, (CHUNK_A, CAT_W), 0)
    j = lax.broadcasted_iota(jnp.int32, (CHUNK_A, CAT_W), 1) % CHUNK_A
    same = lambda s: (i // s) == (j // s)
    eye = jnp.where(i == j, 1.0, 0.0)
    mul = lambda a, bd: jnp.dot(a.astype(BF16), bd, preferred_element_type=F32)
    ps = [jnp.where(same(SOLVE_BASE), -m, 0.0) for m in ms]
    ts = [eye + p for p in ps]
    bds = [_head_blockdiag(p, bd_mask) for p in ps]
    for _ in range(int(math.log2(SOLVE_BASE)) - 1):
        ps = [mul(p, bd) for p, bd in zip(ps, bds)]
        bds = [_head_blockdiag(p, bd_mask) for p in ps]
        ts = [t + mul(t, bd) for t, bd in zip(ts, bds)]
    s = SOLVE_BASE
    while s < CHUNK_A:
        sel = same(2 * s) & jnp.logical_not(same(s))
        bds = [_head_blockdiag(t, bd_mask) for t in ts]
        ys = [mul(jnp.where(sel, m, 0.0), bd) for m, bd in zip(ms, bds)]
        bds = [_head_blockdiag(y, bd_mask) for y in ys]
        ts = [t - mul(t, bd) for t, bd in zip(ts, bds)]
        s *= 2
    return ts


def _mixer_kernel(rev, has_init, emit_state, nblk, *refs):
    NS = SEQ_PER_STEP
    TB = TIME_BLOCK
    d = 1 if rev else 0
    j = pl.program_id(1)
    qkv_ref, qkb_ref, vb_ref, gb_ref = refs[:4]
    pos_ = 4
    if rev:
        of_ref, gates_ref, naw_ref, gnw_ref, gnb_ref = refs[pos_:pos_ + 5]
        pos_ += 5
    if has_init:
        sd0_ref, sr0_ref = refs[pos_:pos_ + 2]
        pos_ += 2
    o_ref = refs[pos_]
    pos_ += 1
    if emit_state:
        sd_ref, sr_ref = refs[pos_:pos_ + 2]
        pos_ += 2
    sa_scr, sr_scr = refs[pos_:pos_ + 2]

    @pl.when(j == 0)
    def _():
        if has_init:
            for bb in range(NS):
                for h in range(H_A):
                    sa_scr[bb, :, h * DV_A:(h + 1) * DV_A] = sd0_ref[bb, 0, 0, h]
                sr_scr[bb] = jnp.zeros(sr_scr.shape[1:], F32)
                for h in range(H_B):
                    lo = (h % 2) * DK_B
                    sr_scr[bb, h // 2, lo:lo + DK_B, (h % 2) * DV_B:(h % 2 + 1) * DV_B] = sr0_ref[bb, 0, 0, h]
        else:
            sa_scr[...] = jnp.zeros(sa_scr.shape, F32)
            sr_scr[...] = jnp.zeros(sr_scr.shape, F32)

    ri = lax.broadcasted_iota(jnp.int32, (TB, TB), 0)
    ci = lax.broadcasted_iota(jnp.int32, (TB, TB), 1)
    cum = jnp.where(((ri // CHUNK_A) == (ci // CHUNK_A)) & ((ci >= ri) if rev else (ci <= ri)),
                    1.0, 0.0).astype(BF16)
    er = lax.broadcasted_iota(jnp.int32, (LANES, CAT_W), 0)
    ec = lax.broadcasted_iota(jnp.int32, (LANES, CAT_W), 1)
    esel_c = jnp.where(er == d * H_A + ec // CHUNK_A, 1.0, 0.0).astype(BF16)
    er = lax.broadcasted_iota(jnp.int32, (LANES, 2 * D_A), 0)
    ec = lax.broadcasted_iota(jnp.int32, (LANES, 2 * D_A), 1)
    esel_w = jnp.where(er == jnp.where(ec < D_A, d * H_A + ec // DK_A, N_AB + d * H_A + (ec - D_A) // DK_A),
                       1.0, 0.0).astype(BF16)
    lane_tb = lax.broadcasted_iota(jnp.int32, (TB, LANES), 1)
    i64 = lax.broadcasted_iota(jnp.int32, (CHUNK_A, CAT_W), 0)
    j64 = lax.broadcasted_iota(jnp.int32, (CHUNK_A, CAT_W), 1) % CHUNK_A
    tri = (i64 <= j64) if rev else (i64 >= j64)
    strict = (i64 < j64) if rev else (i64 > j64)
    bd_mask = (lax.broadcasted_iota(jnp.int32, (CAT_W, CAT_W), 0) // CHUNK_A
               == lax.broadcasted_iota(jnp.int32, (CAT_W, CAT_W), 1) // CHUNK_A)
    bdk_mask = (lax.broadcasted_iota(jnp.int32, (CAT_W, D_A), 0) // CHUNK_A
                == lax.broadcasted_iota(jnp.int32, (CAT_W, D_A), 1) // DK_A)
    lane_head = lax.broadcasted_iota(jnp.int32, (CHUNK_A, CAT_W), 1) // CHUNK_A
    nca = TB // CHUNK_A
    order_a = list(range(nca - 1, -1, -1)) if rev else list(range(nca))

    st = []
    for bb in range(NS):
        gbk = gb_ref[bb]
        gc = None
        for part in _split3(gbk):
            t_ = jnp.dot(cum, part, preferred_element_type=F32)
            gc = t_ if gc is None else gc + t_
        gct = None
        for part in _split3(gbk.T):
            t_ = lax.dot_general(part, cum, (((1,), (1,)), ((), ())), preferred_element_type=F32)
            gct = t_ if gct is None else gct + t_
        parts = _split3(jnp.where(lane_tb < N_AB, gc, gbk))
        gcs = None
        for part in parts:
            t_ = jnp.dot(part, esel_c, preferred_element_type=F32)
            gcs = t_ if gcs is None else gcs + t_
        wide = (jnp.dot(parts[0], esel_w, preferred_element_type=F32)
                + jnp.dot(parts[1], esel_w, preferred_element_type=F32))
        for n in order_a:
            r0 = n * CHUNK_A
            last = r0 if rev else r0 + CHUNK_A - 1
            it = dict(bb=bb, r0=r0)
            q = qkv_ref[bb, r0:r0 + CHUNK_A, 0:D_A]
            k = qkv_ref[bb, r0:r0 + CHUNK_A, D_A:2 * D_A]
            v = qkv_ref[bb, r0:r0 + CHUNK_A, 2 * D_A:3 * D_A]
            gcc = gcs[r0:r0 + CHUNK_A]
            gcr = jnp.concatenate([gct[d * H_A + h:d * H_A + h + 1, r0:r0 + CHUNK_A] for h in range(H_A)], 1)
            gc5 = wide[r0:r0 + CHUNK_A, 0:D_A]
            tot5 = wide[last:last + 1, 0:D_A]
            b5 = wide[r0:r0 + CHUNK_A, D_A:]
            it["decay"] = jnp.where(tri, jnp.exp(jnp.where(tri, gcc - gcr, 0.0)), 0.0)
            kb = k * b5
            eg5 = jnp.exp(gc5)
            it["lhs"] = jnp.concatenate([kb, q], 0)
            it["kbd"] = _head_blockdiag(k, bdk_mask)
            vb5 = v * b5
            kbe5 = kb * eg5
            it["rhs"] = jnp.concatenate(
                [jnp.concatenate([vb5[:, h * DV_A:(h + 1) * DV_A], kbe5[:, h * DK_A:(h + 1) * DK_A]], 1)
                 for h in range(H_A)], 0).astype(BF16)
            it["q_dec"] = q * eg5
            it["k_dec"] = k * jnp.exp(tot5 - gc5)
            it["gl5"] = jnp.exp(tot5)
            st.append(it)
    for it in st:
        mq = _dot_nt(it["lhs"], it["kbd"])
        it["m"] = jnp.where(strict, mq[:CHUNK_A] * it["decay"], 0.0)
        it["qk"] = jnp.where(tri, mq[CHUNK_A:] * it["decay"], 0.0)
    inv = _cat_inverses([it["m"] for it in st], bd_mask)
    for it, t in zip(st, inv):
        lhs = jnp.concatenate([jnp.where(lane_head == h, t, 0.0) for h in range(H_A)], 0)
        sol = jnp.dot(lhs.astype(BF16), it["rhs"], preferred_element_type=F32)
        it["u"] = jnp.concatenate([sol[h * CHUNK_A:(h + 1) * CHUNK_A, :DV_A] for h in range(H_A)], 1)
        it["w"] = jnp.concatenate([sol[h * CHUNK_A:(h + 1) * CHUNK_A, DV_A:] for h in range(H_A)], 1)

    lane = lax.broadcasted_iota(jnp.int32, (CHUNK_B, LANES), 1)
    pi = lax.broadcasted_iota(jnp.int32, (CHUNK_B, CHUNK_B), 0)
    pj = lax.broadcasted_iota(jnp.int32, (CHUNK_B, CHUNK_B), 1)
    pos = lax.broadcasted_iota(jnp.int32, (CHUNK_B, 1), 0)
    if rev:
        pi, pj, pos = CHUNK_B - 1 - pi, CHUNK_B - 1 - pj, CHUNK_B - 1 - pos
    pdiff = (pi - pj).astype(F32)
    posf = pos.astype(F32)
    first_half = (lane // DK_B) == 0
    own_block = (lax.broadcasted_iota(jnp.int32, (LANES, 2 * DV_B), 0) // DK_B
                 == lax.broadcasted_iota(jnp.int32, (LANES, 2 * DV_B), 1) // DV_B)
    ncb = TB // CHUNK_B
    order_b = range(ncb - 1, -1, -1) if rev else range(ncb)
    rt = []
    for pp in range(H_B // 2):
        lgs = [_LOG_GAMMA[H_B - 1 - h] if rev else _LOG_GAMMA[h] for h in (2 * pp, 2 * pp + 1)]
        d_mask = jnp.concatenate([jnp.where(pdiff >= 0, jnp.exp(lg * jnp.maximum(pdiff, 0.0)), 0.0) for lg in lgs], 1)
        xi = jnp.where(first_half, jnp.exp(lgs[0] * (posf + 1.0)), jnp.exp(lgs[1] * (posf + 1.0)))
        zeta = jnp.where(first_half, jnp.exp(lgs[0] * (CHUNK_B - 1.0 - posf)), jnp.exp(lgs[1] * (CHUNK_B - 1.0 - posf)))
        col_head = lax.broadcasted_iota(jnp.int32, (1, 2 * DV_B), 1) // DV_B
        g_chunk = jnp.where(col_head == 0, math.exp(lgs[0] * CHUNK_B), math.exp(lgs[1] * CHUNK_B))
        for bb in range(NS):
            for n in order_b:
                rt.append(dict(bb=bb, r0=n * CHUNK_B, pp=pp, g_chunk=g_chunk, d_mask=d_mask, xi=xi, zeta=zeta))

    def retention_independent(it):
        bb, r0, pp = it["bb"], it["r0"], it["pp"]
        q = qkb_ref[bb, r0:r0 + CHUNK_B, pp * LANES:(pp + 1) * LANES]
        k = qkb_ref[bb, r0:r0 + CHUNK_B, H_B * DK_B + pp * LANES:H_B * DK_B + (pp + 1) * LANES]
        v0 = vb_ref[bb, r0:r0 + CHUNK_B, 2 * pp * DV_B:(2 * pp + 1) * DV_B]
        v1 = vb_ref[bb, r0:r0 + CHUNK_B, (2 * pp + 1) * DV_B:(2 * pp + 2) * DV_B]
        zb = jnp.zeros((CHUNK_B, DV_B), BF16)
        ksplit = jnp.concatenate([jnp.where(first_half, k, 0.0), jnp.where(first_half, 0.0, k)], 0)
        scores = _dot_nt(q, ksplit) * it["d_mask"]
        v_bd = jnp.concatenate([jnp.concatenate([v0, zb], 1), jnp.concatenate([zb, v1], 1)], 0)
        it["inner"] = jnp.dot(scores.astype(BF16), v_bd, preferred_element_type=F32)
        it["qx"] = q * it["xi"]
        ds = lax.dot_general((k * it["zeta"]).astype(BF16), jnp.concatenate([v0, v1], 1),
                             (((0,), (0,)), ((), ())), preferred_element_type=F32)
        it["ds"] = jnp.where(own_block, ds, 0.0)

    def finish(bb, r0, rows, c0, width, o):
        if not rev:
            o_ref[bb, r0:r0 + rows, c0:c0 + width] = o
            return
        o = o + of_ref[bb, r0:r0 + rows, c0:c0 + width]
        outs = []
        for hh in range(width // LANES):
            x = o[:, hh * LANES:(hh + 1) * LANES]
            if c0 < D_A:
                x = x * lax.rsqrt(jnp.mean(x * x, -1, keepdims=True) + 1e-6) * naw_ref[...]
            else:
                cb = c0 - D_A + hh * LANES
                x = _layer_norm(x, 1e-5) * gnw_ref[:, cb:cb + LANES] + gnb_ref[:, cb:cb + LANES]
            outs.append(x)
        o = outs[0] if len(outs) == 1 else jnp.concatenate(outs, 1)
        o_ref[bb, r0:r0 + rows, c0:c0 + width] = (o * gates_ref[bb, r0:r0 + rows, c0:c0 + width]).astype(BF16)

    per_step = -(-len(rt) // nca)
    for c in range(nca):
        grp_items = [st[bb * nca + c] for bb in range(NS)]
        s_old = [sa_scr[it["bb"]] for it in grp_items]
        prods = []
        for it, s in zip(grp_items, s_old):
            pr = []
            for pp in range(H_A // 2):
                lo = pp * 2 * DV_A
                z = jnp.zeros((DK_A, DV_A), F32)
                bds = jnp.concatenate([jnp.concatenate([s[:, lo:lo + DV_A], z], 1),
                                       jnp.concatenate([z, s[:, lo + DV_A:lo + 2 * DV_A]], 1)], 0)
                lhs = jnp.concatenate([it["w"][:, lo:lo + 2 * DK_A], it["q_dec"][:, lo:lo + 2 * DK_A]], 0)
                pr.append(_dot(lhs, bds))
            prods.append(jnp.concatenate(pr, 1))
        for it in rt[c * per_step:(c + 1) * per_step]:
            retention_independent(it)
        v_new = [it["u"] - pr[:CHUNK_A] for it, pr in zip(grp_items, prods)]
        vbd = [_head_blockdiag(v, bdk_mask) for v in v_new]
        upd = []
        for it, vb_ in zip(grp_items, vbd):
            kd_stack = jnp.concatenate([it["k_dec"][:, h * DK_A:(h + 1) * DK_A] for h in range(H_A)], 0)
            upd.append(lax.dot_general(kd_stack.astype(BF16), vb_, (((0,), (0,)), ((), ())),
                                       preferred_element_type=F32))
        for it, s, x in zip(grp_items, s_old, upd):
            sa_scr[it["bb"]] = s * it["gl5"] + x
        for it, pr, vb_ in zip(grp_items, prods, vbd):
            o = pr[CHUNK_A:] + jnp.dot(it["qk"].astype(BF16), vb_, preferred_element_type=F32)
            finish(it["bb"], it["r0"], CHUNK_A, 0, D_A, o)

    for it in rt:
        bb, pp = it["bb"], it["pp"]
        s = sr_scr[bb, pp]
        o = it["inner"] + _dot(it["qx"], s)
        sr_scr[bb, pp] = s * it["g_chunk"] + it["ds"]
        finish(bb, it["r0"], CHUNK_B, D_A + 2 * pp * DV_B, 2 * DV_B, o)

    if emit_state:
        @pl.when(j == nblk - 1)
        def _():
            for bb in range(NS):
                for h in range(H_A):
                    sd_ref[bb, 0, 0, h] = sa_scr[bb, :, h * DV_A:(h + 1) * DV_A]
                for h in range(H_B):
                    lo = (h % 2) * DK_B
                    sr_ref[bb, 0, 0, h] = sr_scr[bb, h // 2, lo:lo + DK_B, (h % 2) * DV_B:(h % 2 + 1) * DV_B]


def _mixer(rev, qkv, qkb, vb, gb, o_fwd, gates, naw, gnw, gnb, sd0, sr0, emit_state):
    bt, t, _ = qkv.shape
    NS = SEQ_PER_STEP
    TB = TIME_BLOCK
    nblk = t // TB
    has_init = sd0 is not None
    d = 1 if rev else 0
    tb_of = (lambda j: nblk - 1 - j) if rev else (lambda j: j)
    tok = lambda w: pl.BlockSpec((NS, TB, w), lambda b, j: (b, tb_of(j), 0))
    in_specs = [tok(QKV_W), tok(QKB_W), tok(D_B), tok(LANES)]
    args = [qkv, qkb, vb, gb]
    if rev:
        in_specs += [tok(D_A + D_B), tok(D_A + D_B), _resident((1, DV_A)), _resident((1, D_B)), _resident((1, D_B))]
        args += [o_fwd, gates, naw, gnw, gnb]
    if has_init:
        in_specs.append(pl.BlockSpec((NS, 1, 1, H_A, DK_A, DV_A), lambda b, j: (b, 0, d, 0, 0, 0)))
        in_specs.append(pl.BlockSpec((NS, 1, 1, H_B, DK_B, DV_B), lambda b, j: (b, 0, d, 0, 0, 0)))
        args += [sd0, sr0]
    out_specs = [tok(D_A + D_B)]
    out_shape = [jax.ShapeDtypeStruct((bt, t, D_A + D_B), BF16 if rev else F32)]
    if emit_state:
        out_specs.append(pl.BlockSpec((NS, 1, 1, H_A, DK_A, DV_A), lambda b, j: (b, 0, 0, 0, 0, 0)))
        out_specs.append(pl.BlockSpec((NS, 1, 1, H_B, DK_B, DV_B), lambda b, j: (b, 0, 0, 0, 0, 0)))
        out_shape.append(jax.ShapeDtypeStruct((bt, DEPTH, 1, H_A, DK_A, DV_A), F32))
        out_shape.append(jax.ShapeDtypeStruct((bt, DEPTH, 1, H_B, DK_B, DV_B), F32))
    scratch = [pltpu.VMEM((NS, DK_A, H_A * DV_A), F32),
               pltpu.VMEM((NS, H_B // 2, LANES, 2 * DV_B), F32)]
    return pl.pallas_call(
        functools.partial(_mixer_kernel, rev, has_init, emit_state, nblk),
        grid=(bt // NS, nblk),
        in_specs=in_specs,
        out_specs=out_specs,
        out_shape=out_shape,
        scratch_shapes=scratch,
        compiler_params=pltpu.CompilerParams(dimension_semantics=("arbitrary", "arbitrary"),
                                             vmem_limit_bytes=VMEM_LIMIT),
        name="mixer_bwd" if rev else "mixer_fwd",
    )(*args)


def _rope_tables(t):
    rows = t // GRID_W
    r = np.repeat(np.arange(rows, dtype=np.float32), GRID_W)
    col = np.tile(np.arange(GRID_W, dtype=np.float32), rows)
    nf = DK_B // 4
    inv = (np.float32(ROPE_BASE) ** (-np.arange(nf, dtype=np.float32) / np.float32(nf))).astype(np.float32)
    ang = jnp.asarray(np.concatenate([r[:, None] * inv, col[:, None] * inv], -1).astype(np.float32))
    cos, sin = jnp.cos(ang), jnp.sin(ang)
    zero = jnp.zeros_like(sin)
    cos_t = jnp.tile(cos, (1, 4))
    sin_up = jnp.tile(jnp.concatenate([-sin, zero], -1), (1, 2))
    sin_dn = jnp.tile(jnp.concatenate([zero, sin], -1), (1, 2))
    return jnp.stack([cos_t, sin_up, sin_dn], 0)


def kernel(x_prompt, x_sample, c, state_delta, state_ret, c_ctx, w_mod, b_mod, w_in, conv_w, a_log, dt_bias,
           norm_a_w, gn_w, gn_b, w_o, ln1_w, ln1_b, w_ff1, b_ff1, w_ff2, b_ff2, ln2_w, ln2_b):
    assert w_mod.shape[0] == DEPTH == 1
    n_dec = c.shape[0]
    rows = -(-(1 + n_dec) // SUBLANES) * SUBLANES
    cond = jnp.zeros((rows, D_MODEL), F32).at[0].set(c_ctx).at[1:1 + n_dec].set(c)
    mod = _modulation(cond, w_mod[0], b_mod[0])
    mod3 = mod.reshape(rows, 1, 6 * D_MODEL)

    w = w_in[0]
    c_ab = QKV_W + D_A
    w_in_p = jnp.concatenate([w[:, :c_ab], w[:, c_ab + 2 * N_AB:], w[:, c_ab:c_ab + 2 * N_AB],
                              jnp.zeros((D_MODEL, LANES - 2 * N_AB), F32)], -1).astype(BF16)
    convw = jnp.zeros((SUBLANES, QKV_W), F32).at[:CONV_K].set(conv_w[0])
    par = (jnp.zeros((SUBLANES, LANES), F32).at[0, :N_AB].set(a_log[0].reshape(-1))
           .at[1, :N_AB].set(dt_bias[0].reshape(-1)))
    naw = norm_a_w[0].reshape(1, DV_A)
    gnw = gn_w[0].reshape(1, D_B)
    gnb = gn_b[0].reshape(1, D_B)
    wo = w_o[0].astype(BF16)
    w1 = w_ff1[0].astype(BF16)
    w2 = w_ff2[0].astype(BF16)
    row = lambda v: v[0].reshape(1, -1)

    def trunk(x, row0, row_stride, rope, sd0, sr0, emit_state, tm):
        qkv, qkb, vb, gates, gb = _projection(x, mod3, row0, row_stride, w_in_p, convw, par, rope, tm)
        fwd = _mixer(False, qkv, qkb, vb, gb, None, None, None, None, None, sd0, sr0, emit_state)
        bwd = _mixer(True, qkv, qkb, vb, gb, fwd[0], gates, naw, gnw, gnb, sd0, sr0, emit_state)
        y = _post(bwd[0], x, mod3, row0, row_stride, wo, row(ln1_w), row(ln1_b), w1, row(b_ff1), w2, row(b_ff2),
                  row(ln2_w), row(ln2_b), tm)
        states = [jnp.concatenate([f, b], 2) for f, b in zip(fwd[1:], bwd[1:])]
        return y, states

    y_prompt, (new_sd, new_sr) = trunk(x_prompt, 0, 0, None, None, None, True, 256)
    y_sample, _ = trunk(x_sample, 1, 1, _rope_tables(x_sample.shape[1]), state_delta, state_ret, False, 512)
    return y_prompt, y_sample, new_sd, new_sr
```

```python
import functools
import math

import jax
import jax.numpy as jnp
import numpy as np
from jax import lax
from jax.experimental import pallas as pl
from jax.experimental.pallas import tpu as pltpu

F32 = jnp.float32
BF16 = jnp.bfloat16

D_MODEL = 1024
H_A, DK_A, DV_A = 4, 128, 128
D_A = H_A * DV_A
CONV_K = 5
CHUNK_A = 64
H_B, DK_B, DV_B = 4, 64, 128
D_B = H_B * DV_B
CHUNK_B = 128
GRID_W = 64
ROPE_BASE = 10000.0
D_FF = 4 * D_MODEL
DEPTH = 1
ALPHA = (2.0 * DEPTH) ** 0.25

LANES = 128
SUBLANES = 8
VMEM_LIMIT = 56 * 1024 * 1024

QKV_W = 3 * D_A
COL_GATE_A = QKV_W
COL_QK_B = COL_GATE_A + D_A
QKB_W = 2 * H_B * DK_B
COL_V_B = COL_QK_B + QKB_W
COL_GATE_B = COL_V_B + D_B
COL_AB = COL_GATE_B + D_B
P_W = COL_AB + LANES
N_AB = 2 * H_A

TIME_BLOCK = 256
HALO = SUBLANES
assert CONV_K == 5 and (CONV_K - 1) // 2 <= HALO


def _dot(a, b):
    return jnp.dot(a.astype(BF16), b.astype(BF16), preferred_element_type=F32)


def _dot_nt(a, b):
    return lax.dot_general(a.astype(BF16), b.astype(BF16), (((1,), (1,)), ((), ())),
                           preferred_element_type=F32)


def _dot_tn(a, b):
    return lax.dot_general(a.astype(BF16), b.astype(BF16), (((0,), (0,)), ((), ())),
                           preferred_element_type=F32)


def _split3(x):
    hi = x.astype(BF16)
    r = x - hi.astype(F32)
    mid = r.astype(BF16)
    lo = (r - mid.astype(F32)).astype(BF16)
    return hi, mid, lo


def _silu(x):
    return x * (1.0 / (1.0 + jnp.exp(-x)))


def _layer_norm(x, eps):
    mu = jnp.mean(x, -1, keepdims=True)
    xc = x - mu
    var = jnp.mean(xc * xc, -1, keepdims=True)
    return xc * lax.rsqrt(var + eps)


def _resident(shape):
    return pl.BlockSpec(shape, lambda *_: (0,) * len(shape), pipeline_mode=pl.Buffered(1))


def _mod_kernel(c_ref, w_ref, b_ref, o_ref):
    o_ref[...] = _dot(_silu(c_ref[...]), w_ref[...]) + b_ref[...]


def _modulation(cond, w_mod, b_mod):
    rows = cond.shape[0]
    n = w_mod.shape[1]
    bn = 512
    return pl.pallas_call(
        _mod_kernel,
        grid=(n // bn,),
        in_specs=[pl.BlockSpec((rows, D_MODEL), lambda j: (0, 0)),
                  pl.BlockSpec((D_MODEL, bn), lambda j: (0, j)),
                  pl.BlockSpec((1, bn), lambda j: (0, j))],
        out_specs=pl.BlockSpec((rows, bn), lambda j: (0, j)),
        out_shape=jax.ShapeDtypeStruct((rows, n), F32),
        compiler_params=pltpu.CompilerParams(dimension_semantics=("arbitrary",)),
        name="modulation",
    )(cond, w_mod, b_mod.reshape(1, n))


def _proj_kernel(use_rope, tm, nt, *refs):
    if use_rope:
        (x_ref, xp_ref, xn_ref, mod_ref, w_ref, convw_ref, par_ref, rope_ref,
         qkv_ref, qkb_ref, vb_ref, gates_ref, gb_ref) = refs
    else:
        (x_ref, xp_ref, xn_ref, mod_ref, w_ref, convw_ref, par_ref,
         qkv_ref, qkb_ref, vb_ref, gates_ref, gb_ref) = refs
    i = pl.program_id(1)
    m = mod_ref[0]
    sh1 = m[:, 0:D_MODEL]
    sc1 = m[:, D_MODEL:2 * D_MODEL]

    def modulated(x):
        return _layer_norm(x, 1e-6) * (1.0 + sc1) + sh1

    h = jnp.concatenate([modulated(xp_ref[0]), modulated(x_ref[0]), modulated(xn_ref[0])], 0)
    p = jnp.dot(h.astype(BF16), w_ref[...], preferred_element_type=F32)

    rows = tm + 2 * HALO
    xe = jnp.concatenate([jnp.where(i > 0, p[0:HALO, 0:QKV_W], 0.0), p[HALO:HALO + tm, 0:QKV_W],
                          jnp.where(i < nt - 1, p[HALO + tm:, 0:QKV_W], 0.0)], 0)
    z = [xe * convw_ref[kk:kk + 1, :] for kk in range(CONV_K)]
    back = pltpu.roll(z[1] + pltpu.roll(z[0], 1, 0), 1, 0)
    ahead = pltpu.roll(z[3] + pltpu.roll(z[4], rows - 1, 0), rows - 1, 0)
    qkv = _silu((z[2] + back + ahead)[HALO:HALO + tm])

    def l2n(x):
        return x * lax.rsqrt(jnp.sum(x * x, -1, keepdims=True) + 1e-6)

    for hd in range(H_A):
        c = hd * DK_A
        qkv_ref[0, :, c:c + DK_A] = l2n(qkv[:, c:c + DK_A]) * (DK_A ** -0.5)
        qkv_ref[0, :, D_A + c:D_A + c + DK_A] = l2n(qkv[:, D_A + c:D_A + c + DK_A])
    qkv_ref[0, :, 2 * D_A:] = qkv[:, 2 * D_A:]

    rest = p[HALO:HALO + tm]
    gates_ref[0, :, 0:D_A] = _silu(rest[:, COL_GATE_A:COL_GATE_A + D_A])
    gates_ref[0, :, D_A:] = _silu(rest[:, COL_GATE_B:COL_GATE_B + D_B])
    vb_ref[0] = rest[:, COL_V_B:COL_V_B + D_B].astype(BF16)

    for c4 in range(QKB_W // LANES):
        x = rest[:, COL_QK_B + c4 * LANES:COL_QK_B + (c4 + 1) * LANES]
        if use_rope:
            x = (x * rope_ref[0] + pltpu.roll(x, LANES - DK_B // 2, 1) * rope_ref[1]
                 + pltpu.roll(x, DK_B // 2, 1) * rope_ref[2])
        if c4 < H_B * DK_B // LANES:
            x = x * (DK_B ** -0.5)
        qkb_ref[0, :, c4 * LANES:(c4 + 1) * LANES] = x

    ab = rest[:, COL_AB:COL_AB + LANES]
    z = ab + par_ref[1:2, :]
    softplus = jnp.maximum(z, 0.0) + jnp.log1p(jnp.exp(-jnp.abs(z)))
    g_all = -jnp.exp(par_ref[0:1, :]) * softplus
    beta_all = 1.0 / (1.0 + jnp.exp(-ab))
    lane = lax.broadcasted_iota(jnp.int32, (tm, LANES), 1)
    gb_ref[0] = jnp.where(lane < N_AB, g_all, jnp.where(lane < 2 * N_AB, beta_all, 0.0))


def _projection(x, mod3, row0, row_stride, w_in_p, convw, par, rope, tm):
    bt, t, _ = x.shape
    nt = t // tm
    hpt = tm // HALO
    nh = t // HALO
    use_rope = rope is not None
    in_specs = [pl.BlockSpec((1, tm, D_MODEL), lambda b, i: (b, i, 0)),
                pl.BlockSpec((1, HALO, D_MODEL), lambda b, i: (b, jnp.maximum(i * hpt - 1, 0), 0)),
                pl.BlockSpec((1, HALO, D_MODEL), lambda b, i: (b, jnp.minimum((i + 1) * hpt, nh - 1), 0)),
                pl.BlockSpec((1, 1, 6 * D_MODEL), lambda b, i: (row0 + row_stride * b, 0, 0)),
                _resident((D_MODEL, P_W)),
                _resident((SUBLANES, QKV_W)),
                _resident((SUBLANES, LANES))]
    args = [x, x, x, mod3, w_in_p, convw, par]
    if use_rope:
        in_specs.append(pl.BlockSpec((3, tm, LANES), lambda b, i: (0, i, 0)))
        args.append(rope)
    widths = (QKV_W, QKB_W, D_B, D_A + D_B, LANES)
    dtypes = (F32, F32, BF16, F32, F32)
    return pl.pallas_call(
        functools.partial(_proj_kernel, use_rope, tm, nt),
        grid=(bt, nt),
        in_specs=in_specs,
        out_specs=[pl.BlockSpec((1, tm, w), lambda b, i: (b, i, 0)) for w in widths],
        out_shape=[jax.ShapeDtypeStruct((bt, t, w), dt) for w, dt in zip(widths, dtypes)],
        compiler_params=pltpu.CompilerParams(dimension_semantics=("arbitrary", "arbitrary"),
                                             vmem_limit_bytes=VMEM_LIMIT),
        name="projection",
    )(*args)


def _post_kernel(o_ref, x_ref, mod_ref, wo_ref, ln1w_ref, ln1b_ref, w1_ref, b1_ref, w2_ref, b2_ref,
                 ln2w_ref, ln2b_ref, y_ref):
    m = mod_ref[0]
    g1 = m[:, 2 * D_MODEL:3 * D_MODEL]
    sh2 = m[:, 3 * D_MODEL:4 * D_MODEL]
    sc2 = m[:, 4 * D_MODEL:5 * D_MODEL]
    g2 = m[:, 5 * D_MODEL:6 * D_MODEL]
    y = jnp.dot(o_ref[0], wo_ref[...], preferred_element_type=F32)
    x1 = _layer_norm(ALPHA * x_ref[0] + g1 * y, 1e-6) * ln1w_ref[...] + ln1b_ref[...]
    h = _layer_norm(x1, 1e-6) * (1.0 + sc2) + sh2
    a = jnp.dot(h.astype(BF16), w1_ref[...], preferred_element_type=F32) + b1_ref[...]
    a = jnp.square(jnp.maximum(a, 0.0))
    f = jnp.dot(a.astype(BF16), w2_ref[...], preferred_element_type=F32) + b2_ref[...]
    y_ref[0] = _layer_norm(ALPHA * x1 + g2 * f, 1e-6) * ln2w_ref[...] + ln2b_ref[...]


def _post(o, x, mod3, row0, row_stride, wo, ln1w, ln1b, w1, b1, w2, b2, ln2w, ln2b, tm):
    bt, t, _ = x.shape
    return pl.pallas_call(
        _post_kernel,
        grid=(bt, t // tm),
        in_specs=[pl.BlockSpec((1, tm, D_MODEL), lambda b, i: (b, i, 0)),
                  pl.BlockSpec((1, tm, D_MODEL), lambda b, i: (b, i, 0)),
                  pl.BlockSpec((1, 1, 6 * D_MODEL), lambda b, i: (row0 + row_stride * b, 0, 0)),
                  _resident((D_MODEL, D_MODEL)),
                  _resident((1, D_MODEL)),
                  _resident((1, D_MODEL)),
                  _resident((D_MODEL, D_FF)),
                  _resident((1, D_FF)),
                  _resident((D_FF, D_MODEL)),
                  _resident((1, D_MODEL)),
                  _resident((1, D_MODEL)),
                  _resident((1, D_MODEL))],
        out_specs=pl.BlockSpec((1, tm, D_MODEL), lambda b, i: (b, i, 0)),
        out_shape=jax.ShapeDtypeStruct((bt, t, D_MODEL), F32),
        compiler_params=pltpu.CompilerParams(dimension_semantics=("arbitrary", "arbitrary"),
                                             vmem_limit_bytes=VMEM_LIMIT),
        name="post",
    )(o, x, mod3, wo, ln1w, ln1b, w1, b1, w2, b2, ln2w, ln2b)


_LOG_GAMMA = [math.log1p(-(2.0 ** (-5.0 - h))) for h in range(H_B)]

SOLVE_BASE = 8
SEQ_PER_STEP = 2
CAT_W = H_A * CHUNK_A


def _head_blockdiag(x, mask):
    return jnp.where(mask, jnp.concatenate([x] * H_A, 0), 0.0).astype(BF16)


def _cat_inverses(ms, bd_mask):
    i = lax.broadcasted_iota(jnp.int32, (CHUNK_A, CAT_W), 0)
    j = lax.broadcasted_iota(jnp.int32, (CHUNK_A, CAT_W), 1) % CHUNK_A
    same = lambda s: (i // s) == (j // s)
    eye = jnp.where(i == j, 1.0, 0.0)
    mul = lambda a, bd: jnp.dot(a.astype(BF16), bd, preferred_element_type=F32)
    ps = [jnp.where(same(SOLVE_BASE), -m, 0.0) for m in ms]
    ts = [eye + p for p in ps]
    bds = [_head_blockdiag(p, bd_mask) for p in ps]
    for _ in range(int(math.log2(SOLVE_BASE)) - 1):
        ps = [mul(p, bd) for p, bd in zip(ps, bds)]
        bds = [_head_blockdiag(p, bd_mask) for p in ps]
        ts = [t + mul(t, bd) for t, bd in zip(ts, bds)]
    s = SOLVE_BASE
    while s < CHUNK_A:
        sel = same(2 * s) & jnp.logical_not(same(s))
        bds = [_head_blockdiag(t, bd_mask) for t in ts]
        ys = [mul(jnp.where(sel, m, 0.0), bd) for m, bd in zip(ms, bds)]
        bds = [_head_blockdiag(y, bd_mask) for y in ys]
        ts = [t - mul(t, bd) for t, bd in zip(ts, bds)]
        s *= 2
    return ts


def _mixer_kernel(rev, has_init, emit_state, nblk, *refs):
    NS = SEQ_PER_STEP
    TB = TIME_BLOCK
    d = 1 if rev else 0
    j = pl.program_id(1)
    qkv_ref, qkb_ref, vb_ref, gb_ref = refs[:4]
    pos_ = 4
    if rev:
        of_ref, gates_ref, naw_ref, gnw_ref, gnb_ref = refs[pos_:pos_ + 5]
        pos_ += 5
    if has_init:
        sd0_ref, sr0_ref = refs[pos_:pos_ + 2]
        pos_ += 2
    o_ref = refs[pos_]
    pos_ += 1
    if emit_state:
        sd_ref, sr_ref = refs[pos_:pos_ + 2]
        pos_ += 2
    sa_scr, sr_scr = refs[pos_:pos_ + 2]

    @pl.when(j == 0)
    def _():
        if has_init:
            for bb in range(NS):
                for h in range(H_A):
                    sa_scr[bb, :, h * DV_A:(h + 1) * DV_A] = sd0_ref[bb, 0, 0, h]
                sr_scr[bb] = jnp.zeros(sr_scr.shape[1:], F32)
                for h in range(H_B):
                    lo = (h % 2) * DK_B
                    sr_scr[bb, h // 2, lo:lo + DK_B, (h % 2) * DV_B:(h % 2 + 1) * DV_B] = sr0_ref[bb, 0, 0, h]
        else:
            sa_scr[...] = jnp.zeros(sa_scr.shape, F32)
            sr_scr[...] = jnp.zeros(sr_scr.shape, F32)

    ri = lax.broadcasted_iota(jnp.int32, (TB, TB), 0)
    ci = lax.broadcasted_iota(jnp.int32, (TB, TB), 1)
    cum = jnp.where(((ri // CHUNK_A) == (ci // CHUNK_A)) & ((ci >= ri) if rev else (ci <= ri)),
                    1.0, 0.0).astype(BF16)
    er = lax.broadcasted_iota(jnp.int32, (LANES, CAT_W), 0)
    ec = lax.broadcasted_iota(jnp.int32, (LANES, CAT_W), 1)
    esel_c = jnp.where(er == d * H_A + ec // CHUNK_A, 1.0, 0.0).astype(BF16)
    lane_tb = lax.broadcasted_iota(jnp.int32, (TB, LANES), 1)
    i64 = lax.broadcasted_iota(jnp.int32, (CHUNK_A, CAT_W), 0)
    j64 = lax.broadcasted_iota(jnp.int32, (CHUNK_A, CAT_W), 1) % CHUNK_A
    tri = (i64 <= j64) if rev else (i64 >= j64)
    strict = (i64 < j64) if rev else (i64 > j64)
    bd_mask = (lax.broadcasted_iota(jnp.int32, (CAT_W, CAT_W), 0) // CHUNK_A
               == lax.broadcasted_iota(jnp.int32, (CAT_W, CAT_W), 1) // CHUNK_A)
    bdk_mask = (lax.broadcasted_iota(jnp.int32, (CAT_W, D_A), 0) // CHUNK_A
                == lax.broadcasted_iota(jnp.int32, (CAT_W, D_A), 1) // DK_A)
    lane_head = lax.broadcasted_iota(jnp.int32, (CHUNK_A, CAT_W), 1) // CHUNK_A
    nca = TB // CHUNK_A
    order_a = list(range(nca - 1, -1, -1)) if rev else list(range(nca))

    st = []
    for bb in range(NS):
        gbk = gb_ref[bb]
        gc = None
        for part in _split3(gbk):
            t_ = jnp.dot(cum, part, preferred_element_type=F32)
            gc = t_ if gc is None else gc + t_
        gct = None
        for part in _split3(gbk.T):
            t_ = lax.dot_general(part, cum, (((1,), (1,)), ((), ())), preferred_element_type=F32)
            gct = t_ if gct is None else gct + t_
        gcb = jnp.where(lane_tb < N_AB, gc, gbk)
        gcs = None
        for part in _split3(gcb):
            t_ = jnp.dot(part, esel_c, preferred_element_type=F32)
            gcs = t_ if gcs is None else gcs + t_
        gc5a = jnp.concatenate([jnp.broadcast_to(gcb[:, d * H_A + h:d * H_A + h + 1], (TB, DK_A))
                                for h in range(H_A)], 1)
        b5a = jnp.concatenate([jnp.broadcast_to(gcb[:, N_AB + d * H_A + h:N_AB + d * H_A + h + 1], (TB, DK_A))
                               for h in range(H_A)], 1)
        for n in order_a:
            r0 = n * CHUNK_A
            last = r0 if rev else r0 + CHUNK_A - 1
            it = dict(bb=bb, r0=r0, gc5=gc5a[r0:r0 + CHUNK_A], tot5=gc5a[last:last + 1], b5=b5a[r0:r0 + CHUNK_A])
            q = qkv_ref[bb, r0:r0 + CHUNK_A, 0:D_A]
            k = qkv_ref[bb, r0:r0 + CHUNK_A, D_A:2 * D_A]
            gcc = gcs[r0:r0 + CHUNK_A]
            gcr = jnp.concatenate([gct[d * H_A + h:d * H_A + h + 1, r0:r0 + CHUNK_A] for h in range(H_A)], 1)
            decay = jnp.where(tri, jnp.exp(jnp.where(tri, gcc - gcr, 0.0)), 0.0)
            mq = _dot_nt(jnp.concatenate([k * it["b5"], q], 0), _head_blockdiag(k, bdk_mask))
            it["m"] = jnp.where(strict, mq[:CHUNK_A] * decay, 0.0)
            it["qk"] = jnp.where(tri, mq[CHUNK_A:] * decay, 0.0)
            st.append(it)
    inv = _cat_inverses([it["m"] for it in st], bd_mask)
    for it, t in zip(st, inv):
        bb, r0 = it["bb"], it["r0"]
        k = qkv_ref[bb, r0:r0 + CHUNK_A, D_A:2 * D_A]
        v = qkv_ref[bb, r0:r0 + CHUNK_A, 2 * D_A:3 * D_A]
        vb5 = v * it["b5"]
        kbe5 = (k * it["b5"]) * jnp.exp(it["gc5"])
        rhs = jnp.concatenate(
            [jnp.concatenate([vb5[:, h * DV_A:(h + 1) * DV_A], kbe5[:, h * DK_A:(h + 1) * DK_A]], 1)
             for h in range(H_A)], 0).astype(BF16)
        lhs = jnp.concatenate([jnp.where(lane_head == h, t, 0.0) for h in range(H_A)], 0)
        sol = jnp.dot(lhs.astype(BF16), rhs, preferred_element_type=F32)
        it["u"] = jnp.concatenate([sol[h * CHUNK_A:(h + 1) * CHUNK_A, :DV_A] for h in range(H_A)], 1)
        it["w"] = jnp.concatenate([sol[h * CHUNK_A:(h + 1) * CHUNK_A, DV_A:] for h in range(H_A)], 1)

    lane = lax.broadcasted_iota(jnp.int32, (CHUNK_B, LANES), 1)
    pi = lax.broadcasted_iota(jnp.int32, (CHUNK_B, CHUNK_B), 0)
    pj = lax.broadcasted_iota(jnp.int32, (CHUNK_B, CHUNK_B), 1)
    pos = lax.broadcasted_iota(jnp.int32, (CHUNK_B, 1), 0)
    if rev:
        pi, pj, pos = CHUNK_B - 1 - pi, CHUNK_B - 1 - pj, CHUNK_B - 1 - pos
    pdiff = (pi - pj).astype(F32)
    posf = pos.astype(F32)
    first_half = (lane // DK_B) == 0
    own_block = (lax.broadcasted_iota(jnp.int32, (LANES, 2 * DV_B), 0) // DK_B
                 == lax.broadcasted_iota(jnp.int32, (LANES, 2 * DV_B), 1) // DV_B)
    ncb = TB // CHUNK_B
    order_b = range(ncb - 1, -1, -1) if rev else range(ncb)
    rt = []
    for pp in range(H_B // 2):
        lgs = [_LOG_GAMMA[H_B - 1 - h] if rev else _LOG_GAMMA[h] for h in (2 * pp, 2 * pp + 1)]
        d_mask = jnp.concatenate([jnp.where(pdiff >= 0, jnp.exp(lg * jnp.maximum(pdiff, 0.0)), 0.0) for lg in lgs], 1)
        xi = jnp.where(first_half, jnp.exp(lgs[0] * (posf + 1.0)), jnp.exp(lgs[1] * (posf + 1.0)))
        zeta = jnp.where(first_half, jnp.exp(lgs[0] * (CHUNK_B - 1.0 - posf)), jnp.exp(lgs[1] * (CHUNK_B - 1.0 - posf)))
        col_head = lax.broadcasted_iota(jnp.int32, (1, 2 * DV_B), 1) // DV_B
        g_chunk = jnp.where(col_head == 0, math.exp(lgs[0] * CHUNK_B), math.exp(lgs[1] * CHUNK_B))
        for bb in range(NS):
            for n in order_b:
                rt.append(dict(bb=bb, r0=n * CHUNK_B, pp=pp, g_chunk=g_chunk, d_mask=d_mask, xi=xi, zeta=zeta))

    def retention_independent(it):
        bb, r0, pp = it["bb"], it["r0"], it["pp"]
        q = qkb_ref[bb, r0:r0 + CHUNK_B, pp * LANES:(pp + 1) * LANES]
        k = qkb_ref[bb, r0:r0 + CHUNK_B, H_B * DK_B + pp * LANES:H_B * DK_B + (pp + 1) * LANES]
        v0 = vb_ref[bb, r0:r0 + CHUNK_B, 2 * pp * DV_B:(2 * pp + 1) * DV_B]
        v1 = vb_ref[bb, r0:r0 + CHUNK_B, (2 * pp + 1) * DV_B:(2 * pp + 2) * DV_B]
        zb = jnp.zeros((CHUNK_B, DV_B), BF16)
        ksplit = jnp.concatenate([jnp.where(first_half, k, 0.0), jnp.where(first_half, 0.0, k)], 0)
        scores = _dot_nt(q, ksplit) * it["d_mask"]
        v_bd = jnp.concatenate([jnp.concatenate([v0, zb], 1), jnp.concatenate([zb, v1], 1)], 0)
        it["inner"] = jnp.dot(scores.astype(BF16), v_bd, preferred_element_type=F32)
        it["qx"] = q * it["xi"]
        ds = lax.dot_general((k * it["zeta"]).astype(BF16), jnp.concatenate([v0, v1], 1),
                             (((0,), (0,)), ((), ())), preferred_element_type=F32)
        it["ds"] = jnp.where(own_block, ds, 0.0)

    def finish(bb, r0, rows, c0, width, o):
        if not rev:
            o_ref[bb, r0:r0 + rows, c0:c0 + width] = o
            return
        o = o + of_ref[bb, r0:r0 + rows, c0:c0 + width]
        outs = []
        for hh in range(width // LANES):
            x = o[:, hh * LANES:(hh + 1) * LANES]
            if c0 < D_A:
                x = x * lax.rsqrt(jnp.mean(x * x, -1, keepdims=True) + 1e-6) * naw_ref[...]
            else:
                cb = c0 - D_A + hh * LANES
                x = _layer_norm(x, 1e-5) * gnw_ref[:, cb:cb + LANES] + gnb_ref[:, cb:cb + LANES]
            outs.append(x)
        o = outs[0] if len(outs) == 1 else jnp.concatenate(outs, 1)
        o_ref[bb, r0:r0 + rows, c0:c0 + width] = (o * gates_ref[bb, r0:r0 + rows, c0:c0 + width]).astype(BF16)

    per_step = -(-len(rt) // nca)
    for c in range(nca):
        grp_items = [st[bb * nca + c] for bb in range(NS)]
        s_old = [sa_scr[it["bb"]] for it in grp_items]
        prods = []
        for it, s in zip(grp_items, s_old):
            q_dec = qkv_ref[it["bb"], it["r0"]:it["r0"] + CHUNK_A, 0:D_A] * jnp.exp(it["gc5"])
            pr = []
            for pp in range(H_A // 2):
                lo = pp * 2 * DV_A
                z = jnp.zeros((DK_A, DV_A), F32)
                bds = jnp.concatenate([jnp.concatenate([s[:, lo:lo + DV_A], z], 1),
                                       jnp.concatenate([z, s[:, lo + DV_A:lo + 2 * DV_A]], 1)], 0)
                lhs = jnp.concatenate([it["w"][:, lo:lo + 2 * DK_A], q_dec[:, lo:lo + 2 * DK_A]], 0)
                pr.append(_dot(lhs, bds))
            prods.append(jnp.concatenate(pr, 1))
        for it in rt[c * per_step:(c + 1) * per_step]:
            retention_independent(it)
        v_new = [it["u"] - pr[:CHUNK_A] for it, pr in zip(grp_items, prods)]
        vbd = [_head_blockdiag(v, bdk_mask) for v in v_new]
        upd = []
        for it, vb_ in zip(grp_items, vbd):
            k_dec = (qkv_ref[it["bb"], it["r0"]:it["r0"] + CHUNK_A, D_A:2 * D_A]
                     * jnp.exp(it["tot5"] - it["gc5"]))
            kd_stack = jnp.concatenate([k_dec[:, h * DK_A:(h + 1) * DK_A] for h in range(H_A)], 0)
            upd.append(lax.dot_general(kd_stack.astype(BF16), vb_, (((0,), (0,)), ((), ())),
                                       preferred_element_type=F32))
        for it, s, x in zip(grp_items, s_old, upd):
            sa_scr[it["bb"]] = s * jnp.exp(it["tot5"]) + x
        for it, pr, vb_ in zip(grp_items, prods, vbd):
            o = pr[CHUNK_A:] + jnp.dot(it["qk"].astype(BF16), vb_, preferred_element_type=F32)
            finish(it["bb"], it["r0"], CHUNK_A, 0, D_A, o)

    for it in rt:
        bb, pp = it["bb"], it["pp"]
        s = sr_scr[bb, pp]
        o = it["inner"] + _dot(it["qx"], s)
        sr_scr[bb, pp] = s * it["g_chunk"] + it["ds"]
        finish(bb, it["r0"], CHUNK_B, D_A + 2 * pp * DV_B, 2 * DV_B, o)

    if emit_state:
        @pl.when(j == nblk - 1)
        def _():
            for bb in range(NS):
                for h in range(H_A):
                    sd_ref[bb, 0, 0, h] = sa_scr[bb, :, h * DV_A:(h + 1) * DV_A]
                for h in range(H_B):
                    lo = (h % 2) * DK_B
                    sr_ref[bb, 0, 0, h] = sr_scr[bb, h // 2, lo:lo + DK_B, (h % 2) * DV_B:(h % 2 + 1) * DV_B]


def _mixer(rev, qkv, qkb, vb, gb, o_fwd, gates, naw, gnw, gnb, sd0, sr0, emit_state):
    bt, t, _ = qkv.shape
    NS = SEQ_PER_STEP
    TB = TIME_BLOCK
    nblk = t // TB
    has_init = sd0 is not None
    d = 1 if rev else 0
    tb_of = (lambda j: nblk - 1 - j) if rev else (lambda j: j)
    tok = lambda w: pl.BlockSpec((NS, TB, w), lambda b, j: (b, tb_of(j), 0))
    in_specs = [tok(QKV_W), tok(QKB_W), tok(D_B), tok(LANES)]
    args = [qkv, qkb, vb, gb]
    if rev:
        in_specs += [tok(D_A + D_B), tok(D_A + D_B), _resident((1, DV_A)), _resident((1, D_B)), _resident((1, D_B))]
        args += [o_fwd, gates, naw, gnw, gnb]
    if has_init:
        in_specs.append(pl.BlockSpec((NS, 1, 1, H_A, DK_A, DV_A), lambda b, j: (b, 0, d, 0, 0, 0)))
        in_specs.append(pl.BlockSpec((NS, 1, 1, H_B, DK_B, DV_B), lambda b, j: (b, 0, d, 0, 0, 0)))
        args += [sd0, sr0]
    out_specs = [tok(D_A + D_B)]
    out_shape = [jax.ShapeDtypeStruct((bt, t, D_A + D_B), BF16 if rev else F32)]
    if emit_state:
        out_specs.append(pl.BlockSpec((NS, 1, 1, H_A, DK_A, DV_A), lambda b, j: (b, 0, 0, 0, 0, 0)))
        out_specs.append(pl.BlockSpec((NS, 1, 1, H_B, DK_B, DV_B), lambda b, j: (b, 0, 0, 0, 0, 0)))
        out_shape.append(jax.ShapeDtypeStruct((bt, DEPTH, 1, H_A, DK_A, DV_A), F32))
        out_shape.append(jax.ShapeDtypeStruct((bt, DEPTH, 1, H_B, DK_B, DV_B), F32))
    scratch = [pltpu.VMEM((NS, DK_A, H_A * DV_A), F32),
               pltpu.VMEM((NS, H_B // 2, LANES, 2 * DV_B), F32)]
    return pl.pallas_call(
        functools.partial(_mixer_kernel, rev, has_init, emit_state, nblk),
        grid=(bt // NS, nblk),
        in_specs=in_specs,
        out_specs=out_specs,
        out_shape=out_shape,
        scratch_shapes=scratch,
        compiler_params=pltpu.CompilerParams(dimension_semantics=("arbitrary", "arbitrary"),
                                             vmem_limit_bytes=VMEM_LIMIT),
        name="mixer_bwd" if rev else "mixer_fwd",
    )(*args)


def _rope_tables(t):
    rows = t // GRID_W
    r = np.repeat(np.arange(rows, dtype=np.float32), GRID_W)
    col = np.tile(np.arange(GRID_W, dtype=np.float32), rows)
    nf = DK_B // 4
    inv = (np.float32(ROPE_BASE) ** (-np.arange(nf, dtype=np.float32) / np.float32(nf))).astype(np.float32)
    ang = jnp.asarray(np.concatenate([r[:, None] * inv, col[:, None] * inv], -1).astype(np.float32))
    cos, sin = jnp.cos(ang), jnp.sin(ang)
    zero = jnp.zeros_like(sin)
    cos_t = jnp.tile(cos, (1, 4))
    sin_up = jnp.tile(jnp.concatenate([-sin, zero], -1), (1, 2))
    sin_dn = jnp.tile(jnp.concatenate([zero, sin], -1), (1, 2))
    return jnp.stack([cos_t, sin_up, sin_dn], 0)


def kernel(x_prompt, x_sample, c, state_delta, state_ret, c_ctx, w_mod, b_mod, w_in, conv_w, a_log, dt_bias,
           norm_a_w, gn_w, gn_b, w_o, ln1_w, ln1_b, w_ff1, b_ff1, w_ff2, b_ff2, ln2_w, ln2_b):
    assert w_mod.shape[0] == DEPTH == 1
    n_dec = c.shape[0]
    rows = -(-(1 + n_dec) // SUBLANES) * SUBLANES
    cond = jnp.zeros((rows, D_MODEL), F32).at[0].set(c_ctx).at[1:1 + n_dec].set(c)
    mod = _modulation(cond, w_mod[0], b_mod[0])
    mod3 = mod.reshape(rows, 1, 6 * D_MODEL)

    w = w_in[0]
    c_ab = QKV_W + D_A
    w_in_p = (jnp.zeros((D_MODEL, P_W), BF16)
              .at[:, :c_ab].set(w[:, :c_ab].astype(BF16))
              .at[:, c_ab:COL_AB].set(w[:, c_ab + 2 * N_AB:].astype(BF16))
              .at[:, COL_AB:COL_AB + 2 * N_AB].set(w[:, c_ab:c_ab + 2 * N_AB].astype(BF16)))
    convw = jnp.zeros((SUBLANES, QKV_W), F32).at[:CONV_K].set(conv_w[0])
    par = (jnp.zeros((SUBLANES, LANES), F32).at[0, :N_AB].set(a_log[0].reshape(-1))
           .at[1, :N_AB].set(dt_bias[0].reshape(-1)))
    naw = norm_a_w[0].reshape(1, DV_A)
    gnw = gn_w[0].reshape(1, D_B)
    gnb = gn_b[0].reshape(1, D_B)
    wo = w_o[0].astype(BF16)
    w1 = w_ff1[0].astype(BF16)
    w2 = w_ff2[0].astype(BF16)
    row = lambda v: v[0].reshape(1, -1)

    def trunk(x, row0, row_stride, rope, sd0, sr0, emit_state, tm):
        qkv, qkb, vb, gates, gb = _projection(x, mod3, row0, row_stride, w_in_p, convw, par, rope, tm)
        fwd = _mixer(False, qkv, qkb, vb, gb, None, None, None, None, None, sd0, sr0, emit_state)
        bwd = _mixer(True, qkv, qkb, vb, gb, fwd[0], gates, naw, gnw, gnb, sd0, sr0, emit_state)
        y = _post(bwd[0], x, mod3, row0, row_stride, wo, row(ln1_w), row(ln1_b), w1, row(b_ff1), w2, row(b_ff2),
                  row(ln2_w), row(ln2_b), tm)
        states = [jnp.concatenate([f, b], 2) for f, b in zip(fwd[1:], bwd[1:])]
        return y, states

    y_prompt, (new_sd, new_sr) = trunk(x_prompt, 0, 0, None, None, None, True, 256)
    y_sample, _ = trunk(x_sample, 1, 1, _rope_tables(x_sample.shape[1]), state_delta, state_ret, False, 512)
    return y_prompt, y_sample, new_sd, new_sr
```

```python
import functools
import math

import jax
import jax.numpy as jnp
import numpy as np
from jax import lax
from jax.experimental import pallas as pl
from jax.experimental.pallas import tpu as pltpu

F32 = jnp.float32
BF16 = jnp.bfloat16

D_MODEL = 1024
H_A, DK_A, DV_A = 4, 128, 128
D_A = H_A * DV_A
CONV_K = 5
CHUNK_A = 64
H_B, DK_B, DV_B = 4, 64, 128
D_B = H_B * DV_B
CHUNK_B = 128
GRID_W = 64
ROPE_BASE = 10000.0
D_FF = 4 * D_MODEL
DEPTH = 1
ALPHA = (2.0 * DEPTH) ** 0.25

LANES = 128
SUBLANES = 8
VMEM_LIMIT = 56 * 1024 * 1024

QKV_W = 3 * D_A
COL_GATE_A = QKV_W
COL_QK_B = COL_GATE_A + D_A
QKB_W = 2 * H_B * DK_B
COL_V_B = COL_QK_B + QKB_W
COL_GATE_B = COL_V_B + D_B
COL_AB = COL_GATE_B + D_B
P_W = COL_AB + LANES
N_AB = 2 * H_A

TIME_BLOCK = 256
HALO = SUBLANES
assert CONV_K == 5 and (CONV_K - 1) // 2 <= HALO


def _dot(a, b):
    return jnp.dot(a.astype(BF16), b.astype(BF16), preferred_element_type=F32)


def _dot_nt(a, b):
    return lax.dot_general(a.astype(BF16), b.astype(BF16), (((1,), (1,)), ((), ())),
                           preferred_element_type=F32)


def _dot_tn(a, b):
    return lax.dot_general(a.astype(BF16), b.astype(BF16), (((0,), (0,)), ((), ())),
                           preferred_element_type=F32)


def _split3(x):
    hi = x.astype(BF16)
    r = x - hi.astype(F32)
    mid = r.astype(BF16)
    lo = (r - mid.astype(F32)).astype(BF16)
    return hi, mid, lo


def _silu(x):
    return x * (1.0 / (1.0 + jnp.exp(-x)))


def _layer_norm(x, eps):
    mu = jnp.mean(x, -1, keepdims=True)
    xc = x - mu
    var = jnp.mean(xc * xc, -1, keepdims=True)
    return xc * lax.rsqrt(var + eps)


def _resident(shape):
    return pl.BlockSpec(shape, lambda *_: (0,) * len(shape), pipeline_mode=pl.Buffered(1))


def _mod_kernel(c_ref, w_ref, b_ref, o_ref):
    o_ref[...] = _dot(_silu(c_ref[...]), w_ref[...]) + b_ref[...]


def _modulation(cond, w_mod, b_mod):
    rows = cond.shape[0]
    n = w_mod.shape[1]
    bn = 512
    return pl.pallas_call(
        _mod_kernel,
        grid=(n // bn,),
        in_specs=[pl.BlockSpec((rows, D_MODEL), lambda j: (0, 0)),
                  pl.BlockSpec((D_MODEL, bn), lambda j: (0, j)),
                  pl.BlockSpec((1, bn), lambda j: (0, j))],
        out_specs=pl.BlockSpec((rows, bn), lambda j: (0, j)),
        out_shape=jax.ShapeDtypeStruct((rows, n), F32),
        compiler_params=pltpu.CompilerParams(dimension_semantics=("arbitrary",)),
        name="modulation",
    )(cond, w_mod, b_mod.reshape(1, n))


def _proj_kernel(use_rope, tm, nt, *refs):
    if use_rope:
        (x_ref, xp_ref, xn_ref, mod_ref, w_ref, convw_ref, par_ref, rope_ref,
         qkv_ref, qkb_ref, vb_ref, gates_ref, gb_ref) = refs
    else:
        (x_ref, xp_ref, xn_ref, mod_ref, w_ref, convw_ref, par_ref,
         qkv_ref, qkb_ref, vb_ref, gates_ref, gb_ref) = refs
    i = pl.program_id(1)
    m = mod_ref[0]
    sh1 = m[:, 0:D_MODEL]
    sc1 = m[:, D_MODEL:2 * D_MODEL]

    def modulated(x):
        return _layer_norm(x, 1e-6) * (1.0 + sc1) + sh1

    h = jnp.concatenate([modulated(xp_ref[0]), modulated(x_ref[0]), modulated(xn_ref[0])], 0)
    p = jnp.dot(h.astype(BF16), w_ref[...], preferred_element_type=F32)

    rows = tm + 2 * HALO
    xe = jnp.concatenate([jnp.where(i > 0, p[0:HALO, 0:QKV_W], 0.0), p[HALO:HALO + tm, 0:QKV_W],
                          jnp.where(i < nt - 1, p[HALO + tm:, 0:QKV_W], 0.0)], 0)
    z = [xe * convw_ref[kk:kk + 1, :] for kk in range(CONV_K)]
    back = pltpu.roll(z[1] + pltpu.roll(z[0], 1, 0), 1, 0)
    ahead = pltpu.roll(z[3] + pltpu.roll(z[4], rows - 1, 0), rows - 1, 0)
    qkv = _silu((z[2] + back + ahead)[HALO:HALO + tm])

    def l2n(x):
        return x * lax.rsqrt(jnp.sum(x * x, -1, keepdims=True) + 1e-6)

    for hd in range(H_A):
        c = hd * DK_A
        qkv_ref[0, :, c:c + DK_A] = l2n(qkv[:, c:c + DK_A]) * (DK_A ** -0.5)
        qkv_ref[0, :, D_A + c:D_A + c + DK_A] = l2n(qkv[:, D_A + c:D_A + c + DK_A])
    qkv_ref[0, :, 2 * D_A:] = qkv[:, 2 * D_A:]

    rest = p[HALO:HALO + tm]
    gates_ref[0, :, 0:D_A] = _silu(rest[:, COL_GATE_A:COL_GATE_A + D_A])
    gates_ref[0, :, D_A:] = _silu(rest[:, COL_GATE_B:COL_GATE_B + D_B])
    vb_ref[0] = rest[:, COL_V_B:COL_V_B + D_B].astype(BF16)

    for c4 in range(QKB_W // LANES):
        x = rest[:, COL_QK_B + c4 * LANES:COL_QK_B + (c4 + 1) * LANES]
        if use_rope:
            x = (x * rope_ref[0] + pltpu.roll(x, LANES - DK_B // 2, 1) * rope_ref[1]
                 + pltpu.roll(x, DK_B // 2, 1) * rope_ref[2])
        if c4 < H_B * DK_B // LANES:
            x = x * (DK_B ** -0.5)
        qkb_ref[0, :, c4 * LANES:(c4 + 1) * LANES] = x

    ab = rest[:, COL_AB:COL_AB + LANES]
    z = ab + par_ref[1:2, :]
    softplus = jnp.maximum(z, 0.0) + jnp.log1p(jnp.exp(-jnp.abs(z)))
    g_all = -jnp.exp(par_ref[0:1, :]) * softplus
    beta_all = 1.0 / (1.0 + jnp.exp(-ab))
    lane = lax.broadcasted_iota(jnp.int32, (tm, LANES), 1)
    gb_ref[0] = jnp.where(lane < N_AB, g_all, jnp.where(lane < 2 * N_AB, beta_all, 0.0))


def _projection(x, mod3, row0, row_stride, w_in_p, convw, par, rope, tm):
    bt, t, _ = x.shape
    nt = t // tm
    hpt = tm // HALO
    nh = t // HALO
    use_rope = rope is not None
    in_specs = [pl.BlockSpec((1, tm, D_MODEL), lambda b, i: (b, i, 0)),
                pl.BlockSpec((1, HALO, D_MODEL), lambda b, i: (b, jnp.maximum(i * hpt - 1, 0), 0)),
                pl.BlockSpec((1, HALO, D_MODEL), lambda b, i: (b, jnp.minimum((i + 1) * hpt, nh - 1), 0)),
                pl.BlockSpec((1, 1, 6 * D_MODEL), lambda b, i: (row0 + row_stride * b, 0, 0)),
                _resident((D_MODEL, P_W)),
                _resident((SUBLANES, QKV_W)),
                _resident((SUBLANES, LANES))]
    args = [x, x, x, mod3, w_in_p, convw, par]
    if use_rope:
        in_specs.append(pl.BlockSpec((3, tm, LANES), lambda b, i: (0, i, 0)))
        args.append(rope)
    widths = (QKV_W, QKB_W, D_B, D_A + D_B, LANES)
    dtypes = (F32, F32, BF16, F32, F32)
    return pl.pallas_call(
        functools.partial(_proj_kernel, use_rope, tm, nt),
        grid=(bt, nt),
        in_specs=in_specs,
        out_specs=[pl.BlockSpec((1, tm, w), lambda b, i: (b, i, 0)) for w in widths],
        out_shape=[jax.ShapeDtypeStruct((bt, t, w), dt) for w, dt in zip(widths, dtypes)],
        compiler_params=pltpu.CompilerParams(dimension_semantics=("arbitrary", "arbitrary"),
                                             vmem_limit_bytes=VMEM_LIMIT),
        name="projection",
    )(*args)


POST_SPLIT = 2


def _post_kernel(o_ref, x_ref, mod_ref, wo_ref, ln1w_ref, ln1b_ref, w1_ref, b1_ref, w2_ref, b2_ref,
                 ln2w_ref, ln2b_ref, y_ref):
    m = mod_ref[0]
    g1 = m[:, 2 * D_MODEL:3 * D_MODEL]
    sh2 = m[:, 3 * D_MODEL:4 * D_MODEL]
    sc2 = m[:, 4 * D_MODEL:5 * D_MODEL]
    g2 = m[:, 5 * D_MODEL:6 * D_MODEL]
    rows = o_ref.shape[1] // POST_SPLIT
    parts = [slice(r * rows, (r + 1) * rows) for r in range(POST_SPLIT)]
    ys = [jnp.dot(o_ref[0, sl], wo_ref[...], preferred_element_type=F32) for sl in parts]
    x1s = [_layer_norm(ALPHA * x_ref[0, sl] + g1 * y, 1e-6) * ln1w_ref[...] + ln1b_ref[...]
           for sl, y in zip(parts, ys)]
    hs = [(_layer_norm(x1, 1e-6) * (1.0 + sc2) + sh2).astype(BF16) for x1 in x1s]
    acts = [jnp.square(jnp.maximum(jnp.dot(h, w1_ref[...], preferred_element_type=F32) + b1_ref[...], 0.0))
            .astype(BF16) for h in hs]
    fs = [jnp.dot(a, w2_ref[...], preferred_element_type=F32) + b2_ref[...] for a in acts]
    for sl, x1, f in zip(parts, x1s, fs):
        y_ref[0, sl] = _layer_norm(ALPHA * x1 + g2 * f, 1e-6) * ln2w_ref[...] + ln2b_ref[...]


def _post(o, x, mod3, row0, row_stride, wo, ln1w, ln1b, w1, b1, w2, b2, ln2w, ln2b, tm):
    bt, t, _ = x.shape
    return pl.pallas_call(
        _post_kernel,
        grid=(bt, t // tm),
        in_specs=[pl.BlockSpec((1, tm, D_MODEL), lambda b, i: (b, i, 0)),
                  pl.BlockSpec((1, tm, D_MODEL), lambda b, i: (b, i, 0)),
                  pl.BlockSpec((1, 1, 6 * D_MODEL), lambda b, i: (row0 + row_stride * b, 0, 0)),
                  _resident((D_MODEL, D_MODEL)),
                  _resident((1, D_MODEL)),
                  _resident((1, D_MODEL)),
                  _resident((D_MODEL, D_FF)),
                  _resident((1, D_FF)),
                  _resident((D_FF, D_MODEL)),
                  _resident((1, D_MODEL)),
                  _resident((1, D_MODEL)),
                  _resident((1, D_MODEL))],
        out_specs=pl.BlockSpec((1, tm, D_MODEL), lambda b, i: (b, i, 0)),
        out_shape=jax.ShapeDtypeStruct((bt, t, D_MODEL), F32),
        compiler_params=pltpu.CompilerParams(dimension_semantics=("arbitrary", "arbitrary"),
                                             vmem_limit_bytes=VMEM_LIMIT),
        name="post",
    )(o, x, mod3, wo, ln1w, ln1b, w1, b1, w2, b2, ln2w, ln2b)


_LOG_GAMMA = [math.log1p(-(2.0 ** (-5.0 - h))) for h in range(H_B)]

SOLVE_BASE = 8
SEQ_PER_STEP = 4
CAT_W = H_A * CHUNK_A

def _head_blockdiag(x, mask):
    return jnp.where(mask, jnp.concatenate([x] * H_A, 0), 0.0).astype(BF16)


def _cat_inverses(ms, bd_mask):
    i = lax.broadcasted_iota(jnp.int32, (CHUNK_A, CAT_W), 0)
    j = lax.broadcasted_iota(jnp.int32, (CHUNK_A, CAT_W), 1) % CHUNK_A
    same = lambda s: (i // s) == (j // s)
    eye = jnp.where(i == j, 1.0, 0.0)
    mul = lambda a, bd: jnp.dot(a.astype(BF16), bd, preferred_element_type=F32)
    ps = [jnp.where(same(SOLVE_BASE), -m, 0.0) for m in ms]
    ts = [eye + p for p in ps]
    bds = [_head_blockdiag(p, bd_mask) for p in ps]
    for _ in range(int(math.log2(SOLVE_BASE)) - 1):
        ps = [mul(p, bd) for p, bd in zip(ps, bds)]
        bds = [_head_blockdiag(p, bd_mask) for p in ps]
        ts = [t + mul(t, bd) for t, bd in zip(ts, bds)]
    s = SOLVE_BASE
    while s < CHUNK_A:
        sel = same(2 * s) & jnp.logical_not(same(s))
        bds = [_head_blockdiag(t, bd_mask) for t in ts]
        ys = [mul(jnp.where(sel, m, 0.0), bd) for m, bd in zip(ms, bds)]
        bds = [_head_blockdiag(y, bd_mask) for y in ys]
        ts = [t - mul(t, bd) for t, bd in zip(ts, bds)]
        s *= 2
    return ts


def _mixer_kernel(rev, has_init, emit_state, nblk, *refs):
    NS = SEQ_PER_STEP
    TB = TIME_BLOCK
    d = 1 if rev else 0
    j = pl.program_id(1)
    qkv_ref, qkb_ref, vb_ref, gb_ref = refs[:4]
    pos_ = 4
    if rev:
        of_ref, gates_ref, naw_ref, gnw_ref, gnb_ref = refs[pos_:pos_ + 5]
        pos_ += 5
    if has_init:
        sd0_ref, sr0_ref = refs[pos_:pos_ + 2]
        pos_ += 2
    o_ref = refs[pos_]
    pos_ += 1
    if emit_state:
        sd_ref, sr_ref = refs[pos_:pos_ + 2]
        pos_ += 2
    sa_scr, sr_scr = refs[pos_:pos_ + 2]

    @pl.when(j == 0)
    def _():
        if has_init:
            for bb in range(NS):
                for h in range(H_A):
                    sa_scr[bb, :, h * DV_A:(h + 1) * DV_A] = sd0_ref[bb, 0, 0, h]
                sr_scr[bb] = jnp.zeros(sr_scr.shape[1:], F32)
                for h in range(H_B):
                    lo = (h % 2) * DK_B
                    sr_scr[bb, h // 2, lo:lo + DK_B, (h % 2) * DV_B:(h % 2 + 1) * DV_B] = sr0_ref[bb, 0, 0, h]
        else:
            sa_scr[...] = jnp.zeros(sa_scr.shape, F32)
            sr_scr[...] = jnp.zeros(sr_scr.shape, F32)

    ri = lax.broadcasted_iota(jnp.int32, (TB, TB), 0)
    ci = lax.broadcasted_iota(jnp.int32, (TB, TB), 1)
    cum = jnp.where(((ri // CHUNK_A) == (ci // CHUNK_A)) & ((ci >= ri) if rev else (ci <= ri)),
                    1.0, 0.0).astype(BF16)
    er = lax.broadcasted_iota(jnp.int32, (LANES, CAT_W), 0)
    ec = lax.broadcasted_iota(jnp.int32, (LANES, CAT_W), 1)
    esel_c = jnp.where(er == d * H_A + ec // CHUNK_A, 1.0, 0.0).astype(BF16)
    lane_tb = lax.broadcasted_iota(jnp.int32, (TB, LANES), 1)
    i64 = lax.broadcasted_iota(jnp.int32, (CHUNK_A, CAT_W), 0)
    j64 = lax.broadcasted_iota(jnp.int32, (CHUNK_A, CAT_W), 1) % CHUNK_A
    tri = (i64 <= j64) if rev else (i64 >= j64)
    strict = (i64 < j64) if rev else (i64 > j64)
    bd_mask = (lax.broadcasted_iota(jnp.int32, (CAT_W, CAT_W), 0) // CHUNK_A
               == lax.broadcasted_iota(jnp.int32, (CAT_W, CAT_W), 1) // CHUNK_A)
    bdk_mask = (lax.broadcasted_iota(jnp.int32, (CAT_W, D_A), 0) // CHUNK_A
                == lax.broadcasted_iota(jnp.int32, (CAT_W, D_A), 1) // DK_A)
    lane_head = lax.broadcasted_iota(jnp.int32, (CHUNK_A, CAT_W), 1) // CHUNK_A
    nca = TB // CHUNK_A
    order_a = list(range(nca - 1, -1, -1)) if rev else list(range(nca))

    st = []
    for bb in range(NS):
        gbk = gb_ref[bb]
        gc = None
        for part in _split3(gbk):
            t_ = jnp.dot(cum, part, preferred_element_type=F32)
            gc = t_ if gc is None else gc + t_
        gct = None
        for part in _split3(gbk.T):
            t_ = lax.dot_general(part, cum, (((1,), (1,)), ((), ())), preferred_element_type=F32)
            gct = t_ if gct is None else gct + t_
        gcb = jnp.where(lane_tb < N_AB, gc, gbk)
        gcs = None
        for part in _split3(gcb):
            t_ = jnp.dot(part, esel_c, preferred_element_type=F32)
            gcs = t_ if gcs is None else gcs + t_
        gc5a = jnp.concatenate([jnp.broadcast_to(gcb[:, d * H_A + h:d * H_A + h + 1], (TB, DK_A))
                                for h in range(H_A)], 1)
        b5a = jnp.concatenate([jnp.broadcast_to(gcb[:, N_AB + d * H_A + h:N_AB + d * H_A + h + 1], (TB, DK_A))
                               for h in range(H_A)], 1)
        for n in order_a:
            r0 = n * CHUNK_A
            last = r0 if rev else r0 + CHUNK_A - 1
            it = dict(bb=bb, r0=r0, gc5=gc5a[r0:r0 + CHUNK_A], tot5=gc5a[last:last + 1], b5=b5a[r0:r0 + CHUNK_A])
            q = qkv_ref[bb, r0:r0 + CHUNK_A, 0:D_A]
            k = qkv_ref[bb, r0:r0 + CHUNK_A, D_A:2 * D_A]
            gcc = gcs[r0:r0 + CHUNK_A]
            gcr = jnp.concatenate([gct[d * H_A + h:d * H_A + h + 1, r0:r0 + CHUNK_A] for h in range(H_A)], 1)
            decay = jnp.where(tri, jnp.exp(jnp.where(tri, gcc - gcr, 0.0)), 0.0)
            mq = _dot_nt(jnp.concatenate([k * it["b5"], q], 0), _head_blockdiag(k, bdk_mask))
            it["m"] = jnp.where(strict, mq[:CHUNK_A] * decay, 0.0)
            it["qk"] = jnp.where(tri, mq[CHUNK_A:] * decay, 0.0)
            st.append(it)
    lane = lax.broadcasted_iota(jnp.int32, (CHUNK_B, LANES), 1)
    pi = lax.broadcasted_iota(jnp.int32, (CHUNK_B, CHUNK_B), 0)
    pj = lax.broadcasted_iota(jnp.int32, (CHUNK_B, CHUNK_B), 1)
    pos = lax.broadcasted_iota(jnp.int32, (CHUNK_B, 1), 0)
    if rev:
        pi, pj, pos = CHUNK_B - 1 - pi, CHUNK_B - 1 - pj, CHUNK_B - 1 - pos
    pdiff = (pi - pj).astype(F32)
    posf = pos.astype(F32)
    first_half = (lane // DK_B) == 0
    own_block = (lax.broadcasted_iota(jnp.int32, (LANES, 2 * DV_B), 0) // DK_B
                 == lax.broadcasted_iota(jnp.int32, (LANES, 2 * DV_B), 1) // DV_B)
    ncb = TB // CHUNK_B
    order_b = range(ncb - 1, -1, -1) if rev else range(ncb)
    rt = []
    for pp in range(H_B // 2):
        lgs = [_LOG_GAMMA[H_B - 1 - h] if rev else _LOG_GAMMA[h] for h in (2 * pp, 2 * pp + 1)]
        d_mask = jnp.concatenate([jnp.where(pdiff >= 0, jnp.exp(lg * jnp.maximum(pdiff, 0.0)), 0.0) for lg in lgs], 1)
        xi = jnp.where(first_half, jnp.exp(lgs[0] * (posf + 1.0)), jnp.exp(lgs[1] * (posf + 1.0)))
        zeta = jnp.where(first_half, jnp.exp(lgs[0] * (CHUNK_B - 1.0 - posf)), jnp.exp(lgs[1] * (CHUNK_B - 1.0 - posf)))
        col_head = lax.broadcasted_iota(jnp.int32, (1, 2 * DV_B), 1) // DV_B
        g_chunk = jnp.where(col_head == 0, math.exp(lgs[0] * CHUNK_B), math.exp(lgs[1] * CHUNK_B))
        for bb in range(NS):
            for n in order_b:
                rt.append(dict(bb=bb, r0=n * CHUNK_B, pp=pp, g_chunk=g_chunk, d_mask=d_mask, xi=xi, zeta=zeta))

    def retention_independent(it):
        bb, r0, pp = it["bb"], it["r0"], it["pp"]
        q = qkb_ref[bb, r0:r0 + CHUNK_B, pp * LANES:(pp + 1) * LANES]
        k = qkb_ref[bb, r0:r0 + CHUNK_B, H_B * DK_B + pp * LANES:H_B * DK_B + (pp + 1) * LANES]
        v0 = vb_ref[bb, r0:r0 + CHUNK_B, 2 * pp * DV_B:(2 * pp + 1) * DV_B]
        v1 = vb_ref[bb, r0:r0 + CHUNK_B, (2 * pp + 1) * DV_B:(2 * pp + 2) * DV_B]
        zb = jnp.zeros((CHUNK_B, DV_B), BF16)
        ksplit = jnp.concatenate([jnp.where(first_half, k, 0.0), jnp.where(first_half, 0.0, k)], 0)
        scores = _dot_nt(q, ksplit) * it["d_mask"]
        v_bd = jnp.concatenate([jnp.concatenate([v0, zb], 1), jnp.concatenate([zb, v1], 1)], 0)
        it["inner"] = jnp.dot(scores.astype(BF16), v_bd, preferred_element_type=F32)
        it["qx"] = q * it["xi"]
        ds = lax.dot_general((k * it["zeta"]).astype(BF16), jnp.concatenate([v0, v1], 1),
                             (((0,), (0,)), ((), ())), preferred_element_type=F32)
        it["ds"] = jnp.where(own_block, ds, 0.0)

    fillers = [functools.partial(retention_independent, it) for it in rt]
    inv = _cat_inverses([it["m"] for it in st], bd_mask)
    for it, t in zip(st, inv):
        bb, r0 = it["bb"], it["r0"]
        k = qkv_ref[bb, r0:r0 + CHUNK_A, D_A:2 * D_A]
        v = qkv_ref[bb, r0:r0 + CHUNK_A, 2 * D_A:3 * D_A]
        vb5 = v * it["b5"]
        kbe5 = (k * it["b5"]) * jnp.exp(it["gc5"])
        rhs = jnp.concatenate(
            [jnp.concatenate([vb5[:, h * DV_A:(h + 1) * DV_A], kbe5[:, h * DK_A:(h + 1) * DK_A]], 1)
             for h in range(H_A)], 0).astype(BF16)
        lhs = jnp.concatenate([jnp.where(lane_head == h, t, 0.0) for h in range(H_A)], 0)
        sol = jnp.dot(lhs.astype(BF16), rhs, preferred_element_type=F32)
        it["u"] = jnp.concatenate([sol[h * CHUNK_A:(h + 1) * CHUNK_A, :DV_A] for h in range(H_A)], 1)
        it["w"] = jnp.concatenate([sol[h * CHUNK_A:(h + 1) * CHUNK_A, DV_A:] for h in range(H_A)], 1)

    def finish(bb, r0, rows, c0, width, o):
        if not rev:
            o_ref[bb, r0:r0 + rows, c0:c0 + width] = o
            return
        o = o + of_ref[bb, r0:r0 + rows, c0:c0 + width]
        outs = []
        for hh in range(width // LANES):
            x = o[:, hh * LANES:(hh + 1) * LANES]
            if c0 < D_A:
                x = x * lax.rsqrt(jnp.mean(x * x, -1, keepdims=True) + 1e-6) * naw_ref[...]
            else:
                cb = c0 - D_A + hh * LANES
                x = _layer_norm(x, 1e-5) * gnw_ref[:, cb:cb + LANES] + gnb_ref[:, cb:cb + LANES]
            outs.append(x)
        o = outs[0] if len(outs) == 1 else jnp.concatenate(outs, 1)
        o_ref[bb, r0:r0 + rows, c0:c0 + width] = (o * gates_ref[bb, r0:r0 + rows, c0:c0 + width]).astype(BF16)

    per_step = -(-len(fillers) // nca)
    for c in range(nca):
        grp_items = [st[bb * nca + c] for bb in range(NS)]
        s_old = [sa_scr[it["bb"]] for it in grp_items]
        prods = []
        for it, s in zip(grp_items, s_old):
            q_dec = qkv_ref[it["bb"], it["r0"]:it["r0"] + CHUNK_A, 0:D_A] * jnp.exp(it["gc5"])
            pr = []
            for pp in range(H_A // 2):
                lo = pp * 2 * DV_A
                z = jnp.zeros((DK_A, DV_A), F32)
                bds = jnp.concatenate([jnp.concatenate([s[:, lo:lo + DV_A], z], 1),
                                       jnp.concatenate([z, s[:, lo + DV_A:lo + 2 * DV_A]], 1)], 0)
                lhs = jnp.concatenate([it["w"][:, lo:lo + 2 * DK_A], q_dec[:, lo:lo + 2 * DK_A]], 0)
                pr.append(_dot(lhs, bds))
            prods.append(jnp.concatenate(pr, 1))
        for fill in fillers[c * per_step:(c + 1) * per_step]:
            fill()
        v_new = [it["u"] - pr[:CHUNK_A] for it, pr in zip(grp_items, prods)]
        vbd = [_head_blockdiag(v, bdk_mask) for v in v_new]
        upd = []
        for it, vb_ in zip(grp_items, vbd):
            k_dec = (qkv_ref[it["bb"], it["r0"]:it["r0"] + CHUNK_A, D_A:2 * D_A]
                     * jnp.exp(it["tot5"] - it["gc5"]))
            kd_stack = jnp.concatenate([k_dec[:, h * DK_A:(h + 1) * DK_A] for h in range(H_A)], 0)
            upd.append(lax.dot_general(kd_stack.astype(BF16), vb_, (((0,), (0,)), ((), ())),
                                       preferred_element_type=F32))
        for it, s, x in zip(grp_items, s_old, upd):
            sa_scr[it["bb"]] = s * jnp.exp(it["tot5"]) + x
        for it, pr, vb_ in zip(grp_items, prods, vbd):
            o = pr[CHUNK_A:] + jnp.dot(it["qk"].astype(BF16), vb_, preferred_element_type=F32)
            finish(it["bb"], it["r0"], CHUNK_A, 0, D_A, o)

    for it in rt:
        bb, pp = it["bb"], it["pp"]
        s = sr_scr[bb, pp]
        o = it["inner"] + _dot(it["qx"], s)
        sr_scr[bb, pp] = s * it["g_chunk"] + it["ds"]
        finish(bb, it["r0"], CHUNK_B, D_A + 2 * pp * DV_B, 2 * DV_B, o)

    if emit_state:
        @pl.when(j == nblk - 1)
        def _():
            for bb in range(NS):
                for h in range(H_A):
                    sd_ref[bb, 0, 0, h] = sa_scr[bb, :, h * DV_A:(h + 1) * DV_A]
                for h in range(H_B):
                    lo = (h % 2) * DK_B
                    sr_ref[bb, 0, 0, h] = sr_scr[bb, h // 2, lo:lo + DK_B, (h % 2) * DV_B:(h % 2 + 1) * DV_B]


def _mixer(rev, qkv, qkb, vb, gb, o_fwd, gates, naw, gnw, gnb, sd0, sr0, emit_state):
    bt, t, _ = qkv.shape
    NS = SEQ_PER_STEP
    TB = TIME_BLOCK
    nblk = t // TB
    has_init = sd0 is not None
    d = 1 if rev else 0
    tb_of = (lambda j: nblk - 1 - j) if rev else (lambda j: j)
    tok = lambda w: pl.BlockSpec((NS, TB, w), lambda b, j: (b, tb_of(j), 0))
    in_specs = [tok(QKV_W), tok(QKB_W), tok(D_B), tok(LANES)]
    args = [qkv, qkb, vb, gb]
    if rev:
        in_specs += [tok(D_A + D_B), tok(D_A + D_B), _resident((1, DV_A)), _resident((1, D_B)), _resident((1, D_B))]
        args += [o_fwd, gates, naw, gnw, gnb]
    if has_init:
        in_specs.append(pl.BlockSpec((NS, 1, 1, H_A, DK_A, DV_A), lambda b, j: (b, 0, d, 0, 0, 0)))
        in_specs.append(pl.BlockSpec((NS, 1, 1, H_B, DK_B, DV_B), lambda b, j: (b, 0, d, 0, 0, 0)))
        args += [sd0, sr0]
    out_specs = [tok(D_A + D_B)]
    out_shape = [jax.ShapeDtypeStruct((bt, t, D_A + D_B), BF16 if rev else F32)]
    if emit_state:
        out_specs.append(pl.BlockSpec((NS, 1, 1, H_A, DK_A, DV_A), lambda b, j: (b, 0, 0, 0, 0, 0)))
        out_specs.append(pl.BlockSpec((NS, 1, 1, H_B, DK_B, DV_B), lambda b, j: (b, 0, 0, 0, 0, 0)))
        out_shape.append(jax.ShapeDtypeStruct((bt, DEPTH, 1, H_A, DK_A, DV_A), F32))
        out_shape.append(jax.ShapeDtypeStruct((bt, DEPTH, 1, H_B, DK_B, DV_B), F32))
    scratch = [pltpu.VMEM((NS, DK_A, H_A * DV_A), F32),
               pltpu.VMEM((NS, H_B // 2, LANES, 2 * DV_B), F32)]
    return pl.pallas_call(
        functools.partial(_mixer_kernel, rev, has_init, emit_state, nblk),
        grid=(bt // NS, nblk),
        in_specs=in_specs,
        out_specs=out_specs,
        out_shape=out_shape,
        scratch_shapes=scratch,
        compiler_params=pltpu.CompilerParams(dimension_semantics=("arbitrary", "arbitrary"),
                                             vmem_limit_bytes=VMEM_LIMIT),
        name="mixer_bwd" if rev else "mixer_fwd",
    )(*args)


def _rope_tables(t):
    rows = t // GRID_W
    r = np.repeat(np.arange(rows, dtype=np.float32), GRID_W)
    col = np.tile(np.arange(GRID_W, dtype=np.float32), rows)
    nf = DK_B // 4
    inv = (np.float32(ROPE_BASE) ** (-np.arange(nf, dtype=np.float32) / np.float32(nf))).astype(np.float32)
    ang = jnp.asarray(np.concatenate([r[:, None] * inv, col[:, None] * inv], -1).astype(np.float32))
    cos, sin = jnp.cos(ang), jnp.sin(ang)
    zero = jnp.zeros_like(sin)
    cos_t = jnp.tile(cos, (1, 4))
    sin_up = jnp.tile(jnp.concatenate([-sin, zero], -1), (1, 2))
    sin_dn = jnp.tile(jnp.concatenate([zero, sin], -1), (1, 2))
    return jnp.stack([cos_t, sin_up, sin_dn], 0)


def kernel(x_prompt, x_sample, c, state_delta, state_ret, c_ctx, w_mod, b_mod, w_in, conv_w, a_log, dt_bias,
           norm_a_w, gn_w, gn_b, w_o, ln1_w, ln1_b, w_ff1, b_ff1, w_ff2, b_ff2, ln2_w, ln2_b):
    assert w_mod.shape[0] == DEPTH == 1
    n_dec = c.shape[0]
    rows = -(-(1 + n_dec) // SUBLANES) * SUBLANES
    cond = jnp.zeros((rows, D_MODEL), F32).at[0].set(c_ctx).at[1:1 + n_dec].set(c)
    mod = _modulation(cond, w_mod[0], b_mod[0])
    mod3 = mod.reshape(rows, 1, 6 * D_MODEL)

    w = w_in[0]
    c_ab = QKV_W + D_A
    w_in_p = (jnp.zeros((D_MODEL, P_W), BF16)
              .at[:, :c_ab].set(w[:, :c_ab].astype(BF16))
              .at[:, c_ab:COL_AB].set(w[:, c_ab + 2 * N_AB:].astype(BF16))
              .at[:, COL_AB:COL_AB + 2 * N_AB].set(w[:, c_ab:c_ab + 2 * N_AB].astype(BF16)))
    convw = jnp.zeros((SUBLANES, QKV_W), F32).at[:CONV_K].set(conv_w[0])
    par = (jnp.zeros((SUBLANES, LANES), F32).at[0, :N_AB].set(a_log[0].reshape(-1))
           .at[1, :N_AB].set(dt_bias[0].reshape(-1)))
    naw = norm_a_w[0].reshape(1, DV_A)
    gnw = gn_w[0].reshape(1, D_B)
    gnb = gn_b[0].reshape(1, D_B)
    wo = w_o[0].astype(BF16)
    w1 = w_ff1[0].astype(BF16)
    w2 = w_ff2[0].astype(BF16)
    row = lambda v: v[0].reshape(1, -1)

    def trunk(x, row0, row_stride, rope, sd0, sr0, emit_state, tm):
        qkv, qkb, vb, gates, gb = _projection(x, mod3, row0, row_stride, w_in_p, convw, par, rope, tm)
        fwd = _mixer(False, qkv, qkb, vb, gb, None, None, None, None, None, sd0, sr0, emit_state)
        bwd = _mixer(True, qkv, qkb, vb, gb, fwd[0], gates, naw, gnw, gnb, sd0, sr0, emit_state)
        y = _post(bwd[0], x, mod3, row0, row_stride, wo, row(ln1_w), row(ln1_b), w1, row(b_ff1), w2, row(b_ff2),
                  row(ln2_w), row(ln2_b), tm)
        states = [jnp.concatenate([f, b], 2) for f, b in zip(fwd[1:], bwd[1:])]
        return y, states

    y_prompt, (new_sd, new_sr) = trunk(x_prompt, 0, 0, None, None, None, True, 256)
    y_sample, _ = trunk(x_sample, 1, 1, _rope_tables(x_sample.shape[1]), state_delta, state_ret, False, 512)
    return y_prompt, y_sample, new_sd, new_sr
```

```python
import functools
import math

import jax
import jax.numpy as jnp
import numpy as np
from jax import lax
from jax.experimental import pallas as pl
from jax.experimental.pallas import tpu as pltpu

F32 = jnp.float32
BF16 = jnp.bfloat16

D_MODEL = 1024
H_A, DK_A, DV_A = 4, 128, 128
D_A = H_A * DV_A
CONV_K = 5
CHUNK_A = 64
H_B, DK_B, DV_B = 4, 64, 128
D_B = H_B * DV_B
CHUNK_B = 128
GRID_W = 64
ROPE_BASE = 10000.0
D_FF = 4 * D_MODEL
DEPTH = 1
ALPHA = (2.0 * DEPTH) ** 0.25

LANES = 128
SUBLANES = 8
VMEM_LIMIT = 56 * 1024 * 1024

QKV_W = 3 * D_A
COL_GATE_A = QKV_W
COL_QK_B = COL_GATE_A + D_A
QKB_W = 2 * H_B * DK_B
COL_V_B = COL_QK_B + QKB_W
COL_GATE_B = COL_V_B + D_B
COL_AB = COL_GATE_B + D_B
P_W = COL_AB + LANES
N_AB = 2 * H_A

TIME_BLOCK = 256
HALO = SUBLANES
PROJ_GROUP = 256


def _dot(a, b):
    return jnp.dot(a.astype(BF16), b.astype(BF16), preferred_element_type=F32)


def _dot_nt(a, b):
    return lax.dot_general(a.astype(BF16), b.astype(BF16), (((1,), (1,)), ((), ())),
                           preferred_element_type=F32)


def _dot_tn(a, b):
    return lax.dot_general(a.astype(BF16), b.astype(BF16), (((0,), (0,)), ((), ())),
                           preferred_element_type=F32)


def _split3(x):
    hi = x.astype(BF16)
    r = x - hi.astype(F32)
    mid = r.astype(BF16)
    lo = (r - mid.astype(F32)).astype(BF16)
    return hi, mid, lo


def _silu(x):
    h = 0.5 * x
    return h + h * jnp.tanh(h)


def _layer_norm(x, eps):
    mu = jnp.mean(x, -1, keepdims=True)
    xc = x - mu
    var = jnp.mean(xc * xc, -1, keepdims=True)
    return xc * lax.rsqrt(var + eps)


def _resident(shape):
    return pl.BlockSpec(shape, lambda *_: (0,) * len(shape), pipeline_mode=pl.Buffered(1))


def _mod_kernel(c_ref, w_ref, b_ref, o_ref):
    o_ref[...] = _dot(_silu(c_ref[...]), w_ref[...]) + b_ref[...]


def _modulation(cond, w_mod, b_mod):
    rows = cond.shape[0]
    n = w_mod.shape[1]
    bn = 1536
    return pl.pallas_call(
        _mod_kernel,
        grid=(n // bn,),
        in_specs=[pl.BlockSpec((rows, D_MODEL), lambda j: (0, 0)),
                  pl.BlockSpec((D_MODEL, bn), lambda j: (0, j)),
                  pl.BlockSpec((1, bn), lambda j: (0, j))],
        out_specs=pl.BlockSpec((rows, bn), lambda j: (0, j)),
        out_shape=jax.ShapeDtypeStruct((rows, n), F32),
        compiler_params=pltpu.CompilerParams(dimension_semantics=("arbitrary",)),
        name="modulation",
    )(cond, w_mod, b_mod.reshape(1, n))


def _proj_kernel(use_rope, tm, nt, *refs):
    if use_rope:
        (x_ref, xp_ref, xn_ref, mod_ref, w_ref, convw_ref, par_ref, rope_ref,
         qkv_ref, qkb_ref, vb_ref, gates_ref, gb_ref) = refs
    else:
        (x_ref, xp_ref, xn_ref, mod_ref, w_ref, convw_ref, par_ref,
         qkv_ref, qkb_ref, vb_ref, gates_ref, gb_ref) = refs
    i = pl.program_id(1)
    m = mod_ref[0]
    sh1 = m[:, 0:D_MODEL]
    sc1 = m[:, D_MODEL:2 * D_MODEL]

    def modulated(x):
        return _layer_norm(x, 1e-6) * (1.0 + sc1) + sh1

    hb = jnp.concatenate([modulated(xp_ref[0]), modulated(x_ref[0]), modulated(xn_ref[0])], 0).astype(BF16)
    rows = tm + 2 * HALO

    def project(c0, width):
        return jnp.dot(hb, w_ref[:, c0:c0 + width], preferred_element_type=F32)

    def conv_silu(pc, c0):
        xe = jnp.concatenate([jnp.where(i > 0, pc[0:HALO], 0.0), pc[HALO:HALO + tm],
                              jnp.where(i < nt - 1, pc[HALO + tm:], 0.0)], 0)
        z = [xe * convw_ref[kk:kk + 1, c0:c0 + pc.shape[1]] for kk in range(CONV_K)]
        back = pltpu.roll(z[1] + pltpu.roll(z[0], 1, 0), 1, 0)
        ahead = pltpu.roll(z[3] + pltpu.roll(z[4], rows - 1, 0), rows - 1, 0)
        return _silu((z[2] + back + ahead)[HALO:HALO + tm])

    def l2n(x):
        return x * lax.rsqrt(jnp.sum(x * x, -1, keepdims=True) + 1e-6)

    def finish_qkv(pc, c0):
        y = conv_silu(pc, c0)
        for hh in range(pc.shape[1] // DK_A):
            c = c0 + hh * DK_A
            x = y[:, hh * DK_A:(hh + 1) * DK_A]
            if c < D_A:
                x = l2n(x) * (DK_A ** -0.5)
            elif c < 2 * D_A:
                x = l2n(x)
            qkv_ref[0, :, c:c + DK_A] = x

    def finish_rest(pc, c0):
        pc = pc[HALO:HALO + tm]
        w = pc.shape[1]
        if c0 < COL_QK_B:
            gates_ref[0, :, c0 - COL_GATE_A:c0 - COL_GATE_A + w] = _silu(pc)
        elif c0 < COL_V_B:
            for hh in range(w // LANES):
                x = pc[:, hh * LANES:(hh + 1) * LANES]
                if use_rope:
                    x = (x * rope_ref[0] + pltpu.roll(x, LANES - DK_B // 2, 1) * rope_ref[1]
                         + pltpu.roll(x, DK_B // 2, 1) * rope_ref[2])
                c = c0 - COL_QK_B + hh * LANES
                if c < H_B * DK_B:
                    x = x * (DK_B ** -0.5)
                qkb_ref[0, :, c:c + LANES] = x
        elif c0 < COL_GATE_B:
            vb_ref[0, :, c0 - COL_V_B:c0 - COL_V_B + w] = pc.astype(BF16)
        elif c0 < COL_AB:
            gates_ref[0, :, D_A + c0 - COL_GATE_B:D_A + c0 - COL_GATE_B + w] = _silu(pc)
        else:
            z = pc + par_ref[1:2, :]
            softplus = jnp.maximum(z, 0.0) + jnp.log1p(jnp.exp(-jnp.abs(z)))
            g_all = -jnp.exp(par_ref[0:1, :]) * softplus
            beta_all = 1.0 / (1.0 + jnp.exp(-pc))
            lane = lax.broadcasted_iota(jnp.int32, (tm, LANES), 1)
            gb_ref[0] = jnp.where(lane < N_AB, g_all, jnp.where(lane < 2 * N_AB, beta_all, 0.0))

    heavy = [(finish_qkv, c) for c in range(0, QKV_W, PROJ_GROUP)]
    light = [(finish_rest, c) for c in range(QKV_W, P_W, PROJ_GROUP)]
    order = []
    while heavy or light:
        if heavy:
            order.append(heavy.pop(0))
        if light:
            order.append(light.pop(0))
    previous = None
    for fn, c0 in order:
        pc = project(c0, min(PROJ_GROUP, P_W - c0))
        if previous is not None:
            previous[0](previous[1], previous[2])
        previous = (fn, pc, c0)
    previous[0](previous[1], previous[2])


def _projection(x, mod3, row0, row_stride, w_in_p, convw, par, rope, tm):
    bt, t, _ = x.shape
    nt = t // tm
    hpt = tm // HALO
    nh = t // HALO
    use_rope = rope is not None
    in_specs = [pl.BlockSpec((1, tm, D_MODEL), lambda b, i: (b, i, 0)),
                pl.BlockSpec((1, HALO, D_MODEL), lambda b, i: (b, jnp.maximum(i * hpt - 1, 0), 0)),
                pl.BlockSpec((1, HALO, D_MODEL), lambda b, i: (b, jnp.minimum((i + 1) * hpt, nh - 1), 0)),
                pl.BlockSpec((1, 1, 6 * D_MODEL), lambda b, i: (row0 + row_stride * b, 0, 0)),
                _resident((D_MODEL, P_W)),
                _resident((SUBLANES, QKV_W)),
                _resident((SUBLANES, LANES))]
    args = [x, x, x, mod3, w_in_p, convw, par]
    if use_rope:
        in_specs.append(pl.BlockSpec((3, tm, LANES), lambda b, i: (0, i, 0)))
        args.append(rope)
    widths = (QKV_W, QKB_W, D_B, D_A + D_B, LANES)
    dtypes = (F32, F32, BF16, F32, F32)
    return pl.pallas_call(
        functools.partial(_proj_kernel, use_rope, tm, nt),
        grid=(bt, nt),
        in_specs=in_specs,
        out_specs=[pl.BlockSpec((1, tm, w), lambda b, i: (b, i, 0)) for w in widths],
        out_shape=[jax.ShapeDtypeStruct((bt, t, w), dt) for w, dt in zip(widths, dtypes)],
        compiler_params=pltpu.CompilerParams(dimension_semantics=("arbitrary", "arbitrary"),
                                             vmem_limit_bytes=VMEM_LIMIT),
        name="projection",
    )(*args)


POST_SPLIT = 2
POST_TILE = 512


def _post_kernel(o_ref, x_ref, mod_ref, wo_ref, ln1w_ref, ln1b_ref, w1_ref, b1_ref, w2_ref, b2_ref,
                 ln2w_ref, ln2b_ref, y_ref):
    m = mod_ref[0]
    g1 = m[:, 2 * D_MODEL:3 * D_MODEL]
    sh2 = m[:, 3 * D_MODEL:4 * D_MODEL]
    sc2 = m[:, 4 * D_MODEL:5 * D_MODEL]
    g2 = m[:, 5 * D_MODEL:6 * D_MODEL]
    rows = o_ref.shape[1] // POST_SPLIT
    parts = [slice(r * rows, (r + 1) * rows) for r in range(POST_SPLIT)]
    ys = [jnp.dot(o_ref[0, sl], wo_ref[...], preferred_element_type=F32) for sl in parts]
    x1s = [_layer_norm(ALPHA * x_ref[0, sl] + g1 * y, 1e-6) * ln1w_ref[...] + ln1b_ref[...]
           for sl, y in zip(parts, ys)]
    hs = [(_layer_norm(x1, 1e-6) * (1.0 + sc2) + sh2).astype(BF16) for x1 in x1s]
    acts = [jnp.square(jnp.maximum(jnp.dot(h, w1_ref[...], preferred_element_type=F32) + b1_ref[...], 0.0))
            .astype(BF16) for h in hs]
    fs = [jnp.dot(a, w2_ref[...], preferred_element_type=F32) + b2_ref[...] for a in acts]
    for sl, x1, f in zip(parts, x1s, fs):
        y_ref[0, sl] = _layer_norm(ALPHA * x1 + g2 * f, 1e-6) * ln2w_ref[...] + ln2b_ref[...]


def _post(o, x, mod3, row0, row_stride, wo, ln1w, ln1b, w1, b1, w2, b2, ln2w, ln2b, tm):
    bt, t, _ = x.shape
    return pl.pallas_call(
        _post_kernel,
        grid=(bt, t // tm),
        in_specs=[pl.BlockSpec((1, tm, D_MODEL), lambda b, i: (b, i, 0)),
                  pl.BlockSpec((1, tm, D_MODEL), lambda b, i: (b, i, 0)),
                  pl.BlockSpec((1, 1, 6 * D_MODEL), lambda b, i: (row0 + row_stride * b, 0, 0)),
                  _resident((D_MODEL, D_MODEL)),
                  _resident((1, D_MODEL)),
                  _resident((1, D_MODEL)),
                  _resident((D_MODEL, D_FF)),
                  _resident((1, D_FF)),
                  _resident((D_FF, D_MODEL)),
                  _resident((1, D_MODEL)),
                  _resident((1, D_MODEL)),
                  _resident((1, D_MODEL))],
        out_specs=pl.BlockSpec((1, tm, D_MODEL), lambda b, i: (b, i, 0)),
        out_shape=jax.ShapeDtypeStruct((bt, t, D_MODEL), F32),
        compiler_params=pltpu.CompilerParams(dimension_semantics=("arbitrary", "arbitrary"),
                                             vmem_limit_bytes=VMEM_LIMIT),
        name="post",
    )(o, x, mod3, wo, ln1w, ln1b, w1, b1, w2, b2, ln2w, ln2b)


_LOG_GAMMA = [math.log1p(-(2.0 ** (-5.0 - h))) for h in range(H_B)]

SOLVE_BASE = 8
SEQ_PER_STEP = 4
CAT_W = H_A * CHUNK_A

def _head_blockdiag(x, mask):
    return jnp.where(mask, jnp.concatenate([x] * H_A, 0), 0.0).astype(BF16)


def _cat_inverses(ms, bd_mask):
    i = lax.broadcasted_iota(jnp.int32, (CHUNK_A, CAT_W), 0)
    j = lax.broadcasted_iota(jnp.int32, (CHUNK_A, CAT_W), 1) % CHUNK_A
    same = lambda s: (i // s) == (j // s)
    eye = jnp.where(i == j, 1.0, 0.0)
    mul = lambda a, bd: jnp.dot(a.astype(BF16), bd, preferred_element_type=F32)
    ps = [jnp.where(same(SOLVE_BASE), -m, 0.0) for m in ms]
    ts = [eye + p for p in ps]
    bds = [_head_blockdiag(p, bd_mask) for p in ps]
    for _ in range(int(math.log2(SOLVE_BASE)) - 1):
        ps = [mul(p, bd) for p, bd in zip(ps, bds)]
        bds = [_head_blockdiag(p, bd_mask) for p in ps]
        ts = [t + mul(t, bd) for t, bd in zip(ts, bds)]
    s = SOLVE_BASE
    while s < CHUNK_A:
        sel = same(2 * s) & jnp.logical_not(same(s))
        bds = [_head_blockdiag(t, bd_mask) for t in ts]
        ys = [mul(jnp.where(sel, m, 0.0), bd) for m, bd in zip(ms, bds)]
        bds = [_head_blockdiag(y, bd_mask) for y in ys]
        ts = [t - mul(t, bd) for t, bd in zip(ts, bds)]
        s *= 2
    return ts


def _mixer_kernel(rev, has_init, emit_state, nblk, *refs):
    NS = SEQ_PER_STEP
    TB = TIME_BLOCK
    d = 1 if rev else 0
    j = pl.program_id(1)
    qkv_ref, qkb_ref, vb_ref, gb_ref = refs[:4]
    pos_ = 4
    if rev:
        of_ref, gates_ref, naw_ref, gnw_ref, gnb_ref = refs[pos_:pos_ + 5]
        pos_ += 5
    if has_init:
        sd0_ref, sr0_ref = refs[pos_:pos_ + 2]
        pos_ += 2
    o_ref = refs[pos_]
    pos_ += 1
    if emit_state:
        sd_ref, sr_ref = refs[pos_:pos_ + 2]
        pos_ += 2
    sa_scr, sr_scr = refs[pos_:pos_ + 2]

    @pl.when(j == 0)
    def _():
        if has_init:
            for bb in range(NS):
                for h in range(H_A):
                    sa_scr[bb, :, h * DV_A:(h + 1) * DV_A] = sd0_ref[bb, 0, 0, h]
                sr_scr[bb] = jnp.zeros(sr_scr.shape[1:], F32)
                for h in range(H_B):
                    lo = (h % 2) * DK_B
                    sr_scr[bb, h // 2, lo:lo + DK_B, (h % 2) * DV_B:(h % 2 + 1) * DV_B] = sr0_ref[bb, 0, 0, h]
        else:
            sa_scr[...] = jnp.zeros(sa_scr.shape, F32)
            sr_scr[...] = jnp.zeros(sr_scr.shape, F32)

    ri = lax.broadcasted_iota(jnp.int32, (TB, TB), 0)
    ci = lax.broadcasted_iota(jnp.int32, (TB, TB), 1)
    cum = jnp.where(((ri // CHUNK_A) == (ci // CHUNK_A)) & ((ci >= ri) if rev else (ci <= ri)),
                    1.0, 0.0).astype(BF16)
    er = lax.broadcasted_iota(jnp.int32, (LANES, CAT_W), 0)
    ec = lax.broadcasted_iota(jnp.int32, (LANES, CAT_W), 1)
    esel_c = jnp.where(er == d * H_A + ec // CHUNK_A, 1.0, 0.0).astype(BF16)
    lane_tb = lax.broadcasted_iota(jnp.int32, (TB, LANES), 1)
    i64 = lax.broadcasted_iota(jnp.int32, (CHUNK_A, CAT_W), 0)
    j64 = lax.broadcasted_iota(jnp.int32, (CHUNK_A, CAT_W), 1) % CHUNK_A
    tri = (i64 <= j64) if rev else (i64 >= j64)
    strict = (i64 < j64) if rev else (i64 > j64)
    bd_mask = (lax.broadcasted_iota(jnp.int32, (CAT_W, CAT_W), 0) // CHUNK_A
               == lax.broadcasted_iota(jnp.int32, (CAT_W, CAT_W), 1) // CHUNK_A)
    bdk_mask = (lax.broadcasted_iota(jnp.int32, (CAT_W, D_A), 0) // CHUNK_A
                == lax.broadcasted_iota(jnp.int32, (CAT_W, D_A), 1) // DK_A)
    lane_head = lax.broadcasted_iota(jnp.int32, (CHUNK_A, CAT_W), 1) // CHUNK_A
    nca = TB // CHUNK_A
    order_a = list(range(nca - 1, -1, -1)) if rev else list(range(nca))

    st = []
    for bb in range(NS):
        gbk = gb_ref[bb]
        gc = None
        for part in _split3(gbk):
            t_ = jnp.dot(cum, part, preferred_element_type=F32)
            gc = t_ if gc is None else gc + t_
        gct = None
        for part in _split3(gbk.T):
            t_ = lax.dot_general(part, cum, (((1,), (1,)), ((), ())), preferred_element_type=F32)
            gct = t_ if gct is None else gct + t_
        gcb = jnp.where(lane_tb < N_AB, gc, gbk)
        gcs = None
        for part in _split3(gcb):
            t_ = jnp.dot(part, esel_c, preferred_element_type=F32)
            gcs = t_ if gcs is None else gcs + t_
        gc5a = jnp.concatenate([jnp.broadcast_to(gcb[:, d * H_A + h:d * H_A + h + 1], (TB, DK_A))
                                for h in range(H_A)], 1)
        b5a = jnp.concatenate([jnp.broadcast_to(gcb[:, N_AB + d * H_A + h:N_AB + d * H_A + h + 1], (TB, DK_A))
                               for h in range(H_A)], 1)
        for n in order_a:
            r0 = n * CHUNK_A
            last = r0 if rev else r0 + CHUNK_A - 1
            it = dict(bb=bb, r0=r0, gc5=gc5a[r0:r0 + CHUNK_A], tot5=gc5a[last:last + 1], b5=b5a[r0:r0 + CHUNK_A])
            q = qkv_ref[bb, r0:r0 + CHUNK_A, 0:D_A]
            k = qkv_ref[bb, r0:r0 + CHUNK_A, D_A:2 * D_A]
            gcc = gcs[r0:r0 + CHUNK_A]
            gcr = jnp.concatenate([gct[d * H_A + h:d * H_A + h + 1, r0:r0 + CHUNK_A] for h in range(H_A)], 1)
            decay = jnp.where(tri, jnp.exp(jnp.where(tri, gcc - gcr, 0.0)), 0.0)
            mq = _dot_nt(jnp.concatenate([k * it["b5"], q], 0), _head_blockdiag(k, bdk_mask))
            it["m"] = jnp.where(strict, mq[:CHUNK_A] * decay, 0.0)
            it["qk"] = jnp.where(tri, mq[CHUNK_A:] * decay, 0.0)
            st.append(it)
    lane = lax.broadcasted_iota(jnp.int32, (CHUNK_B, LANES), 1)
    pi = lax.broadcasted_iota(jnp.int32, (CHUNK_B, CHUNK_B), 0)
    pj = lax.broadcasted_iota(jnp.int32, (CHUNK_B, CHUNK_B), 1)
    pos = lax.broadcasted_iota(jnp.int32, (CHUNK_B, 1), 0)
    if rev:
        pi, pj, pos = CHUNK_B - 1 - pi, CHUNK_B - 1 - pj, CHUNK_B - 1 - pos
    pdiff = (pi - pj).astype(F32)
    posf = pos.astype(F32)
    first_half = (lane // DK_B) == 0
    own_block = (lax.broadcasted_iota(jnp.int32, (LANES, 2 * DV_B), 0) // DK_B
                 == lax.broadcasted_iota(jnp.int32, (LANES, 2 * DV_B), 1) // DV_B)
    ncb = TB // CHUNK_B
    order_b = range(ncb - 1, -1, -1) if rev else range(ncb)
    rt = []
    for pp in range(H_B // 2):
        lgs = [_LOG_GAMMA[H_B - 1 - h] if rev else _LOG_GAMMA[h] for h in (2 * pp, 2 * pp + 1)]
        d_mask = jnp.concatenate([jnp.where(pdiff >= 0, jnp.exp(lg * jnp.maximum(pdiff, 0.0)), 0.0) for lg in lgs], 1)
        xi = jnp.where(first_half, jnp.exp(lgs[0] * (posf + 1.0)), jnp.exp(lgs[1] * (posf + 1.0)))
        zeta = jnp.where(first_half, jnp.exp(lgs[0] * (CHUNK_B - 1.0 - posf)), jnp.exp(lgs[1] * (CHUNK_B - 1.0 - posf)))
        col_head = lax.broadcasted_iota(jnp.int32, (1, 2 * DV_B), 1) // DV_B
        g_chunk = jnp.where(col_head == 0, math.exp(lgs[0] * CHUNK_B), math.exp(lgs[1] * CHUNK_B))
        for bb in range(NS):
            for n in order_b:
                rt.append(dict(bb=bb, r0=n * CHUNK_B, pp=pp, g_chunk=g_chunk, d_mask=d_mask, xi=xi, zeta=zeta))

    def retention_independent(it):
        bb, r0, pp = it["bb"], it["r0"], it["pp"]
        q = qkb_ref[bb, r0:r0 + CHUNK_B, pp * LANES:(pp + 1) * LANES]
        k = qkb_ref[bb, r0:r0 + CHUNK_B, H_B * DK_B + pp * LANES:H_B * DK_B + (pp + 1) * LANES]
        v0 = vb_ref[bb, r0:r0 + CHUNK_B, 2 * pp * DV_B:(2 * pp + 1) * DV_B]
        v1 = vb_ref[bb, r0:r0 + CHUNK_B, (2 * pp + 1) * DV_B:(2 * pp + 2) * DV_B]
        zb = jnp.zeros((CHUNK_B, DV_B), BF16)
        ksplit = jnp.concatenate([jnp.where(first_half, k, 0.0), jnp.where(first_half, 0.0, k)], 0)
        scores = _dot_nt(q, ksplit) * it["d_mask"]
        v_bd = jnp.concatenate([jnp.concatenate([v0, zb], 1), jnp.concatenate([zb, v1], 1)], 0)
        it["inner"] = jnp.dot(scores.astype(BF16), v_bd, preferred_element_type=F32)
        it["qx"] = q * it["xi"]
        ds = lax.dot_general((k * it["zeta"]).astype(BF16), jnp.concatenate([v0, v1], 1),
                             (((0,), (0,)), ((), ())), preferred_element_type=F32)
        it["ds"] = jnp.where(own_block, ds, 0.0)

    fillers = [functools.partial(retention_independent, it) for it in rt]
    inv = _cat_inverses([it["m"] for it in st], bd_mask)
    for it, t in zip(st, inv):
        bb, r0 = it["bb"], it["r0"]
        k = qkv_ref[bb, r0:r0 + CHUNK_A, D_A:2 * D_A]
        v = qkv_ref[bb, r0:r0 + CHUNK_A, 2 * D_A:3 * D_A]
        vb5 = v * it["b5"]
        kbe5 = (k * it["b5"]) * jnp.exp(it["gc5"])
        rhs = jnp.concatenate(
            [jnp.concatenate([vb5[:, h * DV_A:(h + 1) * DV_A], kbe5[:, h * DK_A:(h + 1) * DK_A]], 1)
             for h in range(H_A)], 0).astype(BF16)
        lhs = jnp.concatenate([jnp.where(lane_head == h, t, 0.0) for h in range(H_A)], 0)
        sol = jnp.dot(lhs.astype(BF16), rhs, preferred_element_type=F32)
        it["u"] = jnp.concatenate([sol[h * CHUNK_A:(h + 1) * CHUNK_A, :DV_A] for h in range(H_A)], 1)
        it["w"] = jnp.concatenate([sol[h * CHUNK_A:(h + 1) * CHUNK_A, DV_A:] for h in range(H_A)], 1)

    def finish(bb, r0, rows, c0, width, o):
        if not rev:
            o_ref[bb, r0:r0 + rows, c0:c0 + width] = o
            return
        o = o + of_ref[bb, r0:r0 + rows, c0:c0 + width]
        outs = []
        for hh in range(width // LANES):
            x = o[:, hh * LANES:(hh + 1) * LANES]
            if c0 < D_A:
                x = x * lax.rsqrt(jnp.mean(x * x, -1, keepdims=True) + 1e-6) * naw_ref[...]
            else:
                cb = c0 - D_A + hh * LANES
                x = _layer_norm(x, 1e-5) * gnw_ref[:, cb:cb + LANES] + gnb_ref[:, cb:cb + LANES]
            outs.append(x)
        o = outs[0] if len(outs) == 1 else jnp.concatenate(outs, 1)
        o_ref[bb, r0:r0 + rows, c0:c0 + width] = (o * gates_ref[bb, r0:r0 + rows, c0:c0 + width]).astype(BF16)

    per_step = -(-len(fillers) // nca)
    for c in range(nca):
        grp_items = [st[bb * nca + c] for bb in range(NS)]
        s_old = [sa_scr[it["bb"]] for it in grp_items]
        prods = []
        for it, s in zip(grp_items, s_old):
            q_dec = qkv_ref[it["bb"], it["r0"]:it["r0"] + CHUNK_A, 0:D_A] * jnp.exp(it["gc5"])
            pr = []
            for pp in range(H_A // 2):
                lo = pp * 2 * DV_A
                z = jnp.zeros((DK_A, DV_A), F32)
                bds = jnp.concatenate([jnp.concatenate([s[:, lo:lo + DV_A], z], 1),
                                       jnp.concatenate([z, s[:, lo + DV_A:lo + 2 * DV_A]], 1)], 0)
                lhs = jnp.concatenate([it["w"][:, lo:lo + 2 * DK_A], q_dec[:, lo:lo + 2 * DK_A]], 0)
                pr.append(_dot(lhs, bds))
            prods.append(jnp.concatenate(pr, 1))
        for fill in fillers[c * per_step:(c + 1) * per_step]:
            fill()
        v_new = [it["u"] - pr[:CHUNK_A] for it, pr in zip(grp_items, prods)]
        vbd = [_head_blockdiag(v, bdk_mask) for v in v_new]
        upd = []
        for it, vb_ in zip(grp_items, vbd):
            k_dec = (qkv_ref[it["bb"], it["r0"]:it["r0"] + CHUNK_A, D_A:2 * D_A]
                     * jnp.exp(it["tot5"] - it["gc5"]))
            kd_stack = jnp.concatenate([k_dec[:, h * DK_A:(h + 1) * DK_A] for h in range(H_A)], 0)
            upd.append(lax.dot_general(kd_stack.astype(BF16), vb_, (((0,), (0,)), ((), ())),
                                       preferred_element_type=F32))
        for it, s, x in zip(grp_items, s_old, upd):
            sa_scr[it["bb"]] = s * jnp.exp(it["tot5"]) + x
        for it, pr, vb_ in zip(grp_items, prods, vbd):
            o = pr[CHUNK_A:] + jnp.dot(it["qk"].astype(BF16), vb_, preferred_element_type=F32)
            finish(it["bb"], it["r0"], CHUNK_A, 0, D_A, o)

    for it in rt:
        bb, pp = it["bb"], it["pp"]
        s = sr_scr[bb, pp]
        o = it["inner"] + _dot(it["qx"], s)
        sr_scr[bb, pp] = s * it["g_chunk"] + it["ds"]
        finish(bb, it["r0"], CHUNK_B, D_A + 2 * pp * DV_B, 2 * DV_B, o)

    if emit_state:
        @pl.when(j == nblk - 1)
        def _():
            for bb in range(NS):
                for h in range(H_A):
                    sd_ref[bb, 0, 0, h] = sa_scr[bb, :, h * DV_A:(h + 1) * DV_A]
                for h in range(H_B):
                    lo = (h % 2) * DK_B
                    sr_ref[bb, 0, 0, h] = sr_scr[bb, h // 2, lo:lo + DK_B, (h % 2) * DV_B:(h % 2 + 1) * DV_B]


def _mixer(rev, qkv, qkb, vb, gb, o_fwd, gates, naw, gnw, gnb, sd0, sr0, emit_state):
    bt, t, _ = qkv.shape
    NS = SEQ_PER_STEP
    TB = TIME_BLOCK
    nblk = t // TB
    has_init = sd0 is not None
    d = 1 if rev else 0
    tb_of = (lambda j: nblk - 1 - j) if rev else (lambda j: j)
    tok = lambda w: pl.BlockSpec((NS, TB, w), lambda b, j: (b, tb_of(j), 0))
    in_specs = [tok(QKV_W), tok(QKB_W), tok(D_B), tok(LANES)]
    args = [qkv, qkb, vb, gb]
    if rev:
        in_specs += [tok(D_A + D_B), tok(D_A + D_B), _resident((1, DV_A)), _resident((1, D_B)), _resident((1, D_B))]
        args += [o_fwd, gates, naw, gnw, gnb]
    if has_init:
        in_specs.append(pl.BlockSpec((NS, 1, 1, H_A, DK_A, DV_A), lambda b, j: (b, 0, d, 0, 0, 0)))
        in_specs.append(pl.BlockSpec((NS, 1, 1, H_B, DK_B, DV_B), lambda b, j: (b, 0, d, 0, 0, 0)))
        args += [sd0, sr0]
    out_specs = [tok(D_A + D_B)]
    out_shape = [jax.ShapeDtypeStruct((bt, t, D_A + D_B), BF16 if rev else F32)]
    if emit_state:
        out_specs.append(pl.BlockSpec((NS, 1, 1, H_A, DK_A, DV_A), lambda b, j: (b, 0, 0, 0, 0, 0)))
        out_specs.append(pl.BlockSpec((NS, 1, 1, H_B, DK_B, DV_B), lambda b, j: (b, 0, 0, 0, 0, 0)))
        out_shape.append(jax.ShapeDtypeStruct((bt, DEPTH, 1, H_A, DK_A, DV_A), F32))
        out_shape.append(jax.ShapeDtypeStruct((bt, DEPTH, 1, H_B, DK_B, DV_B), F32))
    scratch = [pltpu.VMEM((NS, DK_A, H_A * DV_A), F32),
               pltpu.VMEM((NS, H_B // 2, LANES, 2 * DV_B), F32)]
    return pl.pallas_call(
        functools.partial(_mixer_kernel, rev, has_init, emit_state, nblk),
        grid=(bt // NS, nblk),
        in_specs=in_specs,
        out_specs=out_specs,
        out_shape=out_shape,
        scratch_shapes=scratch,
        compiler_params=pltpu.CompilerParams(dimension_semantics=("arbitrary", "arbitrary"),
                                             vmem_limit_bytes=VMEM_LIMIT),
        name="mixer_bwd" if rev else "mixer_fwd",
    )(*args)


def _rope_tables(t):
    rows = t // GRID_W
    r = np.repeat(np.arange(rows, dtype=np.float32), GRID_W)
    col = np.tile(np.arange(GRID_W, dtype=np.float32), rows)
    nf = DK_B // 4
    inv = (np.float32(ROPE_BASE) ** (-np.arange(nf, dtype=np.float32) / np.float32(nf))).astype(np.float32)
    ang = jnp.asarray(np.concatenate([r[:, None] * inv, col[:, None] * inv], -1).astype(np.float32))
    cos, sin = jnp.cos(ang), jnp.sin(ang)
    zero = jnp.zeros_like(sin)
    cos_t = jnp.tile(cos, (1, 4))
    sin_up = jnp.tile(jnp.concatenate([-sin, zero], -1), (1, 2))
    sin_dn = jnp.tile(jnp.concatenate([zero, sin], -1), (1, 2))
    return jnp.stack([cos_t, sin_up, sin_dn], 0)


def kernel(x_prompt, x_sample, c, state_delta, state_ret, c_ctx, w_mod, b_mod, w_in, conv_w, a_log, dt_bias,
           norm_a_w, gn_w, gn_b, w_o, ln1_w, ln1_b, w_ff1, b_ff1, w_ff2, b_ff2, ln2_w, ln2_b):
    assert w_mod.shape[0] == DEPTH == 1
    n_dec = c.shape[0]
    rows = -(-(1 + n_dec) // SUBLANES) * SUBLANES
    cond = jnp.zeros((rows, D_MODEL), F32).at[0].set(c_ctx).at[1:1 + n_dec].set(c)
    mod = _modulation(cond, w_mod[0], b_mod[0])
    mod3 = mod.reshape(rows, 1, 6 * D_MODEL)

    w = w_in[0]
    c_ab = QKV_W + D_A
    w_in_p = (jnp.zeros((D_MODEL, P_W), BF16)
              .at[:, :c_ab].set(w[:, :c_ab].astype(BF16))
              .at[:, c_ab:COL_AB].set(w[:, c_ab + 2 * N_AB:].astype(BF16))
              .at[:, COL_AB:COL_AB + 2 * N_AB].set(w[:, c_ab:c_ab + 2 * N_AB].astype(BF16)))
    convw = jnp.zeros((SUBLANES, QKV_W), F32).at[:CONV_K].set(conv_w[0])
    par = (jnp.zeros((SUBLANES, LANES), F32).at[0, :N_AB].set(a_log[0].reshape(-1))
           .at[1, :N_AB].set(dt_bias[0].reshape(-1)))
    naw = norm_a_w[0].reshape(1, DV_A)
    gnw = gn_w[0].reshape(1, D_B)
    gnb = gn_b[0].reshape(1, D_B)
    wo = w_o[0].astype(BF16)
    w1 = w_ff1[0].astype(BF16)
    w2 = w_ff2[0].astype(BF16)
    row = lambda v: v[0].reshape(1, -1)

    def trunk(x, row0, row_stride, rope, sd0, sr0, emit_state, tm):
        qkv, qkb, vb, gates, gb = _projection(x, mod3, row0, row_stride, w_in_p, convw, par, rope, tm)
        fwd = _mixer(False, qkv, qkb, vb, gb, None, None, None, None, None, sd0, sr0, emit_state)
        bwd = _mixer(True, qkv, qkb, vb, gb, fwd[0], gates, naw, gnw, gnb, sd0, sr0, emit_state)
        o = bwd[0]
        if row_stride == 0:
            y = _post(o.reshape(1, -1, D_A + D_B), x.reshape(1, -1, D_MODEL), mod3, row0, 0, wo, row(ln1_w),
                      row(ln1_b), w1, row(b_ff1), w2, row(b_ff2), row(ln2_w), row(ln2_b), POST_TILE).reshape(x.shape)
        else:
            y = _post(o, x, mod3, row0, row_stride, wo, row(ln1_w), row(ln1_b), w1, row(b_ff1), w2, row(b_ff2),
                      row(ln2_w), row(ln2_b), POST_TILE)
        states = [jnp.concatenate([f, b], 2) for f, b in zip(fwd[1:], bwd[1:])]
        return y, states

    y_prompt, (new_sd, new_sr) = trunk(x_prompt, 0, 0, None, None, None, True, 256)
    y_sample, _ = trunk(x_sample, 1, 1, _rope_tables(x_sample.shape[1]), state_delta, state_ret, False, 512)
    return y_prompt, y_sample, new_sd, new_sr
```

```python
import functools
import math

import jax
import jax.numpy as jnp
import numpy as np
from jax import lax
from jax.experimental import pallas as pl
from jax.experimental.pallas import tpu as pltpu

F32 = jnp.float32
BF16 = jnp.bfloat16

D_MODEL = 1024
H_A, DK_A, DV_A = 4, 128, 128
D_A = H_A * DV_A
CONV_K = 5
CHUNK_A = 64
H_B, DK_B, DV_B = 4, 64, 128
D_B = H_B * DV_B
CHUNK_B = 128
GRID_W = 64
ROPE_BASE = 10000.0
D_FF = 4 * D_MODEL
DEPTH = 1
ALPHA = (2.0 * DEPTH) ** 0.25

LANES = 128
SUBLANES = 8
VMEM_LIMIT = 56 * 1024 * 1024

QKV_W = 3 * D_A
COL_GATE_A = QKV_W
COL_QK_B = COL_GATE_A + D_A
QKB_W = 2 * H_B * DK_B
COL_V_B = COL_QK_B + QKB_W
COL_GATE_B = COL_V_B + D_B
COL_AB = COL_GATE_B + D_B
P_W = COL_AB + LANES
N_AB = 2 * H_A

TIME_BLOCK = 256
HALO = SUBLANES
PROJ_GROUP = 256
PROJ_PART = 128
assert CONV_K == 5 and (CONV_K - 1) // 2 <= HALO


def _dot(a, b):
    return jnp.dot(a.astype(BF16), b.astype(BF16), preferred_element_type=F32)


def _dot_nt(a, b):
    return lax.dot_general(a.astype(BF16), b.astype(BF16), (((1,), (1,)), ((), ())),
                           preferred_element_type=F32)


def _dot_tn(a, b):
    return lax.dot_general(a.astype(BF16), b.astype(BF16), (((0,), (0,)), ((), ())),
                           preferred_element_type=F32)


def _split3(x):
    hi = x.astype(BF16)
    r = x - hi.astype(F32)
    mid = r.astype(BF16)
    lo = (r - mid.astype(F32)).astype(BF16)
    return hi, mid, lo


def _silu(x):
    h = 0.5 * x
    return h + h * jnp.tanh(h)


def _layer_norm(x, eps):
    mu = jnp.mean(x, -1, keepdims=True)
    xc = x - mu
    var = jnp.mean(xc * xc, -1, keepdims=True)
    return xc * lax.rsqrt(var + eps)


def _resident(shape):
    return pl.BlockSpec(shape, lambda *_: (0,) * len(shape), pipeline_mode=pl.Buffered(1))


def _mod_kernel(c_ref, w_ref, b_ref, o_ref):
    o_ref[...] = _dot(_silu(c_ref[...]), w_ref[...]) + b_ref[...]


def _modulation(cond, w_mod, b_mod):
    rows = cond.shape[0]
    n = w_mod.shape[1]
    bn = 1536
    return pl.pallas_call(
        _mod_kernel,
        grid=(n // bn,),
        in_specs=[pl.BlockSpec((rows, D_MODEL), lambda j: (0, 0)),
                  pl.BlockSpec((D_MODEL, bn), lambda j: (0, j)),
                  pl.BlockSpec((1, bn), lambda j: (0, j))],
        out_specs=pl.BlockSpec((rows, bn), lambda j: (0, j)),
        out_shape=jax.ShapeDtypeStruct((rows, n), F32),
        compiler_params=pltpu.CompilerParams(dimension_semantics=("arbitrary",)),
        name="modulation",
    )(cond, w_mod, b_mod.reshape(1, n))


def _proj_kernel(use_rope, tm, nt, *refs):
    if use_rope:
        (x_ref, xp_ref, xn_ref, mod_ref, wa_ref, wb_ref, wab_ref, convw_ref, par_ref, rope_ref,
         qkv_ref, qkb_ref, vb_ref, gates_ref, gb_ref) = refs
    else:
        (x_ref, xp_ref, xn_ref, mod_ref, wa_ref, wb_ref, wab_ref, convw_ref, par_ref,
         qkv_ref, qkb_ref, vb_ref, gates_ref, gb_ref) = refs
    i = pl.program_id(1)

    def weights(c0, width):
        if c0 < COL_QK_B:
            return wa_ref[:, c0:c0 + width]
        if c0 < COL_AB:
            return wb_ref[:, c0 - COL_QK_B:c0 - COL_QK_B + width]
        return wab_ref[...]

    m = mod_ref[0]
    sh1 = m[:, 0:D_MODEL]
    sc1 = m[:, D_MODEL:2 * D_MODEL]

    def modulated(x):
        return _layer_norm(x, 1e-6) * (1.0 + sc1) + sh1

    part = PROJ_PART
    n_parts = tm // part
    rows = part + 2 * HALO

    def modulated_rows(s):
        lo = s * part
        above = xp_ref[0] if s == 0 else x_ref[0, lo - HALO:lo]
        below = xn_ref[0] if s == n_parts - 1 else x_ref[0, lo + part:lo + part + HALO]
        return modulated(jnp.concatenate([above, x_ref[0, lo:lo + part], below], 0)).astype(BF16)

    def conv_silu(pc, c0, s):
        above = jnp.where(i > 0, pc[0:HALO], 0.0) if s == 0 else pc[0:HALO]
        below = jnp.where(i < nt - 1, pc[HALO + part:], 0.0) if s == n_parts - 1 else pc[HALO + part:]
        xe = jnp.concatenate([above, pc[HALO:HALO + part], below], 0)
        z = [xe * convw_ref[kk:kk + 1, c0:c0 + pc.shape[1]] for kk in range(CONV_K)]
        back = pltpu.roll(z[1] + pltpu.roll(z[0], 1, 0), 1, 0)
        ahead = pltpu.roll(z[3] + pltpu.roll(z[4], rows - 1, 0), rows - 1, 0)
        return _silu((z[2] + back + ahead)[HALO:HALO + part])

    def l2n(x):
        return x * lax.rsqrt(jnp.sum(x * x, -1, keepdims=True) + 1e-6)

    def finish_qkv(pc, c0, s):
        y = conv_silu(pc, c0, s)
        for hh in range(pc.shape[1] // DK_A):
            c = c0 + hh * DK_A
            x = y[:, hh * DK_A:(hh + 1) * DK_A]
            if c < D_A:
                x = l2n(x) * (DK_A ** -0.5)
            elif c < 2 * D_A:
                x = l2n(x)
            qkv_ref[0, s * part:(s + 1) * part, c:c + DK_A] = x

    def finish_rest(pc, c0, s):
        pc = pc[HALO:HALO + part]
        w = pc.shape[1]
        out_rows = slice(s * part, (s + 1) * part)
        if c0 < COL_QK_B:
            gates_ref[0, out_rows, c0 - COL_GATE_A:c0 - COL_GATE_A + w] = _silu(pc)
        elif c0 < COL_V_B:
            for hh in range(w // LANES):
                x = pc[:, hh * LANES:(hh + 1) * LANES]
                if use_rope:
                    x = (x * rope_ref[0, out_rows] + pltpu.roll(x, LANES - DK_B // 2, 1) * rope_ref[1, out_rows]
                         + pltpu.roll(x, DK_B // 2, 1) * rope_ref[2, out_rows])
                c = c0 - COL_QK_B + hh * LANES
                if c < H_B * DK_B:
                    x = x * (DK_B ** -0.5)
                qkb_ref[0, out_rows, c:c + LANES] = x
        elif c0 < COL_GATE_B:
            vb_ref[0, out_rows, c0 - COL_V_B:c0 - COL_V_B + w] = pc.astype(BF16)
        elif c0 < COL_AB:
            gates_ref[0, out_rows, D_A + c0 - COL_GATE_B:D_A + c0 - COL_GATE_B + w] = _silu(pc)
        else:
            z = pc + par_ref[1:2, :]
            softplus = jnp.maximum(z, 0.0) + jnp.log1p(jnp.exp(-jnp.abs(z)))
            g_all = -jnp.exp(par_ref[0:1, :]) * softplus
            beta_all = 1.0 / (1.0 + jnp.exp(-pc))
            lane = lax.broadcasted_iota(jnp.int32, (part, LANES), 1)
            gb_ref[0, out_rows] = jnp.where(lane < N_AB, g_all, jnp.where(lane < 2 * N_AB, beta_all, 0.0))

    heavy = [(finish_qkv, c) for c in range(0, QKV_W, PROJ_GROUP)]
    light = [(finish_rest, c) for c in range(QKV_W, P_W, PROJ_GROUP)]
    order = []
    while heavy or light:
        if heavy:
            order.append(heavy.pop(0))
        if light:
            order.append(light.pop(0))
    previous = None
    hb = modulated_rows(0)
    for s in range(n_parts):
        hb_next = None
        for n, (fn, c0) in enumerate(order):
            pc = jnp.dot(hb, weights(c0, min(PROJ_GROUP, P_W - c0)), preferred_element_type=F32)
            if n == 1 and s + 1 < n_parts:
                hb_next = modulated_rows(s + 1)
            if previous is not None:
                previous[0](*previous[1:])
            previous = (fn, pc, c0, s)
        hb = hb_next
    previous[0](*previous[1:])


def _projection(x, mod3, row0, row_stride, w_in_p, convw, par, rope, tm):
    bt, t, _ = x.shape
    nt = t // tm
    hpt = tm // HALO
    nh = t // HALO
    use_rope = rope is not None
    in_specs = [pl.BlockSpec((1, tm, D_MODEL), lambda b, i: (b, i, 0)),
                pl.BlockSpec((1, HALO, D_MODEL), lambda b, i: (b, jnp.maximum(i * hpt - 1, 0), 0)),
                pl.BlockSpec((1, HALO, D_MODEL), lambda b, i: (b, jnp.minimum((i + 1) * hpt, nh - 1), 0)),
                pl.BlockSpec((1, 1, 6 * D_MODEL), lambda b, i: (row0 + row_stride * b, 0, 0)),
                _resident((D_MODEL, COL_QK_B)),
                _resident((D_MODEL, COL_AB - COL_QK_B)),
                _resident((D_MODEL, LANES)),
                _resident((SUBLANES, QKV_W)),
                _resident((SUBLANES, LANES))]
    args = [x, x, x, mod3, *w_in_p, convw, par]
    if use_rope:
        in_specs.append(pl.BlockSpec((3, tm, LANES), lambda b, i: (0, i, 0)))
        args.append(rope)
    widths = (QKV_W, QKB_W, D_B, D_A + D_B, LANES)
    dtypes = (F32, F32, BF16, F32, F32)
    return pl.pallas_call(
        functools.partial(_proj_kernel, use_rope, tm, nt),
        grid=(bt, nt),
        in_specs=in_specs,
        out_specs=[pl.BlockSpec((1, tm, w), lambda b, i: (b, i, 0)) for w in widths],
        out_shape=[jax.ShapeDtypeStruct((bt, t, w), dt) for w, dt in zip(widths, dtypes)],
        compiler_params=pltpu.CompilerParams(dimension_semantics=("arbitrary", "arbitrary"),
                                             vmem_limit_bytes=VMEM_LIMIT),
        name="projection",
    )(*args)


POST_SPLIT = 2
POST_TILE = 512


def _post_kernel(o_ref, x_ref, mod_ref, wo_ref, ln1w_ref, ln1b_ref, w1_ref, b1_ref, w2_ref, b2_ref,
                 ln2w_ref, ln2b_ref, y_ref):
    m = mod_ref[0]
    g1 = m[:, 2 * D_MODEL:3 * D_MODEL]
    sh2 = m[:, 3 * D_MODEL:4 * D_MODEL]
    sc2 = m[:, 4 * D_MODEL:5 * D_MODEL]
    g2 = m[:, 5 * D_MODEL:6 * D_MODEL]
    rows = o_ref.shape[1] // POST_SPLIT
    parts = [slice(r * rows, (r + 1) * rows) for r in range(POST_SPLIT)]
    ys = [jnp.dot(o_ref[0, sl], wo_ref[...], preferred_element_type=F32) for sl in parts]
    x1s = [_layer_norm(ALPHA * x_ref[0, sl] + g1 * y, 1e-6) * ln1w_ref[...] + ln1b_ref[...]
           for sl, y in zip(parts, ys)]
    hs = [(_layer_norm(x1, 1e-6) * (1.0 + sc2) + sh2).astype(BF16) for x1 in x1s]
    acts = [jnp.square(jnp.maximum(jnp.dot(h, w1_ref[...], preferred_element_type=F32) + b1_ref[...], 0.0))
            .astype(BF16) for h in hs]
    fs = [jnp.dot(a, w2_ref[...], preferred_element_type=F32) + b2_ref[...] for a in acts]
    for sl, x1, f in zip(parts, x1s, fs):
        y_ref[0, sl] = _layer_norm(ALPHA * x1 + g2 * f, 1e-6) * ln2w_ref[...] + ln2b_ref[...]


def _post(o, x, mod3, row0, row_stride, wo, ln1w, ln1b, w1, b1, w2, b2, ln2w, ln2b, tm):
    bt, t, _ = x.shape
    return pl.pallas_call(
        _post_kernel,
        grid=(bt, t // tm),
        in_specs=[pl.BlockSpec((1, tm, D_MODEL), lambda b, i: (b, i, 0)),
                  pl.BlockSpec((1, tm, D_MODEL), lambda b, i: (b, i, 0)),
                  pl.BlockSpec((1, 1, 6 * D_MODEL), lambda b, i: (row0 + row_stride * b, 0, 0)),
                  _resident((D_MODEL, D_MODEL)),
                  _resident((1, D_MODEL)),
                  _resident((1, D_MODEL)),
                  _resident((D_MODEL, D_FF)),
                  _resident((1, D_FF)),
                  _resident((D_FF, D_MODEL)),
                  _resident((1, D_MODEL)),
                  _resident((1, D_MODEL)),
                  _resident((1, D_MODEL))],
        out_specs=pl.BlockSpec((1, tm, D_MODEL), lambda b, i: (b, i, 0)),
        out_shape=jax.ShapeDtypeStruct((bt, t, D_MODEL), F32),
        compiler_params=pltpu.CompilerParams(dimension_semantics=("arbitrary", "arbitrary"),
                                             vmem_limit_bytes=VMEM_LIMIT),
        name="post",
    )(o, x, mod3, wo, ln1w, ln1b, w1, b1, w2, b2, ln2w, ln2b)


_LOG_GAMMA = [math.log1p(-(2.0 ** (-5.0 - h))) for h in range(H_B)]

SOLVE_BASE = 8
SEQ_PER_STEP = 4
CAT_W = H_A * CHUNK_A

def _head_blockdiag(x, mask):
    return jnp.where(mask, jnp.concatenate([x] * H_A, 0), 0.0).astype(BF16)


def _cat_inverses(ms, bd_mask):
    i = lax.broadcasted_iota(jnp.int32, (CHUNK_A, CAT_W), 0)
    j = lax.broadcasted_iota(jnp.int32, (CHUNK_A, CAT_W), 1) % CHUNK_A
    same = lambda s: (i // s) == (j // s)
    eye = jnp.where(i == j, 1.0, 0.0)
    mul = lambda a, bd: jnp.dot(a.astype(BF16), bd, preferred_element_type=F32)
    ps = [jnp.where(same(SOLVE_BASE), -m, 0.0) for m in ms]
    ts = [eye + p for p in ps]
    bds = [_head_blockdiag(p, bd_mask) for p in ps]
    for _ in range(int(math.log2(SOLVE_BASE)) - 1):
        ps = [mul(p, bd) for p, bd in zip(ps, bds)]
        bds = [_head_blockdiag(p, bd_mask) for p in ps]
        ts = [t + mul(t, bd) for t, bd in zip(ts, bds)]
    s = SOLVE_BASE
    while s < CHUNK_A:
        sel = same(2 * s) & jnp.logical_not(same(s))
        bds = [_head_blockdiag(t, bd_mask) for t in ts]
        ys = [mul(jnp.where(sel, m, 0.0), bd) for m, bd in zip(ms, bds)]
        bds = [_head_blockdiag(y, bd_mask) for y in ys]
        ts = [t - mul(t, bd) for t, bd in zip(ts, bds)]
        s *= 2
    return ts


def _mixer_kernel(rev, has_init, emit_state, nblk, *refs):
    NS = SEQ_PER_STEP
    TB = TIME_BLOCK
    d = 1 if rev else 0
    j = pl.program_id(1)
    qkv_ref, qkb_ref, vb_ref, gb_ref = refs[:4]
    pos_ = 4
    if rev:
        of_ref, gates_ref, naw_ref, gnw_ref, gnb_ref = refs[pos_:pos_ + 5]
        pos_ += 5
    if has_init:
        sd0_ref, sr0_ref = refs[pos_:pos_ + 2]
        pos_ += 2
    o_ref = refs[pos_]
    pos_ += 1
    if emit_state:
        sd_ref, sr_ref = refs[pos_:pos_ + 2]
        pos_ += 2
    sa_scr, sr_scr = refs[pos_:pos_ + 2]

    @pl.when(j == 0)
    def _():
        if has_init:
            for bb in range(NS):
                for h in range(H_A):
                    sa_scr[bb, :, h * DV_A:(h + 1) * DV_A] = sd0_ref[bb, 0, 0, h]
                sr_scr[bb] = jnp.zeros(sr_scr.shape[1:], F32)
                for h in range(H_B):
                    lo = (h % 2) * DK_B
                    sr_scr[bb, h // 2, lo:lo + DK_B, (h % 2) * DV_B:(h % 2 + 1) * DV_B] = sr0_ref[bb, 0, 0, h]
        else:
            sa_scr[...] = jnp.zeros(sa_scr.shape, F32)
            sr_scr[...] = jnp.zeros(sr_scr.shape, F32)

    ri = lax.broadcasted_iota(jnp.int32, (TB, TB), 0)
    ci = lax.broadcasted_iota(jnp.int32, (TB, TB), 1)
    cum = jnp.where(((ri // CHUNK_A) == (ci // CHUNK_A)) & ((ci >= ri) if rev else (ci <= ri)),
                    1.0, 0.0).astype(BF16)
    er = lax.broadcasted_iota(jnp.int32, (LANES, CAT_W), 0)
    ec = lax.broadcasted_iota(jnp.int32, (LANES, CAT_W), 1)
    esel_c = jnp.where(er == d * H_A + ec // CHUNK_A, 1.0, 0.0).astype(BF16)
    lane_tb = lax.broadcasted_iota(jnp.int32, (TB, LANES), 1)
    i64 = lax.broadcasted_iota(jnp.int32, (CHUNK_A, CAT_W), 0)
    j64 = lax.broadcasted_iota(jnp.int32, (CHUNK_A, CAT_W), 1) % CHUNK_A
    tri = (i64 <= j64) if rev else (i64 >= j64)
    strict = (i64 < j64) if rev else (i64 > j64)
    bd_mask = (lax.broadcasted_iota(jnp.int32, (CAT_W, CAT_W), 0) // CHUNK_A
               == lax.broadcasted_iota(jnp.int32, (CAT_W, CAT_W), 1) // CHUNK_A)
    bdk_mask = (lax.broadcasted_iota(jnp.int32, (CAT_W, D_A), 0) // CHUNK_A
                == lax.broadcasted_iota(jnp.int32, (CAT_W, D_A), 1) // DK_A)
    lane_head = lax.broadcasted_iota(jnp.int32, (CHUNK_A, CAT_W), 1) // CHUNK_A
    nca = TB // CHUNK_A
    order_a = list(range(nca - 1, -1, -1)) if rev else list(range(nca))

    st = []
    for bb in range(NS):
        gbk = gb_ref[bb]
        gc = None
        for part in _split3(gbk):
            t_ = jnp.dot(cum, part, preferred_element_type=F32)
            gc = t_ if gc is None else gc + t_
        gct = None
        for part in _split3(gbk.T):
            t_ = lax.dot_general(part, cum, (((1,), (1,)), ((), ())), preferred_element_type=F32)
            gct = t_ if gct is None else gct + t_
        gcb = jnp.where(lane_tb < N_AB, gc, gbk)
        gcs = None
        for part in _split3(gcb):
            t_ = jnp.dot(part, esel_c, preferred_element_type=F32)
            gcs = t_ if gcs is None else gcs + t_
        gc5a = jnp.concatenate([jnp.broadcast_to(gcb[:, d * H_A + h:d * H_A + h + 1], (TB, DK_A))
                                for h in range(H_A)], 1)
        b5a = jnp.concatenate([jnp.broadcast_to(gcb[:, N_AB + d * H_A + h:N_AB + d * H_A + h + 1], (TB, DK_A))
                               for h in range(H_A)], 1)
        for n in order_a:
            r0 = n * CHUNK_A
            last = r0 if rev else r0 + CHUNK_A - 1
            it = dict(bb=bb, r0=r0, gc5=gc5a[r0:r0 + CHUNK_A], tot5=gc5a[last:last + 1], b5=b5a[r0:r0 + CHUNK_A])
            q = qkv_ref[bb, r0:r0 + CHUNK_A, 0:D_A]
            k = qkv_ref[bb, r0:r0 + CHUNK_A, D_A:2 * D_A]
            gcc = gcs[r0:r0 + CHUNK_A]
            gcr = jnp.concatenate([gct[d * H_A + h:d * H_A + h + 1, r0:r0 + CHUNK_A] for h in range(H_A)], 1)
            decay = jnp.where(tri, jnp.exp(jnp.where(tri, gcc - gcr, 0.0)), 0.0)
            mq = _dot_nt(jnp.concatenate([k * it["b5"], q], 0), _head_blockdiag(k, bdk_mask))
            it["m"] = jnp.where(strict, mq[:CHUNK_A] * decay, 0.0)
            it["qk"] = jnp.where(tri, mq[CHUNK_A:] * decay, 0.0)
            st.append(it)
    lane = lax.broadcasted_iota(jnp.int32, (CHUNK_B, LANES), 1)
    pi = lax.broadcasted_iota(jnp.int32, (CHUNK_B, CHUNK_B), 0)
    pj = lax.broadcasted_iota(jnp.int32, (CHUNK_B, CHUNK_B), 1)
    pos = lax.broadcasted_iota(jnp.int32, (CHUNK_B, 1), 0)
    if rev:
        pi, pj, pos = CHUNK_B - 1 - pi, CHUNK_B - 1 - pj, CHUNK_B - 1 - pos
    pdiff = (pi - pj).astype(F32)
    posf = pos.astype(F32)
    first_half = (lane // DK_B) == 0
    own_block = (lax.broadcasted_iota(jnp.int32, (LANES, 2 * DV_B), 0) // DK_B
                 == lax.broadcasted_iota(jnp.int32, (LANES, 2 * DV_B), 1) // DV_B)
    ncb = TB // CHUNK_B
    order_b = range(ncb - 1, -1, -1) if rev else range(ncb)
    rt = []
    for pp in range(H_B // 2):
        lgs = [_LOG_GAMMA[H_B - 1 - h] if rev else _LOG_GAMMA[h] for h in (2 * pp, 2 * pp + 1)]
        d_mask = jnp.concatenate([jnp.where(pdiff >= 0, jnp.exp(lg * jnp.maximum(pdiff, 0.0)), 0.0) for lg in lgs], 1)
        xi = jnp.where(first_half, jnp.exp(lgs[0] * (posf + 1.0)), jnp.exp(lgs[1] * (posf + 1.0)))
        zeta = jnp.where(first_half, jnp.exp(lgs[0] * (CHUNK_B - 1.0 - posf)), jnp.exp(lgs[1] * (CHUNK_B - 1.0 - posf)))
        col_head = lax.broadcasted_iota(jnp.int32, (1, 2 * DV_B), 1) // DV_B
        g_chunk = jnp.where(col_head == 0, math.exp(lgs[0] * CHUNK_B), math.exp(lgs[1] * CHUNK_B))
        for bb in range(NS):
            for n in order_b:
                rt.append(dict(bb=bb, r0=n * CHUNK_B, pp=pp, g_chunk=g_chunk, d_mask=d_mask, xi=xi, zeta=zeta))

    def retention_independent(it):
        bb, r0, pp = it["bb"], it["r0"], it["pp"]
        q = qkb_ref[bb, r0:r0 + CHUNK_B, pp * LANES:(pp + 1) * LANES]
        k = qkb_ref[bb, r0:r0 + CHUNK_B, H_B * DK_B + pp * LANES:H_B * DK_B + (pp + 1) * LANES]
        v0 = vb_ref[bb, r0:r0 + CHUNK_B, 2 * pp * DV_B:(2 * pp + 1) * DV_B]
        v1 = vb_ref[bb, r0:r0 + CHUNK_B, (2 * pp + 1) * DV_B:(2 * pp + 2) * DV_B]
        zb = jnp.zeros((CHUNK_B, DV_B), BF16)
        ksplit = jnp.concatenate([jnp.where(first_half, k, 0.0), jnp.where(first_half, 0.0, k)], 0)
        scores = _dot_nt(q, ksplit) * it["d_mask"]
        v_bd = jnp.concatenate([jnp.concatenate([v0, zb], 1), jnp.concatenate([zb, v1], 1)], 0)
        it["inner"] = jnp.dot(scores.astype(BF16), v_bd, preferred_element_type=F32)
        it["qx"] = q * it["xi"]
        ds = lax.dot_general((k * it["zeta"]).astype(BF16), jnp.concatenate([v0, v1], 1),
                             (((0,), (0,)), ((), ())), preferred_element_type=F32)
        it["ds"] = jnp.where(own_block, ds, 0.0)

    fillers = [functools.partial(retention_independent, it) for it in rt]
    inv = _cat_inverses([it["m"] for it in st], bd_mask)
    for it, t in zip(st, inv):
        bb, r0 = it["bb"], it["r0"]
        k = qkv_ref[bb, r0:r0 + CHUNK_A, D_A:2 * D_A]
        v = qkv_ref[bb, r0:r0 + CHUNK_A, 2 * D_A:3 * D_A]
        vb5 = v * it["b5"]
        kbe5 = (k * it["b5"]) * jnp.exp(it["gc5"])
        rhs = jnp.concatenate(
            [jnp.concatenate([vb5[:, h * DV_A:(h + 1) * DV_A], kbe5[:, h * DK_A:(h + 1) * DK_A]], 1)
             for h in range(H_A)], 0).astype(BF16)
        lhs = jnp.concatenate([jnp.where(lane_head == h, t, 0.0) for h in range(H_A)], 0)
        sol = jnp.dot(lhs.astype(BF16), rhs, preferred_element_type=F32)
        it["u"] = jnp.concatenate([sol[h * CHUNK_A:(h + 1) * CHUNK_A, :DV_A] for h in range(H_A)], 1)
        it["w"] = jnp.concatenate([sol[h * CHUNK_A:(h + 1) * CHUNK_A, DV_A:] for h in range(H_A)], 1)

    def finish(bb, r0, rows, c0, width, o):
        if not rev:
            o_ref[bb, r0:r0 + rows, c0:c0 + width] = o
            return
        o = o + of_ref[bb, r0:r0 + rows, c0:c0 + width]
        outs = []
        for hh in range(width // LANES):
            x = o[:, hh * LANES:(hh + 1) * LANES]
            if c0 < D_A:
                x = x * lax.rsqrt(jnp.mean(x * x, -1, keepdims=True) + 1e-6) * naw_ref[...]
            else:
                cb = c0 - D_A + hh * LANES
                x = _layer_norm(x, 1e-5) * gnw_ref[:, cb:cb + LANES] + gnb_ref[:, cb:cb + LANES]
            outs.append(x)
        o = outs[0] if len(outs) == 1 else jnp.concatenate(outs, 1)
        o_ref[bb, r0:r0 + rows, c0:c0 + width] = (o * gates_ref[bb, r0:r0 + rows, c0:c0 + width]).astype(BF16)

    per_step = -(-len(fillers) // nca)
    for c in range(nca):
        grp_items = [st[bb * nca + c] for bb in range(NS)]
        s_old = [sa_scr[it["bb"]] for it in grp_items]
        prods = []
        for it, s in zip(grp_items, s_old):
            q_dec = qkv_ref[it["bb"], it["r0"]:it["r0"] + CHUNK_A, 0:D_A] * jnp.exp(it["gc5"])
            pr = []
            for pp in range(H_A // 2):
                lo = pp * 2 * DV_A
                z = jnp.zeros((DK_A, DV_A), F32)
                bds = jnp.concatenate([jnp.concatenate([s[:, lo:lo + DV_A], z], 1),
                                       jnp.concatenate([z, s[:, lo + DV_A:lo + 2 * DV_A]], 1)], 0)
                lhs = jnp.concatenate([it["w"][:, lo:lo + 2 * DK_A], q_dec[:, lo:lo + 2 * DK_A]], 0)
                pr.append(_dot(lhs, bds))
            prods.append(jnp.concatenate(pr, 1))
        for fill in fillers[c * per_step:(c + 1) * per_step]:
            fill()
        v_new = [it["u"] - pr[:CHUNK_A] for it, pr in zip(grp_items, prods)]
        vbd = [_head_blockdiag(v, bdk_mask) for v in v_new]
        upd = []
        for it, vb_ in zip(grp_items, vbd):
            k_dec = (qkv_ref[it["bb"], it["r0"]:it["r0"] + CHUNK_A, D_A:2 * D_A]
                     * jnp.exp(it["tot5"] - it["gc5"]))
            kd_stack = jnp.concatenate([k_dec[:, h * DK_A:(h + 1) * DK_A] for h in range(H_A)], 0)
            upd.append(lax.dot_general(kd_stack.astype(BF16), vb_, (((0,), (0,)), ((), ())),
                                       preferred_element_type=F32))
        for it, s, x in zip(grp_items, s_old, upd):
            sa_scr[it["bb"]] = s * jnp.exp(it["tot5"]) + x
        for it, pr, vb_ in zip(grp_items, prods, vbd):
            o = pr[CHUNK_A:] + jnp.dot(it["qk"].astype(BF16), vb_, preferred_element_type=F32)
            finish(it["bb"], it["r0"], CHUNK_A, 0, D_A, o)

    for it in rt:
        bb, pp = it["bb"], it["pp"]
        s = sr_scr[bb, pp]
        o = it["inner"] + _dot(it["qx"], s)
        sr_scr[bb, pp] = s * it["g_chunk"] + it["ds"]
        finish(bb, it["r0"], CHUNK_B, D_A + 2 * pp * DV_B, 2 * DV_B, o)

    if emit_state:
        @pl.when(j == nblk - 1)
        def _():
            for bb in range(NS):
                for h in range(H_A):
                    sd_ref[bb, 0, 0, h] = sa_scr[bb, :, h * DV_A:(h + 1) * DV_A]
                for h in range(H_B):
                    lo = (h % 2) * DK_B
                    sr_ref[bb, 0, 0, h] = sr_scr[bb, h // 2, lo:lo + DK_B, (h % 2) * DV_B:(h % 2 + 1) * DV_B]


def _mixer(rev, qkv, qkb, vb, gb, o_fwd, gates, naw, gnw, gnb, sd0, sr0, emit_state):
    bt, t, _ = qkv.shape
    NS = SEQ_PER_STEP
    TB = TIME_BLOCK
    nblk = t // TB
    has_init = sd0 is not None
    d = 1 if rev else 0
    tb_of = (lambda j: nblk - 1 - j) if rev else (lambda j: j)
    tok = lambda w: pl.BlockSpec((NS, TB, w), lambda b, j: (b, tb_of(j), 0))
    in_specs = [tok(QKV_W), tok(QKB_W), tok(D_B), tok(LANES)]
    args = [qkv, qkb, vb, gb]
    if rev:
        in_specs += [tok(D_A + D_B), tok(D_A + D_B), _resident((1, DV_A)), _resident((1, D_B)), _resident((1, D_B))]
        args += [o_fwd, gates, naw, gnw, gnb]
    if has_init:
        in_specs.append(pl.BlockSpec((NS, 1, 1, H_A, DK_A, DV_A), lambda b, j: (b, 0, d, 0, 0, 0)))
        in_specs.append(pl.BlockSpec((NS, 1, 1, H_B, DK_B, DV_B), lambda b, j: (b, 0, d, 0, 0, 0)))
        args += [sd0, sr0]
    out_specs = [tok(D_A + D_B)]
    out_shape = [jax.ShapeDtypeStruct((bt, t, D_A + D_B), BF16 if rev else F32)]
    if emit_state:
        out_specs.append(pl.BlockSpec((NS, 1, 1, H_A, DK_A, DV_A), lambda b, j: (b, 0, 0, 0, 0, 0)))
        out_specs.append(pl.BlockSpec((NS, 1, 1, H_B, DK_B, DV_B), lambda b, j: (b, 0, 0, 0, 0, 0)))
        out_shape.append(jax.ShapeDtypeStruct((bt, DEPTH, 1, H_A, DK_A, DV_A), F32))
        out_shape.append(jax.ShapeDtypeStruct((bt, DEPTH, 1, H_B, DK_B, DV_B), F32))
    scratch = [pltpu.VMEM((NS, DK_A, H_A * DV_A), F32),
               pltpu.VMEM((NS, H_B // 2, LANES, 2 * DV_B), F32)]
    return pl.pallas_call(
        functools.partial(_mixer_kernel, rev, has_init, emit_state, nblk),
        grid=(bt // NS, nblk),
        in_specs=in_specs,
        out_specs=out_specs,
        out_shape=out_shape,
        scratch_shapes=scratch,
        compiler_params=pltpu.CompilerParams(dimension_semantics=("arbitrary", "arbitrary"),
                                             vmem_limit_bytes=VMEM_LIMIT),
        name="mixer_bwd" if rev else "mixer_fwd",
    )(*args)


def _rope_tables(t):
    rows = t // GRID_W
    r = np.repeat(np.arange(rows, dtype=np.float32), GRID_W)
    col = np.tile(np.arange(GRID_W, dtype=np.float32), rows)
    nf = DK_B // 4
    inv = (np.float32(ROPE_BASE) ** (-np.arange(nf, dtype=np.float32) / np.float32(nf))).astype(np.float32)
    ang = jnp.asarray(np.concatenate([r[:, None] * inv, col[:, None] * inv], -1).astype(np.float32))
    cos, sin = jnp.cos(ang), jnp.sin(ang)
    zero = jnp.zeros_like(sin)
    cos_t = jnp.tile(cos, (1, 4))
    sin_up = jnp.tile(jnp.concatenate([-sin, zero], -1), (1, 2))
    sin_dn = jnp.tile(jnp.concatenate([zero, sin], -1), (1, 2))
    return jnp.stack([cos_t, sin_up, sin_dn], 0)


def kernel(x_prompt, x_sample, c, state_delta, state_ret, c_ctx, w_mod, b_mod, w_in, conv_w, a_log, dt_bias,
           norm_a_w, gn_w, gn_b, w_o, ln1_w, ln1_b, w_ff1, b_ff1, w_ff2, b_ff2, ln2_w, ln2_b):
    assert w_mod.shape[0] == DEPTH == 1
    n_dec = c.shape[0]
    rows = -(-(1 + n_dec) // SUBLANES) * SUBLANES
    cond = jnp.zeros((rows, D_MODEL), F32).at[0].set(c_ctx).at[1:1 + n_dec].set(c)
    mod = _modulation(cond, w_mod[0], b_mod[0])
    mod3 = mod.reshape(rows, 1, 6 * D_MODEL)

    w = w_in[0]
    c_ab = QKV_W + D_A
    w_in_p = (w[:, :c_ab].astype(BF16), w[:, c_ab + 2 * N_AB:].astype(BF16),
              jnp.zeros((D_MODEL, LANES), BF16).at[:, :2 * N_AB].set(w[:, c_ab:c_ab + 2 * N_AB].astype(BF16)))
    convw = jnp.zeros((SUBLANES, QKV_W), F32).at[:CONV_K].set(conv_w[0])
    par = (jnp.zeros((SUBLANES, LANES), F32).at[0, :N_AB].set(a_log[0].reshape(-1))
           .at[1, :N_AB].set(dt_bias[0].reshape(-1)))
    naw = norm_a_w[0].reshape(1, DV_A)
    gnw = gn_w[0].reshape(1, D_B)
    gnb = gn_b[0].reshape(1, D_B)
    wo = w_o[0].astype(BF16)
    w1 = w_ff1[0].astype(BF16)
    w2 = w_ff2[0].astype(BF16)
    row = lambda v: v[0].reshape(1, -1)

    def trunk(x, row0, row_stride, rope, sd0, sr0, emit_state, tm):
        qkv, qkb, vb, gates, gb = _projection(x, mod3, row0, row_stride, w_in_p, convw, par, rope, tm)
        fwd = _mixer(False, qkv, qkb, vb, gb, None, None, None, None, None, sd0, sr0, emit_state)
        bwd = _mixer(True, qkv, qkb, vb, gb, fwd[0], gates, naw, gnw, gnb, sd0, sr0, emit_state)
        o = bwd[0]
        if row_stride == 0:
            y = _post(o.reshape(1, -1, D_A + D_B), x.reshape(1, -1, D_MODEL), mod3, row0, 0, wo, row(ln1_w),
                      row(ln1_b), w1, row(b_ff1), w2, row(b_ff2), row(ln2_w), row(ln2_b), POST_TILE).reshape(x.shape)
        else:
            y = _post(o, x, mod3, row0, row_stride, wo, row(ln1_w), row(ln1_b), w1, row(b_ff1), w2, row(b_ff2),
                      row(ln2_w), row(ln2_b), POST_TILE)
        states = [jnp.concatenate([f, b], 2) for f, b in zip(fwd[1:], bwd[1:])]
        return y, states

    y_prompt, (new_sd, new_sr) = trunk(x_prompt, 0, 0, None, None, None, True, 256)
    y_sample, _ = trunk(x_sample, 1, 1, _rope_tables(x_sample.shape[1]), state_delta, state_ret, False, 512)
    return y_prompt, y_sample, new_sd, new_sr
```

```python
import functools
import math

import jax
import jax.numpy as jnp
import numpy as np
from jax import lax
from jax.experimental import pallas as pl
from jax.experimental.pallas import tpu as pltpu

F32 = jnp.float32
BF16 = jnp.bfloat16

D_MODEL = 1024
H_A, DK_A, DV_A = 4, 128, 128
D_A = H_A * DV_A
CONV_K = 5
CHUNK_A = 64
H_B, DK_B, DV_B = 4, 64, 128
D_B = H_B * DV_B
CHUNK_B = 128
GRID_W = 64
ROPE_BASE = 10000.0
D_FF = 4 * D_MODEL
DEPTH = 1
ALPHA = (2.0 * DEPTH) ** 0.25

LANES = 128
SUBLANES = 8
VMEM_LIMIT = 56 * 1024 * 1024

QKV_W = 3 * D_A
COL_GATE_A = QKV_W
COL_QK_B = COL_GATE_A + D_A
QKB_W = 2 * H_B * DK_B
COL_V_B = COL_QK_B + QKB_W
COL_GATE_B = COL_V_B + D_B
COL_AB = COL_GATE_B + D_B
P_W = COL_AB + LANES
N_AB = 2 * H_A

TIME_BLOCK = 256
HALO = SUBLANES
PROJ_GROUP = 256
assert CONV_K == 5 and (CONV_K - 1) // 2 <= HALO


def _dot(a, b):
    return jnp.dot(a.astype(BF16), b.astype(BF16), preferred_element_type=F32)


def _dot_nt(a, b):
    return lax.dot_general(a.astype(BF16), b.astype(BF16), (((1,), (1,)), ((), ())),
                           preferred_element_type=F32)


def _dot_tn(a, b):
    return lax.dot_general(a.astype(BF16), b.astype(BF16), (((0,), (0,)), ((), ())),
                           preferred_element_type=F32)


def _split3(x):
    hi = x.astype(BF16)
    r = x - hi.astype(F32)
    mid = r.astype(BF16)
    lo = (r - mid.astype(F32)).astype(BF16)
    return hi, mid, lo


def _silu(x):
    h = 0.5 * x
    return h + h * jnp.tanh(h)


def _layer_norm(x, eps):
    mu = jnp.mean(x, -1, keepdims=True)
    xc = x - mu
    var = jnp.mean(xc * xc, -1, keepdims=True)
    return xc * lax.rsqrt(var + eps)


def _resident(shape):
    return pl.BlockSpec(shape, lambda *_: (0,) * len(shape), pipeline_mode=pl.Buffered(1))


def _mod_kernel(c_ref, w_ref, b_ref, o_ref):
    o_ref[...] = _dot(_silu(c_ref[...]), w_ref[...]) + b_ref[...]


def _modulation(cond, w_mod, b_mod):
    rows = cond.shape[0]
    n = w_mod.shape[1]
    bn = 1536
    return pl.pallas_call(
        _mod_kernel,
        grid=(n // bn,),
        in_specs=[pl.BlockSpec((rows, D_MODEL), lambda j: (0, 0)),
                  pl.BlockSpec((D_MODEL, bn), lambda j: (0, j)),
                  pl.BlockSpec((1, bn), lambda j: (0, j))],
        out_specs=pl.BlockSpec((rows, bn), lambda j: (0, j)),
        out_shape=jax.ShapeDtypeStruct((rows, n), F32),
        compiler_params=pltpu.CompilerParams(dimension_semantics=("arbitrary",)),
        name="modulation",
    )(cond, w_mod, b_mod.reshape(1, n))


def _proj_kernel(use_rope, tm, nt, *refs):
    if use_rope:
        (x_ref, xp_ref, xn_ref, mod_ref, w_ref, convw_ref, par_ref, rope_ref,
         qkv_ref, qkb_ref, vb_ref, gates_ref, gb_ref) = refs
    else:
        (x_ref, xp_ref, xn_ref, mod_ref, w_ref, convw_ref, par_ref,
         qkv_ref, qkb_ref, vb_ref, gates_ref, gb_ref) = refs
    i = pl.program_id(1)
    m = mod_ref[0]
    sh1 = m[:, 0:D_MODEL]
    sc1 = m[:, D_MODEL:2 * D_MODEL]

    def modulated(x):
        return _layer_norm(x, 1e-6) * (1.0 + sc1) + sh1

    hb = jnp.concatenate([modulated(xp_ref[0]), modulated(x_ref[0]), modulated(xn_ref[0])], 0).astype(BF16)
    rows = tm + 2 * HALO

    def project(c0, width):
        return jnp.dot(hb, w_ref[:, c0:c0 + width], preferred_element_type=F32)

    def conv_silu(pc, c0):
        xe = jnp.concatenate([jnp.where(i > 0, pc[0:HALO], 0.0), pc[HALO:HALO + tm],
                              jnp.where(i < nt - 1, pc[HALO + tm:], 0.0)], 0)
        z = [xe * convw_ref[kk:kk + 1, c0:c0 + pc.shape[1]] for kk in range(CONV_K)]
        back = pltpu.roll(z[1] + pltpu.roll(z[0], 1, 0), 1, 0)
        ahead = pltpu.roll(z[3] + pltpu.roll(z[4], rows - 1, 0), rows - 1, 0)
        return _silu((z[2] + back + ahead)[HALO:HALO + tm])

    def l2n(x):
        return x * lax.rsqrt(jnp.sum(x * x, -1, keepdims=True) + 1e-6)

    def finish_qkv(pc, c0):
        y = conv_silu(pc, c0)
        for hh in range(pc.shape[1] // DK_A):
            c = c0 + hh * DK_A
            x = y[:, hh * DK_A:(hh + 1) * DK_A]
            if c < D_A:
                x = l2n(x) * (DK_A ** -0.5)
            elif c < 2 * D_A:
                x = l2n(x)
            qkv_ref[0, :, c:c + DK_A] = x

    def finish_rest(pc, c0):
        pc = pc[HALO:HALO + tm]
        w = pc.shape[1]
        if c0 < COL_QK_B:
            gates_ref[0, :, c0 - COL_GATE_A:c0 - COL_GATE_A + w] = _silu(pc)
        elif c0 < COL_V_B:
            for hh in range(w // LANES):
                x = pc[:, hh * LANES:(hh + 1) * LANES]
                if use_rope:
                    x = (x * rope_ref[0] + pltpu.roll(x, LANES - DK_B // 2, 1) * rope_ref[1]
                         + pltpu.roll(x, DK_B // 2, 1) * rope_ref[2])
                c = c0 - COL_QK_B + hh * LANES
                if c < H_B * DK_B:
                    x = x * (DK_B ** -0.5)
                qkb_ref[0, :, c:c + LANES] = x
        elif c0 < COL_GATE_B:
            vb_ref[0, :, c0 - COL_V_B:c0 - COL_V_B + w] = pc.astype(BF16)
        elif c0 < COL_AB:
            gates_ref[0, :, D_A + c0 - COL_GATE_B:D_A + c0 - COL_GATE_B + w] = _silu(pc)
        else:
            z = pc + par_ref[1:2, :]
            softplus = jnp.maximum(z, 0.0) + jnp.log1p(jnp.exp(-jnp.abs(z)))
            g_all = -jnp.exp(par_ref[0:1, :]) * softplus
            beta_all = 1.0 / (1.0 + jnp.exp(-pc))
            lane = lax.broadcasted_iota(jnp.int32, (tm, LANES), 1)
            gb_ref[0] = jnp.where(lane < N_AB, g_all, jnp.where(lane < 2 * N_AB, beta_all, 0.0))

    heavy = [(finish_qkv, c) for c in range(0, QKV_W, PROJ_GROUP)]
    light = [(finish_rest, c) for c in range(QKV_W, P_W, PROJ_GROUP)]
    order = []
    while heavy or light:
        if heavy:
            order.append(heavy.pop(0))
        if light:
            order.append(light.pop(0))
    previous = None
    for fn, c0 in order:
        pc = project(c0, min(PROJ_GROUP, P_W - c0))
        if previous is not None:
            previous[0](previous[1], previous[2])
        previous = (fn, pc, c0)
    previous[0](previous[1], previous[2])


def _projection(x, mod3, row0, row_stride, w_in_p, convw, par, rope, tm):
    bt, t, _ = x.shape
    nt = t // tm
    hpt = tm // HALO
    nh = t // HALO
    use_rope = rope is not None
    in_specs = [pl.BlockSpec((1, tm, D_MODEL), lambda b, i: (b, i, 0)),
                pl.BlockSpec((1, HALO, D_MODEL), lambda b, i: (b, jnp.maximum(i * hpt - 1, 0), 0)),
                pl.BlockSpec((1, HALO, D_MODEL), lambda b, i: (b, jnp.minimum((i + 1) * hpt, nh - 1), 0)),
                pl.BlockSpec((1, 1, 6 * D_MODEL), lambda b, i: (row0 + row_stride * b, 0, 0)),
                _resident((D_MODEL, P_W)),
                _resident((SUBLANES, QKV_W)),
                _resident((SUBLANES, LANES))]
    args = [x, x, x, mod3, w_in_p, convw, par]
    if use_rope:
        in_specs.append(pl.BlockSpec((3, tm, LANES), lambda b, i: (0, i, 0)))
        args.append(rope)
    widths = (QKV_W, QKB_W, D_B, D_A + D_B, LANES)
    dtypes = (F32, F32, BF16, F32, F32)
    return pl.pallas_call(
        functools.partial(_proj_kernel, use_rope, tm, nt),
        grid=(bt, nt),
        in_specs=in_specs,
        out_specs=[pl.BlockSpec((1, tm, w), lambda b, i: (b, i, 0)) for w in widths],
        out_shape=[jax.ShapeDtypeStruct((bt, t, w), dt) for w, dt in zip(widths, dtypes)],
        compiler_params=pltpu.CompilerParams(dimension_semantics=("arbitrary", "arbitrary"),
                                             vmem_limit_bytes=VMEM_LIMIT),
        name="projection",
    )(*args)


POST_SPLIT = 2
POST_TILE = 512


def _post_kernel(o_ref, x_ref, mod_ref, wo_ref, ln1w_ref, ln1b_ref, w1_ref, b1_ref, w2_ref, b2_ref,
                 ln2w_ref, ln2b_ref, y_ref):
    m = mod_ref[0]
    g1 = m[:, 2 * D_MODEL:3 * D_MODEL]
    sh2 = m[:, 3 * D_MODEL:4 * D_MODEL]
    sc2 = m[:, 4 * D_MODEL:5 * D_MODEL]
    g2 = m[:, 5 * D_MODEL:6 * D_MODEL]
    rows = o_ref.shape[1] // POST_SPLIT
    parts = [slice(r * rows, (r + 1) * rows) for r in range(POST_SPLIT)]
    ys = [jnp.dot(o_ref[0, sl], wo_ref[...], preferred_element_type=F32) for sl in parts]
    x1s = [_layer_norm(ALPHA * x_ref[0, sl] + g1 * y, 1e-6) * ln1w_ref[...] + ln1b_ref[...]
           for sl, y in zip(parts, ys)]
    hs = [(_layer_norm(x1, 1e-6) * (1.0 + sc2) + sh2).astype(BF16) for x1 in x1s]
    acts = [jnp.square(jnp.maximum(jnp.dot(h, w1_ref[...], preferred_element_type=F32) + b1_ref[...], 0.0))
            .astype(BF16) for h in hs]
    fs = [jnp.dot(a, w2_ref[...], preferred_element_type=F32) + b2_ref[...] for a in acts]
    for sl, x1, f in zip(parts, x1s, fs):
        y_ref[0, sl] = _layer_norm(ALPHA * x1 + g2 * f, 1e-6) * ln2w_ref[...] + ln2b_ref[...]


def _post(o, x, mod3, row0, row_stride, wo, ln1w, ln1b, w1, b1, w2, b2, ln2w, ln2b, tm):
    bt, t, _ = x.shape
    return pl.pallas_call(
        _post_kernel,
        grid=(bt, t // tm),
        in_specs=[pl.BlockSpec((1, tm, D_MODEL), lambda b, i: (b, i, 0)),
                  pl.BlockSpec((1, tm, D_MODEL), lambda b, i: (b, i, 0)),
                  pl.BlockSpec((1, 1, 6 * D_MODEL), lambda b, i: (row0 + row_stride * b, 0, 0)),
                  _resident((D_MODEL, D_MODEL)),
                  _resident((1, D_MODEL)),
                  _resident((1, D_MODEL)),
                  _resident((D_MODEL, D_FF)),
                  _resident((1, D_FF)),
                  _resident((D_FF, D_MODEL)),
                  _resident((1, D_MODEL)),
                  _resident((1, D_MODEL)),
                  _resident((1, D_MODEL))],
        out_specs=pl.BlockSpec((1, tm, D_MODEL), lambda b, i: (b, i, 0)),
        out_shape=jax.ShapeDtypeStruct((bt, t, D_MODEL), F32),
        compiler_params=pltpu.CompilerParams(dimension_semantics=("arbitrary", "arbitrary"),
                                             vmem_limit_bytes=VMEM_LIMIT),
        name="post",
    )(o, x, mod3, wo, ln1w, ln1b, w1, b1, w2, b2, ln2w, ln2b)


_LOG_GAMMA = [math.log1p(-(2.0 ** (-5.0 - h))) for h in range(H_B)]

SOLVE_BASE = 8
SEQ_PER_STEP = 4
CAT_W = H_A * CHUNK_A

def _head_blockdiag(x, mask):
    return jnp.where(mask, jnp.concatenate([x] * H_A, 0), 0.0).astype(BF16)


def _cat_inverses(ms, bd_mask):
    i = lax.broadcasted_iota(jnp.int32, (CHUNK_A, CAT_W), 0)
    j = lax.broadcasted_iota(jnp.int32, (CHUNK_A, CAT_W), 1) % CHUNK_A
    same = lambda s: (i // s) == (j // s)
    eye = jnp.where(i == j, 1.0, 0.0)
    mul = lambda a, bd: jnp.dot(a.astype(BF16), bd, preferred_element_type=F32)
    ps = [jnp.where(same(SOLVE_BASE), -m, 0.0) for m in ms]
    ts = [eye + p for p in ps]
    bds = [_head_blockdiag(p, bd_mask) for p in ps]
    for _ in range(int(math.log2(SOLVE_BASE)) - 1):
        ps = [mul(p, bd) for p, bd in zip(ps, bds)]
        bds = [_head_blockdiag(p, bd_mask) for p in ps]
        ts = [t + mul(t, bd) for t, bd in zip(ts, bds)]
    s = SOLVE_BASE
    while s < CHUNK_A:
        sel = same(2 * s) & jnp.logical_not(same(s))
        bds = [_head_blockdiag(t, bd_mask) for t in ts]
        ys = [mul(jnp.where(sel, m, 0.0), bd) for m, bd in zip(ms, bds)]
        bds = [_head_blockdiag(y, bd_mask) for y in ys]
        ts = [t - mul(t, bd) for t, bd in zip(ts, bds)]
        s *= 2
    return ts


def _mixer_kernel(rev, has_init, emit_state, nblk, *refs):
    NS = SEQ_PER_STEP
    TB = TIME_BLOCK
    d = 1 if rev else 0
    j = pl.program_id(1)
    qkv_ref, qkb_ref, vb_ref, gb_ref = refs[:4]
    pos_ = 4
    if rev:
        of_ref, gates_ref, naw_ref, gnw_ref, gnb_ref = refs[pos_:pos_ + 5]
        pos_ += 5
    if has_init:
        sd0_ref, sr0_ref = refs[pos_:pos_ + 2]
        pos_ += 2
    o_ref = refs[pos_]
    pos_ += 1
    if emit_state:
        sd_ref, sr_ref = refs[pos_:pos_ + 2]
        pos_ += 2
    sa_scr, sr_scr = refs[pos_:pos_ + 2]

    @pl.when(j == 0)
    def _():
        if has_init:
            for bb in range(NS):
                for h in range(H_A):
                    sa_scr[bb, :, h * DV_A:(h + 1) * DV_A] = sd0_ref[bb, 0, 0, h]
                sr_scr[bb] = jnp.zeros(sr_scr.shape[1:], F32)
                for h in range(H_B):
                    lo = (h % 2) * DK_B
                    sr_scr[bb, h // 2, lo:lo + DK_B, (h % 2) * DV_B:(h % 2 + 1) * DV_B] = sr0_ref[bb, 0, 0, h]
        else:
            sa_scr[...] = jnp.zeros(sa_scr.shape, F32)
            sr_scr[...] = jnp.zeros(sr_scr.shape, F32)

    ri = lax.broadcasted_iota(jnp.int32, (TB, TB), 0)
    ci = lax.broadcasted_iota(jnp.int32, (TB, TB), 1)
    cum = jnp.where(((ri // CHUNK_A) == (ci // CHUNK_A)) & ((ci >= ri) if rev else (ci <= ri)),
                    1.0, 0.0).astype(BF16)
    er = lax.broadcasted_iota(jnp.int32, (LANES, CAT_W), 0)
    ec = lax.broadcasted_iota(jnp.int32, (LANES, CAT_W), 1)
    esel_c = jnp.where(er == d * H_A + ec // CHUNK_A, 1.0, 0.0).astype(BF16)
    lane_tb = lax.broadcasted_iota(jnp.int32, (TB, LANES), 1)
    i64 = lax.broadcasted_iota(jnp.int32, (CHUNK_A, CAT_W), 0)
    j64 = lax.broadcasted_iota(jnp.int32, (CHUNK_A, CAT_W), 1) % CHUNK_A
    tri = (i64 <= j64) if rev else (i64 >= j64)
    strict = (i64 < j64) if rev else (i64 > j64)
    bd_mask = (lax.broadcasted_iota(jnp.int32, (CAT_W, CAT_W), 0) // CHUNK_A
               == lax.broadcasted_iota(jnp.int32, (CAT_W, CAT_W), 1) // CHUNK_A)
    bdk_mask = (lax.broadcasted_iota(jnp.int32, (CAT_W, D_A), 0) // CHUNK_A
                == lax.broadcasted_iota(jnp.int32, (CAT_W, D_A), 1) // DK_A)
    lane_head = lax.broadcasted_iota(jnp.int32, (CHUNK_A, CAT_W), 1) // CHUNK_A
    nca = TB // CHUNK_A
    order_a = list(range(nca - 1, -1, -1)) if rev else list(range(nca))

    st = []
    for bb in range(NS):
        gbk = gb_ref[bb]
        gc = None
        for part in _split3(gbk):
            t_ = jnp.dot(cum, part, preferred_element_type=F32)
            gc = t_ if gc is None else gc + t_
        gct = None
        for part in _split3(gbk.T):
            t_ = lax.dot_general(part, cum, (((1,), (1,)), ((), ())), preferred_element_type=F32)
            gct = t_ if gct is None else gct + t_
        gcb = jnp.where(lane_tb < N_AB, gc, gbk)
        gcs = None
        for part in _split3(gcb):
            t_ = jnp.dot(part, esel_c, preferred_element_type=F32)
            gcs = t_ if gcs is None else gcs + t_
        gc5a = jnp.concatenate([jnp.broadcast_to(gcb[:, d * H_A + h:d * H_A + h + 1], (TB, DK_A))
                                for h in range(H_A)], 1)
        b5a = jnp.concatenate([jnp.broadcast_to(gcb[:, N_AB + d * H_A + h:N_AB + d * H_A + h + 1], (TB, DK_A))
                               for h in range(H_A)], 1)
        for n in order_a:
            r0 = n * CHUNK_A
            last = r0 if rev else r0 + CHUNK_A - 1
            it = dict(bb=bb, r0=r0, gc5=gc5a[r0:r0 + CHUNK_A], tot5=gc5a[last:last + 1], b5=b5a[r0:r0 + CHUNK_A])
            q = qkv_ref[bb, r0:r0 + CHUNK_A, 0:D_A]
            k = qkv_ref[bb, r0:r0 + CHUNK_A, D_A:2 * D_A]
            gcc = gcs[r0:r0 + CHUNK_A]
            gcr = jnp.concatenate([gct[d * H_A + h:d * H_A + h + 1, r0:r0 + CHUNK_A] for h in range(H_A)], 1)
            decay = jnp.where(tri, jnp.exp(jnp.where(tri, gcc - gcr, 0.0)), 0.0)
            mq = _dot_nt(jnp.concatenate([k * it["b5"], q], 0), _head_blockdiag(k, bdk_mask))
            it["m"] = jnp.where(strict, mq[:CHUNK_A] * decay, 0.0)
            it["qk"] = jnp.where(tri, mq[CHUNK_A:] * decay, 0.0)
            st.append(it)
    lane = lax.broadcasted_iota(jnp.int32, (CHUNK_B, LANES), 1)
    pi = lax.broadcasted_iota(jnp.int32, (CHUNK_B, CHUNK_B), 0)
    pj = lax.broadcasted_iota(jnp.int32, (CHUNK_B, CHUNK_B), 1)
    pos = lax.broadcasted_iota(jnp.int32, (CHUNK_B, 1), 0)
    if rev:
        pi, pj, pos = CHUNK_B - 1 - pi, CHUNK_B - 1 - pj, CHUNK_B - 1 - pos
    pdiff = (pi - pj).astype(F32)
    posf = pos.astype(F32)
    first_half = (lane // DK_B) == 0
    own_block = (lax.broadcasted_iota(jnp.int32, (LANES, 2 * DV_B), 0) // DK_B
                 == lax.broadcasted_iota(jnp.int32, (LANES, 2 * DV_B), 1) // DV_B)
    ncb = TB // CHUNK_B
    order_b = range(ncb - 1, -1, -1) if rev else range(ncb)
    rt = []
    for pp in range(H_B // 2):
        lgs = [_LOG_GAMMA[H_B - 1 - h] if rev else _LOG_GAMMA[h] for h in (2 * pp, 2 * pp + 1)]
        d_mask = jnp.concatenate([jnp.where(pdiff >= 0, jnp.exp(lg * jnp.maximum(pdiff, 0.0)), 0.0) for lg in lgs], 1)
        xi = jnp.where(first_half, jnp.exp(lgs[0] * (posf + 1.0)), jnp.exp(lgs[1] * (posf + 1.0)))
        zeta = jnp.where(first_half, jnp.exp(lgs[0] * (CHUNK_B - 1.0 - posf)), jnp.exp(lgs[1] * (CHUNK_B - 1.0 - posf)))
        col_head = lax.broadcasted_iota(jnp.int32, (1, 2 * DV_B), 1) // DV_B
        g_chunk = jnp.where(col_head == 0, math.exp(lgs[0] * CHUNK_B), math.exp(lgs[1] * CHUNK_B))
        for bb in range(NS):
            for n in order_b:
                rt.append(dict(bb=bb, r0=n * CHUNK_B, pp=pp, g_chunk=g_chunk, d_mask=d_mask, xi=xi, zeta=zeta))

    def retention_independent(it):
        bb, r0, pp = it["bb"], it["r0"], it["pp"]
        q = qkb_ref[bb, r0:r0 + CHUNK_B, pp * LANES:(pp + 1) * LANES]
        k = qkb_ref[bb, r0:r0 + CHUNK_B, H_B * DK_B + pp * LANES:H_B * DK_B + (pp + 1) * LANES]
        v0 = vb_ref[bb, r0:r0 + CHUNK_B, 2 * pp * DV_B:(2 * pp + 1) * DV_B]
        v1 = vb_ref[bb, r0:r0 + CHUNK_B, (2 * pp + 1) * DV_B:(2 * pp + 2) * DV_B]
        zb = jnp.zeros((CHUNK_B, DV_B), BF16)
        ksplit = jnp.concatenate([jnp.where(first_half, k, 0.0), jnp.where(first_half, 0.0, k)], 0)
        scores = _dot_nt(q, ksplit) * it["d_mask"]
        v_bd = jnp.concatenate([jnp.concatenate([v0, zb], 1), jnp.concatenate([zb, v1], 1)], 0)
        it["inner"] = jnp.dot(scores.astype(BF16), v_bd, preferred_element_type=F32)
        it["qx"] = q * it["xi"]
        ds = lax.dot_general((k * it["zeta"]).astype(BF16), jnp.concatenate([v0, v1], 1),
                             (((0,), (0,)), ((), ())), preferred_element_type=F32)
        it["ds"] = jnp.where(own_block, ds, 0.0)

    fillers = [functools.partial(retention_independent, it) for it in rt]
    inv = _cat_inverses([it["m"] for it in st], bd_mask)
    for it, t in zip(st, inv):
        bb, r0 = it["bb"], it["r0"]
        k = qkv_ref[bb, r0:r0 + CHUNK_A, D_A:2 * D_A]
        v = qkv_ref[bb, r0:r0 + CHUNK_A, 2 * D_A:3 * D_A]
        vb5 = v * it["b5"]
        kbe5 = (k * it["b5"]) * jnp.exp(it["gc5"])
        rhs = jnp.concatenate(
            [jnp.concatenate([vb5[:, h * DV_A:(h + 1) * DV_A], kbe5[:, h * DK_A:(h + 1) * DK_A]], 1)
             for h in range(H_A)], 0).astype(BF16)
        lhs = jnp.concatenate([jnp.where(lane_head == h, t, 0.0) for h in range(H_A)], 0)
        sol = jnp.dot(lhs.astype(BF16), rhs, preferred_element_type=F32)
        it["u"] = jnp.concatenate([sol[h * CHUNK_A:(h + 1) * CHUNK_A, :DV_A] for h in range(H_A)], 1)
        it["w"] = jnp.concatenate([sol[h * CHUNK_A:(h + 1) * CHUNK_A, DV_A:] for h in range(H_A)], 1)

    def finish(bb, r0, rows, c0, width, o):
        if not rev:
            o_ref[bb, r0:r0 + rows, c0:c0 + width] = o
            return
        o = o + of_ref[bb, r0:r0 + rows, c0:c0 + width]
        outs = []
        for hh in range(width // LANES):
            x = o[:, hh * LANES:(hh + 1) * LANES]
            if c0 < D_A:
                x = x * lax.rsqrt(jnp.mean(x * x, -1, keepdims=True) + 1e-6) * naw_ref[...]
            else:
                cb = c0 - D_A + hh * LANES
                x = _layer_norm(x, 1e-5) * gnw_ref[:, cb:cb + LANES] + gnb_ref[:, cb:cb + LANES]
            outs.append(x)
        o = outs[0] if len(outs) == 1 else jnp.concatenate(outs, 1)
        o_ref[bb, r0:r0 + rows, c0:c0 + width] = (o * gates_ref[bb, r0:r0 + rows, c0:c0 + width]).astype(BF16)

    per_step = -(-len(fillers) // nca)
    for c in range(nca):
        grp_items = [st[bb * nca + c] for bb in range(NS)]
        s_old = [sa_scr[it["bb"]] for it in grp_items]
        prods = []
        for it, s in zip(grp_items, s_old):
            q_dec = qkv_ref[it["bb"], it["r0"]:it["r0"] + CHUNK_A, 0:D_A] * jnp.exp(it["gc5"])
            pr = []
            for pp in range(H_A // 2):
                lo = pp * 2 * DV_A
                z = jnp.zeros((DK_A, DV_A), F32)
                bds = jnp.concatenate([jnp.concatenate([s[:, lo:lo + DV_A], z], 1),
                                       jnp.concatenate([z, s[:, lo + DV_A:lo + 2 * DV_A]], 1)], 0)
                lhs = jnp.concatenate([it["w"][:, lo:lo + 2 * DK_A], q_dec[:, lo:lo + 2 * DK_A]], 0)
                pr.append(_dot(lhs, bds))
            prods.append(jnp.concatenate(pr, 1))
        for fill in fillers[c * per_step:(c + 1) * per_step]:
            fill()
        v_new = [it["u"] - pr[:CHUNK_A] for it, pr in zip(grp_items, prods)]
        vbd = [_head_blockdiag(v, bdk_mask) for v in v_new]
        upd = []
        for it, vb_ in zip(grp_items, vbd):
            k_dec = (qkv_ref[it["bb"], it["r0"]:it["r0"] + CHUNK_A, D_A:2 * D_A]
                     * jnp.exp(it["tot5"] - it["gc5"]))
            kd_stack = jnp.concatenate([k_dec[:, h * DK_A:(h + 1) * DK_A] for h in range(H_A)], 0)
            upd.append(lax.dot_general(kd_stack.astype(BF16), vb_, (((0,), (0,)), ((), ())),
                                       preferred_element_type=F32))
        for it, s, x in zip(grp_items, s_old, upd):
            sa_scr[it["bb"]] = s * jnp.exp(it["tot5"]) + x
        for it, pr, vb_ in zip(grp_items, prods, vbd):
            o = pr[CHUNK_A:] + jnp.dot(it["qk"].astype(BF16), vb_, preferred_element_type=F32)
            finish(it["bb"], it["r0"], CHUNK_A, 0, D_A, o)

    for it in rt:
        bb, pp = it["bb"], it["pp"]
        s = sr_scr[bb, pp]
        o = it["inner"] + _dot(it["qx"], s)
        sr_scr[bb, pp] = s * it["g_chunk"] + it["ds"]
        finish(bb, it["r0"], CHUNK_B, D_A + 2 * pp * DV_B, 2 * DV_B, o)

    if emit_state:
        @pl.when(j == nblk - 1)
        def _():
            for bb in range(NS):
                for h in range(H_A):
                    sd_ref[bb, 0, 0, h] = sa_scr[bb, :, h * DV_A:(h + 1) * DV_A]
                for h in range(H_B):
                    lo = (h % 2) * DK_B
                    sr_ref[bb, 0, 0, h] = sr_scr[bb, h // 2, lo:lo + DK_B, (h % 2) * DV_B:(h % 2 + 1) * DV_B]


def _mixer(rev, qkv, qkb, vb, gb, o_fwd, gates, naw, gnw, gnb, sd0, sr0, emit_state):
    bt, t, _ = qkv.shape
    NS = SEQ_PER_STEP
    TB = TIME_BLOCK
    nblk = t // TB
    has_init = sd0 is not None
    d = 1 if rev else 0
    tb_of = (lambda j: nblk - 1 - j) if rev else (lambda j: j)
    tok = lambda w: pl.BlockSpec((NS, TB, w), lambda b, j: (b, tb_of(j), 0))
    in_specs = [tok(QKV_W), tok(QKB_W), tok(D_B), tok(LANES)]
    args = [qkv, qkb, vb, gb]
    if rev:
        in_specs += [tok(D_A + D_B), tok(D_A + D_B), _resident((1, DV_A)), _resident((1, D_B)), _resident((1, D_B))]
        args += [o_fwd, gates, naw, gnw, gnb]
    if has_init:
        in_specs.append(pl.BlockSpec((NS, 1, 1, H_A, DK_A, DV_A), lambda b, j: (b, 0, d, 0, 0, 0)))
        in_specs.append(pl.BlockSpec((NS, 1, 1, H_B, DK_B, DV_B), lambda b, j: (b, 0, d, 0, 0, 0)))
        args += [sd0, sr0]
    out_specs = [tok(D_A + D_B)]
    out_shape = [jax.ShapeDtypeStruct((bt, t, D_A + D_B), BF16 if rev else F32)]
    if emit_state:
        out_specs.append(pl.BlockSpec((NS, 1, 1, H_A, DK_A, DV_A), lambda b, j: (b, 0, 0, 0, 0, 0)))
        out_specs.append(pl.BlockSpec((NS, 1, 1, H_B, DK_B, DV_B), lambda b, j: (b, 0, 0, 0, 0, 0)))
        out_shape.append(jax.ShapeDtypeStruct((bt, DEPTH, 1, H_A, DK_A, DV_A), F32))
        out_shape.append(jax.ShapeDtypeStruct((bt, DEPTH, 1, H_B, DK_B, DV_B), F32))
    scratch = [pltpu.VMEM((NS, DK_A, H_A * DV_A), F32),
               pltpu.VMEM((NS, H_B // 2, LANES, 2 * DV_B), F32)]
    return pl.pallas_call(
        functools.partial(_mixer_kernel, rev, has_init, emit_state, nblk),
        grid=(bt // NS, nblk),
        in_specs=in_specs,
        out_specs=out_specs,
        out_shape=out_shape,
        scratch_shapes=scratch,
        compiler_params=pltpu.CompilerParams(dimension_semantics=("arbitrary", "arbitrary"),
                                             vmem_limit_bytes=VMEM_LIMIT),
        name="mixer_bwd" if rev else "mixer_fwd",
    )(*args)


def _rope_tables(t):
    rows = t // GRID_W
    r = np.repeat(np.arange(rows, dtype=np.float32), GRID_W)
    col = np.tile(np.arange(GRID_W, dtype=np.float32), rows)
    nf = DK_B // 4
    inv = (np.float32(ROPE_BASE) ** (-np.arange(nf, dtype=np.float32) / np.float32(nf))).astype(np.float32)
    ang = jnp.asarray(np.concatenate([r[:, None] * inv, col[:, None] * inv], -1).astype(np.float32))
    cos, sin = jnp.cos(ang), jnp.sin(ang)
    zero = jnp.zeros_like(sin)
    cos_t = jnp.tile(cos, (1, 4))
    sin_up = jnp.tile(jnp.concatenate([-sin, zero], -1), (1, 2))
    sin_dn = jnp.tile(jnp.concatenate([zero, sin], -1), (1, 2))
    return jnp.stack([cos_t, sin_up, sin_dn], 0)


def kernel(x_prompt, x_sample, c, state_delta, state_ret, c_ctx, w_mod, b_mod, w_in, conv_w, a_log, dt_bias,
           norm_a_w, gn_w, gn_b, w_o, ln1_w, ln1_b, w_ff1, b_ff1, w_ff2, b_ff2, ln2_w, ln2_b):
    assert w_mod.shape[0] == DEPTH == 1
    n_dec = c.shape[0]
    rows = -(-(1 + n_dec) // SUBLANES) * SUBLANES
    cond = jnp.zeros((rows, D_MODEL), F32).at[0].set(c_ctx).at[1:1 + n_dec].set(c)
    mod = _modulation(cond, w_mod[0], b_mod[0])
    mod3 = mod.reshape(rows, 1, 6 * D_MODEL)

    w = w_in[0]
    c_ab = QKV_W + D_A
    w_in_p = (jnp.zeros((D_MODEL, P_W), BF16)
              .at[:, :c_ab].set(w[:, :c_ab].astype(BF16))
              .at[:, c_ab:COL_AB].set(w[:, c_ab + 2 * N_AB:].astype(BF16))
              .at[:, COL_AB:COL_AB + 2 * N_AB].set(w[:, c_ab:c_ab + 2 * N_AB].astype(BF16)))
    convw = jnp.zeros((SUBLANES, QKV_W), F32).at[:CONV_K].set(conv_w[0])
    par = (jnp.zeros((SUBLANES, LANES), F32).at[0, :N_AB].set(a_log[0].reshape(-1))
           .at[1, :N_AB].set(dt_bias[0].reshape(-1)))
    naw = norm_a_w[0].reshape(1, DV_A)
    gnw = gn_w[0].reshape(1, D_B)
    gnb = gn_b[0].reshape(1, D_B)
    wo = w_o[0].astype(BF16)
    w1 = w_ff1[0].astype(BF16)
    w2 = w_ff2[0].astype(BF16)
    row = lambda v: v[0].reshape(1, -1)

    def trunk(x, row0, row_stride, rope, sd0, sr0, emit_state, tm):
        qkv, qkb, vb, gates, gb = _projection(x, mod3, row0, row_stride, w_in_p, convw, par, rope, tm)
        fwd = _mixer(False, qkv, qkb, vb, gb, None, None, None, None, None, sd0, sr0, emit_state)
        bwd = _mixer(True, qkv, qkb, vb, gb, fwd[0], gates, naw, gnw, gnb, sd0, sr0, emit_state)
        o = bwd[0]
        if row_stride == 0:
            y = _post(o.reshape(1, -1, D_A + D_B), x.reshape(1, -1, D_MODEL), mod3, row0, 0, wo, row(ln1_w),
                      row(ln1_b), w1, row(b_ff1), w2, row(b_ff2), row(ln2_w), row(ln2_b), POST_TILE).reshape(x.shape)
        else:
            y = _post(o, x, mod3, row0, row_stride, wo, row(ln1_w), row(ln1_b), w1, row(b_ff1), w2, row(b_ff2),
                      row(ln2_w), row(ln2_b), POST_TILE)
        states = [jnp.concatenate([f, b], 2) for f, b in zip(fwd[1:], bwd[1:])]
        return y, states

    y_prompt, (new_sd, new_sr) = trunk(x_prompt, 0, 0, None, None, None, True, 256)
    y_sample, _ = trunk(x_sample, 1, 1, _rope_tables(x_sample.shape[1]), state_delta, state_ret, False, 512)
    return y_prompt, y_sample, new_sd, new_sr
```

```python
import functools
import math

import jax
import jax.numpy as jnp
import numpy as np
from jax import lax
from jax.experimental import pallas as pl
from jax.experimental.pallas import tpu as pltpu

F32 = jnp.float32
BF16 = jnp.bfloat16

D_MODEL = 1024
H_A, DK_A, DV_A = 4, 128, 128
D_A = H_A * DV_A
CONV_K = 5
CHUNK_A = 64
H_B, DK_B, DV_B = 4, 64, 128
D_B = H_B * DV_B
CHUNK_B = 128
GRID_W = 64
ROPE_BASE = 10000.0
D_FF = 4 * D_MODEL
DEPTH = 1
ALPHA = (2.0 * DEPTH) ** 0.25

LANES = 128
SUBLANES = 8
VMEM_LIMIT = 56 * 1024 * 1024

QKV_W = 3 * D_A
COL_GATE_A = QKV_W
COL_QK_B = COL_GATE_A + D_A
QKB_W = 2 * H_B * DK_B
COL_V_B = COL_QK_B + QKB_W
COL_GATE_B = COL_V_B + D_B
COL_AB = COL_GATE_B + D_B
P_W = COL_AB + LANES
N_AB = 2 * H_A

TIME_BLOCK = 256
HALO = SUBLANES
PROJ_GROUP = 256
assert CONV_K == 5 and (CONV_K - 1) // 2 <= HALO


def _dot(a, b):
    return jnp.dot(a.astype(BF16), b.astype(BF16), preferred_element_type=F32)


def _dot_nt(a, b):
    return lax.dot_general(a.astype(BF16), b.astype(BF16), (((1,), (1,)), ((), ())),
                           preferred_element_type=F32)


def _dot_tn(a, b):
    return lax.dot_general(a.astype(BF16), b.astype(BF16), (((0,), (0,)), ((), ())),
                           preferred_element_type=F32)


def _split3(x):
    hi = x.astype(BF16)
    r = x - hi.astype(F32)
    mid = r.astype(BF16)
    lo = (r - mid.astype(F32)).astype(BF16)
    return hi, mid, lo


def _silu(x):
    h = 0.5 * x
    return h + h * jnp.tanh(h)


def _layer_norm(x, eps):
    mu = jnp.mean(x, -1, keepdims=True)
    xc = x - mu
    var = jnp.mean(xc * xc, -1, keepdims=True)
    return xc * lax.rsqrt(var + eps)


def _resident(shape):
    return pl.BlockSpec(shape, lambda *_: (0,) * len(shape), pipeline_mode=pl.Buffered(1))


def _mod_kernel(c_ref, w_ref, b_ref, o_ref):
    o_ref[...] = _dot(_silu(c_ref[...]), w_ref[...]) + b_ref[...]


def _modulation(cond, w_mod, b_mod):
    rows = cond.shape[0]
    n = w_mod.shape[1]
    bn = 1536
    return pl.pallas_call(
        _mod_kernel,
        grid=(n // bn,),
        in_specs=[pl.BlockSpec((rows, D_MODEL), lambda j: (0, 0)),
                  pl.BlockSpec((D_MODEL, bn), lambda j: (0, j)),
                  pl.BlockSpec((1, bn), lambda j: (0, j))],
        out_specs=pl.BlockSpec((rows, bn), lambda j: (0, j)),
        out_shape=jax.ShapeDtypeStruct((rows, n), F32),
        compiler_params=pltpu.CompilerParams(dimension_semantics=("arbitrary",)),
        name="modulation",
    )(cond, w_mod, b_mod.reshape(1, n))


def _proj_kernel(use_rope, tm, nt, *refs):
    if use_rope:
        (x_ref, xp_ref, xn_ref, mod_ref, w_ref, convw_ref, par_ref, rope_ref,
         qkv_ref, qkb_ref, vb_ref, gates_ref, gb_ref) = refs
    else:
        (x_ref, xp_ref, xn_ref, mod_ref, w_ref, convw_ref, par_ref,
         qkv_ref, qkb_ref, vb_ref, gates_ref, gb_ref) = refs
    i = pl.program_id(1)
    m = mod_ref[0]
    sh1 = m[:, 0:D_MODEL]
    sc1 = m[:, D_MODEL:2 * D_MODEL]

    def modulated(x):
        return _layer_norm(x, 1e-6) * (1.0 + sc1) + sh1

    hb = jnp.concatenate([modulated(xp_ref[0]), modulated(x_ref[0]), modulated(xn_ref[0])], 0).astype(BF16)
    rows = tm + 2 * HALO

    def project(c0, width):
        return jnp.dot(hb, w_ref[:, c0:c0 + width], preferred_element_type=F32)

    def conv_silu(pc, c0):
        xe = jnp.concatenate([jnp.where(i > 0, pc[0:HALO], 0.0), pc[HALO:HALO + tm],
                              jnp.where(i < nt - 1, pc[HALO + tm:], 0.0)], 0)
        z = [xe * convw_ref[kk:kk + 1, c0:c0 + pc.shape[1]] for kk in range(CONV_K)]
        back = pltpu.roll(z[1] + pltpu.roll(z[0], 1, 0), 1, 0)
        ahead = pltpu.roll(z[3] + pltpu.roll(z[4], rows - 1, 0), rows - 1, 0)
        return _silu((z[2] + back + ahead)[HALO:HALO + tm])

    def l2n(x):
        return x * lax.rsqrt(jnp.sum(x * x, -1, keepdims=True) + 1e-6)

    def finish_qkv(pc, c0):
        y = conv_silu(pc, c0)
        for hh in range(pc.shape[1] // DK_A):
            c = c0 + hh * DK_A
            x = y[:, hh * DK_A:(hh + 1) * DK_A]
            if c < D_A:
                x = l2n(x) * (DK_A ** -0.5)
            elif c < 2 * D_A:
                x = l2n(x)
            qkv_ref[0, :, c:c + DK_A] = x

    def finish_rest(pc, c0):
        pc = pc[HALO:HALO + tm]
        w = pc.shape[1]
        if c0 < COL_QK_B:
            gates_ref[0, :, c0 - COL_GATE_A:c0 - COL_GATE_A + w] = _silu(pc)
        elif c0 < COL_V_B:
            for hh in range(w // LANES):
                x = pc[:, hh * LANES:(hh + 1) * LANES]
                if use_rope:
                    x = (x * rope_ref[0] + pltpu.roll(x, LANES - DK_B // 2, 1) * rope_ref[1]
                         + pltpu.roll(x, DK_B // 2, 1) * rope_ref[2])
                c = c0 - COL_QK_B + hh * LANES
                if c < H_B * DK_B:
                    x = x * (DK_B ** -0.5)
                qkb_ref[0, :, c:c + LANES] = x
        elif c0 < COL_GATE_B:
            vb_ref[0, :, c0 - COL_V_B:c0 - COL_V_B + w] = pc.astype(BF16)
        elif c0 < COL_AB:
            gates_ref[0, :, D_A + c0 - COL_GATE_B:D_A + c0 - COL_GATE_B + w] = _silu(pc)
        else:
            z = pc + par_ref[1:2, :]
            softplus = jnp.maximum(z, 0.0) + jnp.log1p(jnp.exp(-jnp.abs(z)))
            g_all = -jnp.exp(par_ref[0:1, :]) * softplus
            beta_all = 1.0 / (1.0 + jnp.exp(-pc))
            lane = lax.broadcasted_iota(jnp.int32, (tm, LANES), 1)
            gb_ref[0] = jnp.where(lane < N_AB, g_all, jnp.where(lane < 2 * N_AB, beta_all, 0.0))

    heavy = [(finish_qkv, c) for c in range(0, QKV_W, PROJ_GROUP)]
    light = [(finish_rest, c) for c in range(QKV_W, P_W, PROJ_GROUP)]
    order = []
    while heavy or light:
        if heavy:
            order.append(heavy.pop(0))
        if light:
            order.append(light.pop(0))
    previous = None
    for fn, c0 in order:
        pc = project(c0, min(PROJ_GROUP, P_W - c0))
        if previous is not None:
            previous[0](previous[1], previous[2])
        previous = (fn, pc, c0)
    previous[0](previous[1], previous[2])


def _projection(x, mod3, row0, row_stride, w_in_p, convw, par, rope, tm):
    bt, t, _ = x.shape
    nt = t // tm
    hpt = tm // HALO
    nh = t // HALO
    use_rope = rope is not None
    in_specs = [pl.BlockSpec((1, tm, D_MODEL), lambda b, i: (b, i, 0)),
                pl.BlockSpec((1, HALO, D_MODEL), lambda b, i: (b, jnp.maximum(i * hpt - 1, 0), 0)),
                pl.BlockSpec((1, HALO, D_MODEL), lambda b, i: (b, jnp.minimum((i + 1) * hpt, nh - 1), 0)),
                pl.BlockSpec((1, 1, 6 * D_MODEL), lambda b, i: (row0 + row_stride * b, 0, 0)),
                _resident((D_MODEL, P_W)),
                _resident((SUBLANES, QKV_W)),
                _resident((SUBLANES, LANES))]
    args = [x, x, x, mod3, w_in_p, convw, par]
    if use_rope:
        in_specs.append(pl.BlockSpec((3, tm, LANES), lambda b, i: (0, i, 0)))
        args.append(rope)
    widths = (QKV_W, QKB_W, D_B, D_A + D_B, LANES)
    dtypes = (F32, F32, BF16, F32, F32)
    return pl.pallas_call(
        functools.partial(_proj_kernel, use_rope, tm, nt),
        grid=(bt, nt),
        in_specs=in_specs,
        out_specs=[pl.BlockSpec((1, tm, w), lambda b, i: (b, i, 0)) for w in widths],
        out_shape=[jax.ShapeDtypeStruct((bt, t, w), dt) for w, dt in zip(widths, dtypes)],
        compiler_params=pltpu.CompilerParams(dimension_semantics=("arbitrary", "arbitrary"),
                                             vmem_limit_bytes=VMEM_LIMIT),
        name="projection",
    )(*args)


POST_SPLIT = 2
POST_TILE = 512


def _post_kernel(o_ref, x_ref, mod_ref, wo_ref, ln1w_ref, ln1b_ref, w1_ref, b1_ref, w2_ref, b2_ref,
                 ln2w_ref, ln2b_ref, y_ref):
    m = mod_ref[0]
    g1 = m[:, 2 * D_MODEL:3 * D_MODEL]
    sh2 = m[:, 3 * D_MODEL:4 * D_MODEL]
    sc2 = m[:, 4 * D_MODEL:5 * D_MODEL]
    g2 = m[:, 5 * D_MODEL:6 * D_MODEL]
    rows = o_ref.shape[1] // POST_SPLIT
    parts = [slice(r * rows, (r + 1) * rows) for r in range(POST_SPLIT)]
    ys = [jnp.dot(o_ref[0, sl], wo_ref[...], preferred_element_type=F32) for sl in parts]
    x1s = [_layer_norm(ALPHA * x_ref[0, sl] + g1 * y, 1e-6) * ln1w_ref[...] + ln1b_ref[...]
           for sl, y in zip(parts, ys)]
    hs = [(_layer_norm(x1, 1e-6) * (1.0 + sc2) + sh2).astype(BF16) for x1 in x1s]
    acts = [jnp.square(jnp.maximum(jnp.dot(h, w1_ref[...], preferred_element_type=F32) + b1_ref[...], 0.0))
            .astype(BF16) for h in hs]
    fs = [jnp.dot(a, w2_ref[...], preferred_element_type=F32) + b2_ref[...] for a in acts]
    for sl, x1, f in zip(parts, x1s, fs):
        y_ref[0, sl] = _layer_norm(ALPHA * x1 + g2 * f, 1e-6) * ln2w_ref[...] + ln2b_ref[...]


def _post(o, x, mod3, row0, row_stride, wo, ln1w, ln1b, w1, b1, w2, b2, ln2w, ln2b, tm):
    bt, t, _ = x.shape
    return pl.pallas_call(
        _post_kernel,
        grid=(bt, t // tm),
        in_specs=[pl.BlockSpec((1, tm, D_MODEL), lambda b, i: (b, i, 0)),
                  pl.BlockSpec((1, tm, D_MODEL), lambda b, i: (b, i, 0)),
                  pl.BlockSpec((1, 1, 6 * D_MODEL), lambda b, i: (row0 + row_stride * b, 0, 0)),
                  _resident((D_MODEL, D_MODEL)),
                  _resident((1, D_MODEL)),
                  _resident((1, D_MODEL)),
                  _resident((D_MODEL, D_FF)),
                  _resident((1, D_FF)),
                  _resident((D_FF, D_MODEL)),
                  _resident((1, D_MODEL)),
                  _resident((1, D_MODEL)),
                  _resident((1, D_MODEL))],
        out_specs=pl.BlockSpec((1, tm, D_MODEL), lambda b, i: (b, i, 0)),
        out_shape=jax.ShapeDtypeStruct((bt, t, D_MODEL), F32),
        compiler_params=pltpu.CompilerParams(dimension_semantics=("arbitrary", "arbitrary"),
                                             vmem_limit_bytes=VMEM_LIMIT),
        name="post",
    )(o, x, mod3, wo, ln1w, ln1b, w1, b1, w2, b2, ln2w, ln2b)


_LOG_GAMMA = [math.log1p(-(2.0 ** (-5.0 - h))) for h in range(H_B)]

SOLVE_BASE = 8
SEQ_PER_STEP = 4
CAT_W = H_A * CHUNK_A

def _head_blockdiag(x, mask):
    return jnp.where(mask, jnp.concatenate([x] * H_A, 0), 0.0).astype(BF16)


def _cat_inverses(ms, bd_mask):
    i = lax.broadcasted_iota(jnp.int32, (CHUNK_A, CAT_W), 0)
    j = lax.broadcasted_iota(jnp.int32, (CHUNK_A, CAT_W), 1) % CHUNK_A
    same = lambda s: (i // s) == (j // s)
    eye = jnp.where(i == j, 1.0, 0.0)
    mul = lambda a, bd: jnp.dot(a.astype(BF16), bd, preferred_element_type=F32)
    ps = [jnp.where(same(SOLVE_BASE), -m, 0.0) for m in ms]
    ts = [eye + p for p in ps]
    bds = [_head_blockdiag(p, bd_mask) for p in ps]
    for _ in range(int(math.log2(SOLVE_BASE)) - 1):
        ps = [mul(p, bd) for p, bd in zip(ps, bds)]
        bds = [_head_blockdiag(p, bd_mask) for p in ps]
        ts = [t + mul(t, bd) for t, bd in zip(ts, bds)]
    s = SOLVE_BASE
    while s < CHUNK_A:
        sel = same(2 * s) & jnp.logical_not(same(s))
        bds = [_head_blockdiag(t, bd_mask) for t in ts]
        ys = [mul(jnp.where(sel, m, 0.0), bd) for m, bd in zip(ms, bds)]
        bds = [_head_blockdiag(y, bd_mask) for y in ys]
        ts = [t - mul(t, bd) for t, bd in zip(ts, bds)]
        s *= 2
    return ts


def _mixer_kernel(rev, has_init, emit_state, nblk, *refs):
    NS = SEQ_PER_STEP
    TB = TIME_BLOCK
    d = 1 if rev else 0
    j = pl.program_id(1)
    qkv_ref, qkb_ref, vb_ref, gb_ref = refs[:4]
    pos_ = 4
    if rev:
        of_ref, gates_ref, naw_ref, gnw_ref, gnb_ref = refs[pos_:pos_ + 5]
        pos_ += 5
    if has_init:
        sd0_ref, sr0_ref = refs[pos_:pos_ + 2]
        pos_ += 2
    o_ref = refs[pos_]
    pos_ += 1
    if emit_state:
        sd_ref, sr_ref = refs[pos_:pos_ + 2]
        pos_ += 2
    sa_scr, sr_scr = refs[pos_:pos_ + 2]

    @pl.when(j == 0)
    def _():
        if has_init:
            for bb in range(NS):
                for h in range(H_A):
                    sa_scr[bb, :, h * DV_A:(h + 1) * DV_A] = sd0_ref[bb, 0, 0, h]
                sr_scr[bb] = jnp.zeros(sr_scr.shape[1:], F32)
                for h in range(H_B):
                    lo = (h % 2) * DK_B
                    sr_scr[bb, h // 2, lo:lo + DK_B, (h % 2) * DV_B:(h % 2 + 1) * DV_B] = sr0_ref[bb, 0, 0, h]
        else:
            sa_scr[...] = jnp.zeros(sa_scr.shape, F32)
            sr_scr[...] = jnp.zeros(sr_scr.shape, F32)

    ri = lax.broadcasted_iota(jnp.int32, (TB, TB), 0)
    ci = lax.broadcasted_iota(jnp.int32, (TB, TB), 1)
    cum = jnp.where(((ri // CHUNK_A) == (ci // CHUNK_A)) & ((ci >= ri) if rev else (ci <= ri)),
                    1.0, 0.0).astype(BF16)
    er = lax.broadcasted_iota(jnp.int32, (LANES, CAT_W), 0)
    ec = lax.broadcasted_iota(jnp.int32, (LANES, CAT_W), 1)
    esel_c = jnp.where(er == d * H_A + ec // CHUNK_A, 1.0, 0.0).astype(BF16)
    lane_tb = lax.broadcasted_iota(jnp.int32, (TB, LANES), 1)
    i64 = lax.broadcasted_iota(jnp.int32, (CHUNK_A, CAT_W), 0)
    j64 = lax.broadcasted_iota(jnp.int32, (CHUNK_A, CAT_W), 1) % CHUNK_A
    tri = (i64 <= j64) if rev else (i64 >= j64)
    strict = (i64 < j64) if rev else (i64 > j64)
    bd_mask = (lax.broadcasted_iota(jnp.int32, (CAT_W, CAT_W), 0) // CHUNK_A
               == lax.broadcasted_iota(jnp.int32, (CAT_W, CAT_W), 1) // CHUNK_A)
    bdk_mask = (lax.broadcasted_iota(jnp.int32, (CAT_W, D_A), 0) // CHUNK_A
                == lax.broadcasted_iota(jnp.int32, (CAT_W, D_A), 1) // DK_A)
    lane_head = lax.broadcasted_iota(jnp.int32, (CHUNK_A, CAT_W), 1) // CHUNK_A
    nca = TB // CHUNK_A
    order_a = list(range(nca - 1, -1, -1)) if rev else list(range(nca))

    st = []
    for bb in range(NS):
        gbk = gb_ref[bb]
        gc = None
        for part in _split3(gbk):
            t_ = jnp.dot(cum, part, preferred_element_type=F32)
            gc = t_ if gc is None else gc + t_
        gct = None
        for part in _split3(gbk.T[0:2 * SUBLANES]):
            t_ = lax.dot_general(part, cum, (((1,), (1,)), ((), ())), preferred_element_type=F32)
            gct = t_ if gct is None else gct + t_
        gcb = jnp.where(lane_tb < N_AB, gc, gbk)
        gcs = None
        for part in _split3(gcb):
            t_ = jnp.dot(part, esel_c, preferred_element_type=F32)
            gcs = t_ if gcs is None else gcs + t_
        gc5a = jnp.concatenate([jnp.broadcast_to(gcb[:, d * H_A + h:d * H_A + h + 1], (TB, DK_A))
                                for h in range(H_A)], 1)
        b5a = jnp.concatenate([jnp.broadcast_to(gcb[:, N_AB + d * H_A + h:N_AB + d * H_A + h + 1], (TB, DK_A))
                               for h in range(H_A)], 1)
        for n in order_a:
            r0 = n * CHUNK_A
            last = r0 if rev else r0 + CHUNK_A - 1
            it = dict(bb=bb, r0=r0, gc5=gc5a[r0:r0 + CHUNK_A], tot5=gc5a[last:last + 1], b5=b5a[r0:r0 + CHUNK_A])
            q = qkv_ref[bb, r0:r0 + CHUNK_A, 0:D_A]
            k = qkv_ref[bb, r0:r0 + CHUNK_A, D_A:2 * D_A]
            gcc = gcs[r0:r0 + CHUNK_A]
            gcr = jnp.concatenate([gct[d * H_A + h:d * H_A + h + 1, r0:r0 + CHUNK_A] for h in range(H_A)], 1)
            decay = jnp.where(tri, jnp.exp(jnp.where(tri, gcc - gcr, 0.0)), 0.0)
            mq = _dot_nt(jnp.concatenate([k * it["b5"], q], 0), _head_blockdiag(k, bdk_mask))
            it["m"] = jnp.where(strict, mq[:CHUNK_A] * decay, 0.0)
            it["qk"] = jnp.where(tri, mq[CHUNK_A:] * decay, 0.0)
            st.append(it)
    lane = lax.broadcasted_iota(jnp.int32, (CHUNK_B, LANES), 1)
    pi = lax.broadcasted_iota(jnp.int32, (CHUNK_B, CHUNK_B), 0)
    pj = lax.broadcasted_iota(jnp.int32, (CHUNK_B, CHUNK_B), 1)
    pos = lax.broadcasted_iota(jnp.int32, (CHUNK_B, 1), 0)
    if rev:
        pi, pj, pos = CHUNK_B - 1 - pi, CHUNK_B - 1 - pj, CHUNK_B - 1 - pos
    pdiff = (pi - pj).astype(F32)
    posf = pos.astype(F32)
    first_half = (lane // DK_B) == 0
    own_block = (lax.broadcasted_iota(jnp.int32, (LANES, 2 * DV_B), 0) // DK_B
                 == lax.broadcasted_iota(jnp.int32, (LANES, 2 * DV_B), 1) // DV_B)
    ncb = TB // CHUNK_B
    order_b = range(ncb - 1, -1, -1) if rev else range(ncb)
    rt = []
    for pp in range(H_B // 2):
        lgs = [_LOG_GAMMA[H_B - 1 - h] if rev else _LOG_GAMMA[h] for h in (2 * pp, 2 * pp + 1)]
        d_mask = jnp.concatenate([jnp.where(pdiff >= 0, jnp.exp(lg * jnp.maximum(pdiff, 0.0)), 0.0) for lg in lgs], 1)
        xi = jnp.where(first_half, jnp.exp(lgs[0] * (posf + 1.0)), jnp.exp(lgs[1] * (posf + 1.0)))
        zeta = jnp.where(first_half, jnp.exp(lgs[0] * (CHUNK_B - 1.0 - posf)), jnp.exp(lgs[1] * (CHUNK_B - 1.0 - posf)))
        col_head = lax.broadcasted_iota(jnp.int32, (1, 2 * DV_B), 1) // DV_B
        g_chunk = jnp.where(col_head == 0, math.exp(lgs[0] * CHUNK_B), math.exp(lgs[1] * CHUNK_B))
        for bb in range(NS):
            for n in order_b:
                rt.append(dict(bb=bb, r0=n * CHUNK_B, pp=pp, g_chunk=g_chunk, d_mask=d_mask, xi=xi, zeta=zeta))

    def retention_independent(it):
        bb, r0, pp = it["bb"], it["r0"], it["pp"]
        q = qkb_ref[bb, r0:r0 + CHUNK_B, pp * LANES:(pp + 1) * LANES]
        k = qkb_ref[bb, r0:r0 + CHUNK_B, H_B * DK_B + pp * LANES:H_B * DK_B + (pp + 1) * LANES]
        v0 = vb_ref[bb, r0:r0 + CHUNK_B, 2 * pp * DV_B:(2 * pp + 1) * DV_B]
        v1 = vb_ref[bb, r0:r0 + CHUNK_B, (2 * pp + 1) * DV_B:(2 * pp + 2) * DV_B]
        zb = jnp.zeros((CHUNK_B, DV_B), BF16)
        ksplit = jnp.concatenate([jnp.where(first_half, k, 0.0), jnp.where(first_half, 0.0, k)], 0)
        scores = _dot_nt(q, ksplit) * it["d_mask"]
        v_bd = jnp.concatenate([jnp.concatenate([v0, zb], 1), jnp.concatenate([zb, v1], 1)], 0)
        it["inner"] = jnp.dot(scores.astype(BF16), v_bd, preferred_element_type=F32)
        it["qx"] = q * it["xi"]
        ds = lax.dot_general((k * it["zeta"]).astype(BF16), jnp.concatenate([v0, v1], 1),
                             (((0,), (0,)), ((), ())), preferred_element_type=F32)
        it["ds"] = jnp.where(own_block, ds, 0.0)

    fillers = [functools.partial(retention_independent, it) for it in rt]
    inv = _cat_inverses([it["m"] for it in st], bd_mask)
    for it, t in zip(st, inv):
        bb, r0 = it["bb"], it["r0"]
        k = qkv_ref[bb, r0:r0 + CHUNK_A, D_A:2 * D_A]
        v = qkv_ref[bb, r0:r0 + CHUNK_A, 2 * D_A:3 * D_A]
        vb5 = v * it["b5"]
        kbe5 = (k * it["b5"]) * jnp.exp(it["gc5"])
        rhs = jnp.concatenate(
            [jnp.concatenate([vb5[:, h * DV_A:(h + 1) * DV_A], kbe5[:, h * DK_A:(h + 1) * DK_A]], 1)
             for h in range(H_A)], 0).astype(BF16)
        lhs = jnp.concatenate([jnp.where(lane_head == h, t, 0.0) for h in range(H_A)], 0)
        sol = jnp.dot(lhs.astype(BF16), rhs, preferred_element_type=F32)
        it["u"] = jnp.concatenate([sol[h * CHUNK_A:(h + 1) * CHUNK_A, :DV_A] for h in range(H_A)], 1)
        it["w"] = jnp.concatenate([sol[h * CHUNK_A:(h + 1) * CHUNK_A, DV_A:] for h in range(H_A)], 1)

    def finish(bb, r0, rows, c0, width, o):
        if not rev:
            o_ref[bb, r0:r0 + rows, c0:c0 + width] = o
            return
        o = o + of_ref[bb, r0:r0 + rows, c0:c0 + width]
        outs = []
        for hh in range(width // LANES):
            x = o[:, hh * LANES:(hh + 1) * LANES]
            if c0 < D_A:
                x = x * lax.rsqrt(jnp.mean(x * x, -1, keepdims=True) + 1e-6) * naw_ref[...]
            else:
                cb = c0 - D_A + hh * LANES
                x = _layer_norm(x, 1e-5) * gnw_ref[:, cb:cb + LANES] + gnb_ref[:, cb:cb + LANES]
            outs.append(x)
        o = outs[0] if len(outs) == 1 else jnp.concatenate(outs, 1)
        o_ref[bb, r0:r0 + rows, c0:c0 + width] = (o * gates_ref[bb, r0:r0 + rows, c0:c0 + width]).astype(BF16)

    per_step = -(-len(fillers) // nca)
    for c in range(nca):
        grp_items = [st[bb * nca + c] for bb in range(NS)]
        s_old = [sa_scr[it["bb"]] for it in grp_items]
        prods = []
        for it, s in zip(grp_items, s_old):
            q_dec = qkv_ref[it["bb"], it["r0"]:it["r0"] + CHUNK_A, 0:D_A] * jnp.exp(it["gc5"])
            pr = []
            for pp in range(H_A // 2):
                lo = pp * 2 * DV_A
                z = jnp.zeros((DK_A, DV_A), F32)
                bds = jnp.concatenate([jnp.concatenate([s[:, lo:lo + DV_A], z], 1),
                                       jnp.concatenate([z, s[:, lo + DV_A:lo + 2 * DV_A]], 1)], 0)
                lhs = jnp.concatenate([it["w"][:, lo:lo + 2 * DK_A], q_dec[:, lo:lo + 2 * DK_A]], 0)
                pr.append(_dot(lhs, bds))
            prods.append(jnp.concatenate(pr, 1))
        for fill in fillers[c * per_step:(c + 1) * per_step]:
            fill()
        v_new = [it["u"] - pr[:CHUNK_A] for it, pr in zip(grp_items, prods)]
        vbd = [_head_blockdiag(v, bdk_mask) for v in v_new]
        both = []
        for it, vb_ in zip(grp_items, vbd):
            k_dec = (qkv_ref[it["bb"], it["r0"]:it["r0"] + CHUNK_A, D_A:2 * D_A]
                     * jnp.exp(it["tot5"] - it["gc5"]))
            kd_stack = jnp.concatenate([k_dec[:, h * DK_A:(h + 1) * DK_A] for h in range(H_A)], 0)
            lhs = jnp.concatenate([kd_stack.T, it["qk"]], 0)
            both.append(_dot(lhs, vb_))
        for it, s, x in zip(grp_items, s_old, both):
            sa_scr[it["bb"]] = s * jnp.exp(it["tot5"]) + x[:DK_A]
        for it, pr, x in zip(grp_items, prods, both):
            finish(it["bb"], it["r0"], CHUNK_A, 0, D_A, pr[CHUNK_A:] + x[DK_A:])

    for it in rt:
        bb, pp = it["bb"], it["pp"]
        s = sr_scr[bb, pp]
        o = it["inner"] + _dot(it["qx"], s)
        sr_scr[bb, pp] = s * it["g_chunk"] + it["ds"]
        finish(bb, it["r0"], CHUNK_B, D_A + 2 * pp * DV_B, 2 * DV_B, o)

    if emit_state:
        @pl.when(j == nblk - 1)
        def _():
            for bb in range(NS):
                for h in range(H_A):
                    sd_ref[bb, 0, 0, h] = sa_scr[bb, :, h * DV_A:(h + 1) * DV_A]
                for h in range(H_B):
                    lo = (h % 2) * DK_B
                    sr_ref[bb, 0, 0, h] = sr_scr[bb, h // 2, lo:lo + DK_B, (h % 2) * DV_B:(h % 2 + 1) * DV_B]


def _mixer(rev, qkv, qkb, vb, gb, o_fwd, gates, naw, gnw, gnb, sd0, sr0, emit_state):
    bt, t, _ = qkv.shape
    NS = SEQ_PER_STEP
    TB = TIME_BLOCK
    nblk = t // TB
    has_init = sd0 is not None
    d = 1 if rev else 0
    tb_of = (lambda j: nblk - 1 - j) if rev else (lambda j: j)
    tok = lambda w: pl.BlockSpec((NS, TB, w), lambda b, j: (b, tb_of(j), 0))
    in_specs = [tok(QKV_W), tok(QKB_W), tok(D_B), tok(LANES)]
    args = [qkv, qkb, vb, gb]
    if rev:
        in_specs += [tok(D_A + D_B), tok(D_A + D_B), _resident((1, DV_A)), _resident((1, D_B)), _resident((1, D_B))]
        args += [o_fwd, gates, naw, gnw, gnb]
    if has_init:
        in_specs.append(pl.BlockSpec((NS, 1, 1, H_A, DK_A, DV_A), lambda b, j: (b, 0, d, 0, 0, 0)))
        in_specs.append(pl.BlockSpec((NS, 1, 1, H_B, DK_B, DV_B), lambda b, j: (b, 0, d, 0, 0, 0)))
        args += [sd0, sr0]
    out_specs = [tok(D_A + D_B)]
    out_shape = [jax.ShapeDtypeStruct((bt, t, D_A + D_B), BF16 if rev else F32)]
    if emit_state:
        out_specs.append(pl.BlockSpec((NS, 1, 1, H_A, DK_A, DV_A), lambda b, j: (b, 0, 0, 0, 0, 0)))
        out_specs.append(pl.BlockSpec((NS, 1, 1, H_B, DK_B, DV_B), lambda b, j: (b, 0, 0, 0, 0, 0)))
        out_shape.append(jax.ShapeDtypeStruct((bt, DEPTH, 1, H_A, DK_A, DV_A), F32))
        out_shape.append(jax.ShapeDtypeStruct((bt, DEPTH, 1, H_B, DK_B, DV_B), F32))
    scratch = [pltpu.VMEM((NS, DK_A, H_A * DV_A), F32),
               pltpu.VMEM((NS, H_B // 2, LANES, 2 * DV_B), F32)]
    return pl.pallas_call(
        functools.partial(_mixer_kernel, rev, has_init, emit_state, nblk),
        grid=(bt // NS, nblk),
        in_specs=in_specs,
        out_specs=out_specs,
        out_shape=out_shape,
        scratch_shapes=scratch,
        compiler_params=pltpu.CompilerParams(dimension_semantics=("arbitrary", "arbitrary"),
                                             vmem_limit_bytes=VMEM_LIMIT),
        name="mixer_bwd" if rev else "mixer_fwd",
    )(*args)


def _rope_tables(t):
    rows = t // GRID_W
    r = np.repeat(np.arange(rows, dtype=np.float32), GRID_W)
    col = np.tile(np.arange(GRID_W, dtype=np.float32), rows)
    nf = DK_B // 4
    inv = (np.float32(ROPE_BASE) ** (-np.arange(nf, dtype=np.float32) / np.float32(nf))).astype(np.float32)
    ang = jnp.asarray(np.concatenate([r[:, None] * inv, col[:, None] * inv], -1).astype(np.float32))
    cos, sin = jnp.cos(ang), jnp.sin(ang)
    zero = jnp.zeros_like(sin)
    cos_t = jnp.tile(cos, (1, 4))
    sin_up = jnp.tile(jnp.concatenate([-sin, zero], -1), (1, 2))
    sin_dn = jnp.tile(jnp.concatenate([zero, sin], -1), (1, 2))
    return jnp.stack([cos_t, sin_up, sin_dn], 0)


def kernel(x_prompt, x_sample, c, state_delta, state_ret, c_ctx, w_mod, b_mod, w_in, conv_w, a_log, dt_bias,
           norm_a_w, gn_w, gn_b, w_o, ln1_w, ln1_b, w_ff1, b_ff1, w_ff2, b_ff2, ln2_w, ln2_b):
    assert w_mod.shape[0] == DEPTH == 1
    n_dec = c.shape[0]
    rows = -(-(1 + n_dec) // SUBLANES) * SUBLANES
    cond = jnp.zeros((rows, D_MODEL), F32).at[0].set(c_ctx).at[1:1 + n_dec].set(c)
    mod = _modulation(cond, w_mod[0], b_mod[0])
    mod3 = mod.reshape(rows, 1, 6 * D_MODEL)

    w = w_in[0]
    c_ab = QKV_W + D_A
    w_in_p = (jnp.zeros((D_MODEL, P_W), BF16)
              .at[:, :c_ab].set(w[:, :c_ab].astype(BF16))
              .at[:, c_ab:COL_AB].set(w[:, c_ab + 2 * N_AB:].astype(BF16))
              .at[:, COL_AB:COL_AB + 2 * N_AB].set(w[:, c_ab:c_ab + 2 * N_AB].astype(BF16)))
    convw = jnp.zeros((SUBLANES, QKV_W), F32).at[:CONV_K].set(conv_w[0])
    par = (jnp.zeros((SUBLANES, LANES), F32).at[0, :N_AB].set(a_log[0].reshape(-1))
           .at[1, :N_AB].set(dt_bias[0].reshape(-1)))
    naw = norm_a_w[0].reshape(1, DV_A)
    gnw = gn_w[0].reshape(1, D_B)
    gnb = gn_b[0].reshape(1, D_B)
    wo = w_o[0].astype(BF16)
    w1 = w_ff1[0].astype(BF16)
    w2 = w_ff2[0].astype(BF16)
    row = lambda v: v[0].reshape(1, -1)

    def trunk(x, row0, row_stride, rope, sd0, sr0, emit_state, tm):
        qkv, qkb, vb, gates, gb = _projection(x, mod3, row0, row_stride, w_in_p, convw, par, rope, tm)
        fwd = _mixer(False, qkv, qkb, vb, gb, None, None, None, None, None, sd0, sr0, emit_state)
        bwd = _mixer(True, qkv, qkb, vb, gb, fwd[0], gates, naw, gnw, gnb, sd0, sr0, emit_state)
        o = bwd[0]
        if row_stride == 0:
            y = _post(o.reshape(1, -1, D_A + D_B), x.reshape(1, -1, D_MODEL), mod3, row0, 0, wo, row(ln1_w),
                      row(ln1_b), w1, row(b_ff1), w2, row(b_ff2), row(ln2_w), row(ln2_b), POST_TILE).reshape(x.shape)
        else:
            y = _post(o, x, mod3, row0, row_stride, wo, row(ln1_w), row(ln1_b), w1, row(b_ff1), w2, row(b_ff2),
                      row(ln2_w), row(ln2_b), POST_TILE)
        states = [jnp.concatenate([f, b], 2) for f, b in zip(fwd[1:], bwd[1:])]
        return y, states

    y_prompt, (new_sd, new_sr) = trunk(x_prompt, 0, 0, None, None, None, True, 256)
    y_sample, _ = trunk(x_sample, 1, 1, _rope_tables(x_sample.shape[1]), state_delta, state_ret, False, 512)
    return y_prompt, y_sample, new_sd, new_sr
```

```python
import functools
import math

import jax
import jax.numpy as jnp
import numpy as np
from jax import lax
from jax.experimental import pallas as pl
from jax.experimental.pallas import tpu as pltpu

F32 = jnp.float32
BF16 = jnp.bfloat16

D_MODEL = 1024
H_A, DK_A, DV_A = 4, 128, 128
D_A = H_A * DV_A
CONV_K = 5
CHUNK_A = 64
H_B, DK_B, DV_B = 4, 64, 128
D_B = H_B * DV_B
CHUNK_B = 128
GRID_W = 64
ROPE_BASE = 10000.0
D_FF = 4 * D_MODEL
DEPTH = 1
ALPHA = (2.0 * DEPTH) ** 0.25

LANES = 128
SUBLANES = 8
VMEM_LIMIT = 56 * 1024 * 1024

QKV_W = 3 * D_A
COL_GATE_A = QKV_W
COL_QK_B = COL_GATE_A + D_A
QKB_W = 2 * H_B * DK_B
COL_V_B = COL_QK_B + QKB_W
COL_GATE_B = COL_V_B + D_B
COL_AB = COL_GATE_B + D_B
P_W = COL_AB + LANES
N_AB = 2 * H_A

TIME_BLOCK = 256
HALO = SUBLANES
PROJ_GROUP = 256
assert CONV_K == 5 and (CONV_K - 1) // 2 <= HALO


def _dot(a, b):
    return jnp.dot(a.astype(BF16), b.astype(BF16), preferred_element_type=F32)


def _dot_nt(a, b):
    return lax.dot_general(a.astype(BF16), b.astype(BF16), (((1,), (1,)), ((), ())),
                           preferred_element_type=F32)


def _dot_tn(a, b):
    return lax.dot_general(a.astype(BF16), b.astype(BF16), (((0,), (0,)), ((), ())),
                           preferred_element_type=F32)


def _split3(x):
    hi = x.astype(BF16)
    r = x - hi.astype(F32)
    mid = r.astype(BF16)
    lo = (r - mid.astype(F32)).astype(BF16)
    return hi, mid, lo


def _silu(x):
    h = 0.5 * x
    return h + h * jnp.tanh(h)


def _layer_norm(x, eps):
    mu = jnp.mean(x, -1, keepdims=True)
    xc = x - mu
    var = jnp.mean(xc * xc, -1, keepdims=True)
    return xc * lax.rsqrt(var + eps)


def _resident(shape):
    return pl.BlockSpec(shape, lambda *_: (0,) * len(shape), pipeline_mode=pl.Buffered(1))


def _mod_kernel(c_ref, w_ref, b_ref, o_ref):
    o_ref[...] = _dot(_silu(c_ref[...]), w_ref[...]) + b_ref[...]


def _modulation(cond, w_mod, b_mod):
    rows = cond.shape[0]
    n = w_mod.shape[1]
    bn = 1536
    return pl.pallas_call(
        _mod_kernel,
        grid=(n // bn,),
        in_specs=[pl.BlockSpec((rows, D_MODEL), lambda j: (0, 0)),
                  pl.BlockSpec((D_MODEL, bn), lambda j: (0, j)),
                  pl.BlockSpec((1, bn), lambda j: (0, j))],
        out_specs=pl.BlockSpec((rows, bn), lambda j: (0, j)),
        out_shape=jax.ShapeDtypeStruct((rows, n), F32),
        compiler_params=pltpu.CompilerParams(dimension_semantics=("arbitrary",)),
        name="modulation",
    )(cond, w_mod, b_mod.reshape(1, n))


def _proj_kernel(use_rope, tm, nt, *refs):
    if use_rope:
        (x_ref, xp_ref, xn_ref, mod_ref, w_ref, convw_ref, par_ref, rope_ref,
         qkv_ref, qkb_ref, vb_ref, gates_ref, gb_ref) = refs
    else:
        (x_ref, xp_ref, xn_ref, mod_ref, w_ref, convw_ref, par_ref,
         qkv_ref, qkb_ref, vb_ref, gates_ref, gb_ref) = refs
    i = pl.program_id(1)
    m = mod_ref[0]
    sh1 = m[:, 0:D_MODEL]
    sc1 = m[:, D_MODEL:2 * D_MODEL]

    def modulated(x):
        return _layer_norm(x, 1e-6) * (1.0 + sc1) + sh1

    hb = jnp.concatenate([modulated(xp_ref[0]), modulated(x_ref[0]), modulated(xn_ref[0])], 0).astype(BF16)
    rows = tm + 2 * HALO

    def project(c0, width):
        return jnp.dot(hb, w_ref[:, c0:c0 + width], preferred_element_type=F32)

    def conv_silu(pc, c0):
        xe = jnp.concatenate([jnp.where(i > 0, pc[0:HALO], 0.0), pc[HALO:HALO + tm],
                              jnp.where(i < nt - 1, pc[HALO + tm:], 0.0)], 0)
        z = [xe * convw_ref[kk:kk + 1, c0:c0 + pc.shape[1]] for kk in range(CONV_K)]
        back = pltpu.roll(z[1] + pltpu.roll(z[0], 1, 0), 1, 0)
        ahead = pltpu.roll(z[3] + pltpu.roll(z[4], rows - 1, 0), rows - 1, 0)
        return _silu((z[2] + back + ahead)[HALO:HALO + tm])

    def l2n(x):
        return x * lax.rsqrt(jnp.sum(x * x, -1, keepdims=True) + 1e-6)

    def finish_qkv(pc, c0):
        y = conv_silu(pc, c0)
        for hh in range(pc.shape[1] // DK_A):
            c = c0 + hh * DK_A
            x = y[:, hh * DK_A:(hh + 1) * DK_A]
            if c < D_A:
                x = l2n(x) * (DK_A ** -0.5)
            elif c < 2 * D_A:
                x = l2n(x)
            qkv_ref[0, :, c:c + DK_A] = x

    def finish_rest(pc, c0):
        pc = pc[HALO:HALO + tm]
        w = pc.shape[1]
        if c0 < COL_QK_B:
            gates_ref[0, :, c0 - COL_GATE_A:c0 - COL_GATE_A + w] = _silu(pc)
        elif c0 < COL_V_B:
            for hh in range(w // LANES):
                x = pc[:, hh * LANES:(hh + 1) * LANES]
                if use_rope:
                    x = (x * rope_ref[0] + pltpu.roll(x, LANES - DK_B // 2, 1) * rope_ref[1]
                         + pltpu.roll(x, DK_B // 2, 1) * rope_ref[2])
                c = c0 - COL_QK_B + hh * LANES
                if c < H_B * DK_B:
                    x = x * (DK_B ** -0.5)
                qkb_ref[0, :, c:c + LANES] = x
        elif c0 < COL_GATE_B:
            vb_ref[0, :, c0 - COL_V_B:c0 - COL_V_B + w] = pc.astype(BF16)
        elif c0 < COL_AB:
            gates_ref[0, :, D_A + c0 - COL_GATE_B:D_A + c0 - COL_GATE_B + w] = _silu(pc)
        else:
            z = pc + par_ref[1:2, :]
            softplus = jnp.maximum(z, 0.0) + jnp.log1p(jnp.exp(-jnp.abs(z)))
            g_all = -jnp.exp(par_ref[0:1, :]) * softplus
            beta_all = 1.0 / (1.0 + jnp.exp(-pc))
            lane = lax.broadcasted_iota(jnp.int32, (tm, LANES), 1)
            gb_ref[0] = jnp.where(lane < N_AB, g_all, jnp.where(lane < 2 * N_AB, beta_all, 0.0))

    heavy = [(finish_qkv, c) for c in range(0, QKV_W, PROJ_GROUP)]
    light = [(finish_rest, c) for c in range(QKV_W, P_W, PROJ_GROUP)]
    order = []
    while heavy or light:
        if heavy:
            order.append(heavy.pop(0))
        if light:
            order.append(light.pop(0))
    previous = None
    for fn, c0 in order:
        pc = project(c0, min(PROJ_GROUP, P_W - c0))
        if previous is not None:
            previous[0](previous[1], previous[2])
        previous = (fn, pc, c0)
    previous[0](previous[1], previous[2])


def _projection(x, mod3, row0, row_stride, w_in_p, convw, par, rope, tm):
    bt, t, _ = x.shape
    nt = t // tm
    hpt = tm // HALO
    nh = t // HALO
    use_rope = rope is not None
    in_specs = [pl.BlockSpec((1, tm, D_MODEL), lambda b, i: (b, i, 0)),
                pl.BlockSpec((1, HALO, D_MODEL), lambda b, i: (b, jnp.maximum(i * hpt - 1, 0), 0)),
                pl.BlockSpec((1, HALO, D_MODEL), lambda b, i: (b, jnp.minimum((i + 1) * hpt, nh - 1), 0)),
                pl.BlockSpec((1, 1, 6 * D_MODEL), lambda b, i: (row0 + row_stride * b, 0, 0)),
                _resident((D_MODEL, P_W)),
                _resident((SUBLANES, QKV_W)),
                _resident((SUBLANES, LANES))]
    args = [x, x, x, mod3, w_in_p, convw, par]
    if use_rope:
        in_specs.append(pl.BlockSpec((3, tm, LANES), lambda b, i: (0, i, 0)))
        args.append(rope)
    widths = (QKV_W, QKB_W, D_B, D_A + D_B, LANES)
    dtypes = (F32, F32, BF16, F32, F32)
    return pl.pallas_call(
        functools.partial(_proj_kernel, use_rope, tm, nt),
        grid=(bt, nt),
        in_specs=in_specs,
        out_specs=[pl.BlockSpec((1, tm, w), lambda b, i: (b, i, 0)) for w in widths],
        out_shape=[jax.ShapeDtypeStruct((bt, t, w), dt) for w, dt in zip(widths, dtypes)],
        compiler_params=pltpu.CompilerParams(dimension_semantics=("arbitrary", "arbitrary"),
                                             vmem_limit_bytes=VMEM_LIMIT),
        name="projection",
    )(*args)


POST_SPLIT = 2
POST_TILE = 512


def _post_kernel(o_ref, x_ref, mod_ref, wo_ref, ln1w_ref, ln1b_ref, w1_ref, b1_ref, w2_ref, b2_ref,
                 ln2w_ref, ln2b_ref, y_ref):
    m = mod_ref[0]
    g1 = m[:, 2 * D_MODEL:3 * D_MODEL]
    sh2 = m[:, 3 * D_MODEL:4 * D_MODEL]
    sc2 = m[:, 4 * D_MODEL:5 * D_MODEL]
    g2 = m[:, 5 * D_MODEL:6 * D_MODEL]
    rows = o_ref.shape[1] // POST_SPLIT
    parts = [slice(r * rows, (r + 1) * rows) for r in range(POST_SPLIT)]
    ys = [jnp.dot(o_ref[0, sl], wo_ref[...], preferred_element_type=F32) for sl in parts]
    x1s = [_layer_norm(ALPHA * x_ref[0, sl] + g1 * y, 1e-6) * ln1w_ref[...] + ln1b_ref[...]
           for sl, y in zip(parts, ys)]
    hs = [(_layer_norm(x1, 1e-6) * (1.0 + sc2) + sh2).astype(BF16) for x1 in x1s]
    acts = [jnp.square(jnp.maximum(jnp.dot(h, w1_ref[...], preferred_element_type=F32) + b1_ref[...], 0.0))
            .astype(BF16) for h in hs]
    fs = [jnp.dot(a, w2_ref[...], preferred_element_type=F32) + b2_ref[...] for a in acts]
    for sl, x1, f in zip(parts, x1s, fs):
        y_ref[0, sl] = _layer_norm(ALPHA * x1 + g2 * f, 1e-6) * ln2w_ref[...] + ln2b_ref[...]


def _post(o, x, mod3, row0, row_stride, wo, ln1w, ln1b, w1, b1, w2, b2, ln2w, ln2b, tm):
    bt, t, _ = x.shape
    return pl.pallas_call(
        _post_kernel,
        grid=(bt, t // tm),
        in_specs=[pl.BlockSpec((1, tm, D_MODEL), lambda b, i: (b, i, 0)),
                  pl.BlockSpec((1, tm, D_MODEL), lambda b, i: (b, i, 0)),
                  pl.BlockSpec((1, 1, 6 * D_MODEL), lambda b, i: (row0 + row_stride * b, 0, 0)),
                  _resident((D_MODEL, D_MODEL)),
                  _resident((1, D_MODEL)),
                  _resident((1, D_MODEL)),
                  _resident((D_MODEL, D_FF)),
                  _resident((1, D_FF)),
                  _resident((D_FF, D_MODEL)),
                  _resident((1, D_MODEL)),
                  _resident((1, D_MODEL)),
                  _resident((1, D_MODEL))],
        out_specs=pl.BlockSpec((1, tm, D_MODEL), lambda b, i: (b, i, 0)),
        out_shape=jax.ShapeDtypeStruct((bt, t, D_MODEL), F32),
        compiler_params=pltpu.CompilerParams(dimension_semantics=("arbitrary", "arbitrary"),
                                             vmem_limit_bytes=VMEM_LIMIT),
        name="post",
    )(o, x, mod3, wo, ln1w, ln1b, w1, b1, w2, b2, ln2w, ln2b)


_LOG_GAMMA = [math.log1p(-(2.0 ** (-5.0 - h))) for h in range(H_B)]

SOLVE_BASE = 8
SEQ_PER_STEP = 4
CAT_W = H_A * CHUNK_A

def _head_blockdiag(x, mask):
    return jnp.where(mask, jnp.concatenate([x] * H_A, 0), 0.0).astype(BF16)


def _cat_inverses(ms, bd_mask):
    i = lax.broadcasted_iota(jnp.int32, (CHUNK_A, CAT_W), 0)
    j = lax.broadcasted_iota(jnp.int32, (CHUNK_A, CAT_W), 1) % CHUNK_A
    same = lambda s: (i // s) == (j // s)
    eye = jnp.where(i == j, 1.0, 0.0)
    mul = lambda a, bd: jnp.dot(a.astype(BF16), bd, preferred_element_type=F32)
    ps = [jnp.where(same(SOLVE_BASE), -m, 0.0) for m in ms]
    ts = [eye + p for p in ps]
    bds = [_head_blockdiag(p, bd_mask) for p in ps]
    ps = [mul(p, bd) for p, bd in zip(ps, bds)]
    bds = [_head_blockdiag(p, bd_mask) for p in ps]
    for _ in range(int(math.log2(SOLVE_BASE)) - 2):
        both = [mul(jnp.concatenate([t, p], 0), bd) for t, p, bd in zip(ts, ps, bds)]
        ts = [t + x[:CHUNK_A] for t, x in zip(ts, both)]
        ps = [x[CHUNK_A:] for x in both]
        bds = [_head_blockdiag(p, bd_mask) for p in ps]
    ts = [t + mul(t, bd) for t, bd in zip(ts, bds)]
    s = SOLVE_BASE
    while s < CHUNK_A:
        sel = same(2 * s) & jnp.logical_not(same(s))
        bds = [_head_blockdiag(t, bd_mask) for t in ts]
        ys = [mul(jnp.where(sel, m, 0.0), bd) for m, bd in zip(ms, bds)]
        bds = [_head_blockdiag(y, bd_mask) for y in ys]
        ts = [t - mul(t, bd) for t, bd in zip(ts, bds)]
        s *= 2
    return ts


def _mixer_kernel(rev, has_init, emit_state, nblk, *refs):
    NS = SEQ_PER_STEP
    TB = TIME_BLOCK
    d = 1 if rev else 0
    j = pl.program_id(1)
    qkv_ref, qkb_ref, vb_ref, gb_ref = refs[:4]
    pos_ = 4
    if rev:
        of_ref, gates_ref, naw_ref, gnw_ref, gnb_ref = refs[pos_:pos_ + 5]
        pos_ += 5
    if has_init:
        sd0_ref, sr0_ref = refs[pos_:pos_ + 2]
        pos_ += 2
    o_ref = refs[pos_]
    pos_ += 1
    if emit_state:
        sd_ref, sr_ref = refs[pos_:pos_ + 2]
        pos_ += 2
    sa_scr, sr_scr = refs[pos_:pos_ + 2]

    @pl.when(j == 0)
    def _():
        if has_init:
            for bb in range(NS):
                for h in range(H_A):
                    sa_scr[bb, :, h * DV_A:(h + 1) * DV_A] = sd0_ref[bb, 0, 0, h]
                sr_scr[bb] = jnp.zeros(sr_scr.shape[1:], F32)
                for h in range(H_B):
                    lo = (h % 2) * DK_B
                    sr_scr[bb, h // 2, lo:lo + DK_B, (h % 2) * DV_B:(h % 2 + 1) * DV_B] = sr0_ref[bb, 0, 0, h]
        else:
            sa_scr[...] = jnp.zeros(sa_scr.shape, F32)
            sr_scr[...] = jnp.zeros(sr_scr.shape, F32)

    ri = lax.broadcasted_iota(jnp.int32, (TB, TB), 0)
    ci = lax.broadcasted_iota(jnp.int32, (TB, TB), 1)
    cum = jnp.where(((ri // CHUNK_A) == (ci // CHUNK_A)) & ((ci >= ri) if rev else (ci <= ri)),
                    1.0, 0.0).astype(BF16)
    er = lax.broadcasted_iota(jnp.int32, (LANES, CAT_W), 0)
    ec = lax.broadcasted_iota(jnp.int32, (LANES, CAT_W), 1)
    esel_c = jnp.where(er == d * H_A + ec // CHUNK_A, 1.0, 0.0).astype(BF16)
    lane_tb = lax.broadcasted_iota(jnp.int32, (TB, LANES), 1)
    i64 = lax.broadcasted_iota(jnp.int32, (CHUNK_A, CAT_W), 0)
    j64 = lax.broadcasted_iota(jnp.int32, (CHUNK_A, CAT_W), 1) % CHUNK_A
    tri = (i64 <= j64) if rev else (i64 >= j64)
    strict = (i64 < j64) if rev else (i64 > j64)
    bd_mask = (lax.broadcasted_iota(jnp.int32, (CAT_W, CAT_W), 0) // CHUNK_A
               == lax.broadcasted_iota(jnp.int32, (CAT_W, CAT_W), 1) // CHUNK_A)
    bdk_mask = (lax.broadcasted_iota(jnp.int32, (CAT_W, D_A), 0) // CHUNK_A
                == lax.broadcasted_iota(jnp.int32, (CAT_W, D_A), 1) // DK_A)
    lane_head = lax.broadcasted_iota(jnp.int32, (CHUNK_A, CAT_W), 1) // CHUNK_A
    nca = TB // CHUNK_A
    order_a = list(range(nca - 1, -1, -1)) if rev else list(range(nca))

    st = []
    for bb in range(NS):
        gbk = gb_ref[bb]
        gc = None
        for part in _split3(gbk):
            t_ = jnp.dot(cum, part, preferred_element_type=F32)
            gc = t_ if gc is None else gc + t_
        gct = None
        for part in _split3(gbk.T[0:2 * SUBLANES]):
            t_ = lax.dot_general(part, cum, (((1,), (1,)), ((), ())), preferred_element_type=F32)
            gct = t_ if gct is None else gct + t_
        gcb = jnp.where(lane_tb < N_AB, gc, gbk)
        gcs = None
        for part in _split3(gcb):
            t_ = jnp.dot(part, esel_c, preferred_element_type=F32)
            gcs = t_ if gcs is None else gcs + t_
        gc5a = jnp.concatenate([jnp.broadcast_to(gcb[:, d * H_A + h:d * H_A + h + 1], (TB, DK_A))
                                for h in range(H_A)], 1)
        b5a = jnp.concatenate([jnp.broadcast_to(gcb[:, N_AB + d * H_A + h:N_AB + d * H_A + h + 1], (TB, DK_A))
                               for h in range(H_A)], 1)
        for n in order_a:
            r0 = n * CHUNK_A
            last = r0 if rev else r0 + CHUNK_A - 1
            it = dict(bb=bb, r0=r0, gc5=gc5a[r0:r0 + CHUNK_A], tot5=gc5a[last:last + 1], b5=b5a[r0:r0 + CHUNK_A])
            q = qkv_ref[bb, r0:r0 + CHUNK_A, 0:D_A]
            k = qkv_ref[bb, r0:r0 + CHUNK_A, D_A:2 * D_A]
            gcc = gcs[r0:r0 + CHUNK_A]
            gcr = jnp.concatenate([gct[d * H_A + h:d * H_A + h + 1, r0:r0 + CHUNK_A] for h in range(H_A)], 1)
            decay = jnp.where(tri, jnp.exp(jnp.where(tri, gcc - gcr, 0.0)), 0.0)
            mq = _dot_nt(jnp.concatenate([k * it["b5"], q], 0), _head_blockdiag(k, bdk_mask))
            it["m"] = jnp.where(strict, mq[:CHUNK_A] * decay, 0.0)
            it["qk"] = jnp.where(tri, mq[CHUNK_A:] * decay, 0.0)
            st.append(it)
    lane = lax.broadcasted_iota(jnp.int32, (CHUNK_B, LANES), 1)
    pi = lax.broadcasted_iota(jnp.int32, (CHUNK_B, CHUNK_B), 0)
    pj = lax.broadcasted_iota(jnp.int32, (CHUNK_B, CHUNK_B), 1)
    pos = lax.broadcasted_iota(jnp.int32, (CHUNK_B, 1), 0)
    if rev:
        pi, pj, pos = CHUNK_B - 1 - pi, CHUNK_B - 1 - pj, CHUNK_B - 1 - pos
    pdiff = (pi - pj).astype(F32)
    posf = pos.astype(F32)
    first_half = (lane // DK_B) == 0
    own_block = (lax.broadcasted_iota(jnp.int32, (LANES, 2 * DV_B), 0) // DK_B
                 == lax.broadcasted_iota(jnp.int32, (LANES, 2 * DV_B), 1) // DV_B)
    ncb = TB // CHUNK_B
    order_b = range(ncb - 1, -1, -1) if rev else range(ncb)
    rt = []
    for pp in range(H_B // 2):
        lgs = [_LOG_GAMMA[H_B - 1 - h] if rev else _LOG_GAMMA[h] for h in (2 * pp, 2 * pp + 1)]
        d_mask = jnp.concatenate([jnp.where(pdiff >= 0, jnp.exp(lg * jnp.maximum(pdiff, 0.0)), 0.0) for lg in lgs], 1)
        xi = jnp.where(first_half, jnp.exp(lgs[0] * (posf + 1.0)), jnp.exp(lgs[1] * (posf + 1.0)))
        zeta = jnp.where(first_half, jnp.exp(lgs[0] * (CHUNK_B - 1.0 - posf)), jnp.exp(lgs[1] * (CHUNK_B - 1.0 - posf)))
        col_head = lax.broadcasted_iota(jnp.int32, (1, 2 * DV_B), 1) // DV_B
        g_chunk = jnp.where(col_head == 0, math.exp(lgs[0] * CHUNK_B), math.exp(lgs[1] * CHUNK_B))
        for bb in range(NS):
            for n in order_b:
                rt.append(dict(bb=bb, r0=n * CHUNK_B, pp=pp, g_chunk=g_chunk, d_mask=d_mask, xi=xi, zeta=zeta))

    def retention_independent(it):
        bb, r0, pp = it["bb"], it["r0"], it["pp"]
        q = qkb_ref[bb, r0:r0 + CHUNK_B, pp * LANES:(pp + 1) * LANES]
        k = qkb_ref[bb, r0:r0 + CHUNK_B, H_B * DK_B + pp * LANES:H_B * DK_B + (pp + 1) * LANES]
        v0 = vb_ref[bb, r0:r0 + CHUNK_B, 2 * pp * DV_B:(2 * pp + 1) * DV_B]
        v1 = vb_ref[bb, r0:r0 + CHUNK_B, (2 * pp + 1) * DV_B:(2 * pp + 2) * DV_B]
        zb = jnp.zeros((CHUNK_B, DV_B), BF16)
        ksplit = jnp.concatenate([jnp.where(first_half, k, 0.0), jnp.where(first_half, 0.0, k)], 0)
        scores = _dot_nt(q, ksplit) * it["d_mask"]
        v_bd = jnp.concatenate([jnp.concatenate([v0, zb], 1), jnp.concatenate([zb, v1], 1)], 0)
        it["inner"] = jnp.dot(scores.astype(BF16), v_bd, preferred_element_type=F32)
        it["qx"] = q * it["xi"]
        ds = lax.dot_general((k * it["zeta"]).astype(BF16), jnp.concatenate([v0, v1], 1),
                             (((0,), (0,)), ((), ())), preferred_element_type=F32)
        it["ds"] = jnp.where(own_block, ds, 0.0)

    fillers = [functools.partial(retention_independent, it) for it in rt]
    inv = _cat_inverses([it["m"] for it in st], bd_mask)
    for it, t in zip(st, inv):
        bb, r0 = it["bb"], it["r0"]
        k = qkv_ref[bb, r0:r0 + CHUNK_A, D_A:2 * D_A]
        v = qkv_ref[bb, r0:r0 + CHUNK_A, 2 * D_A:3 * D_A]
        vb5 = v * it["b5"]
        kbe5 = (k * it["b5"]) * jnp.exp(it["gc5"])
        rhs = jnp.concatenate(
            [jnp.concatenate([vb5[:, h * DV_A:(h + 1) * DV_A], kbe5[:, h * DK_A:(h + 1) * DK_A]], 1)
             for h in range(H_A)], 0).astype(BF16)
        lhs = jnp.concatenate([jnp.where(lane_head == h, t, 0.0) for h in range(H_A)], 0)
        sol = jnp.dot(lhs.astype(BF16), rhs, preferred_element_type=F32)
        it["u"] = jnp.concatenate([sol[h * CHUNK_A:(h + 1) * CHUNK_A, :DV_A] for h in range(H_A)], 1)
        it["w"] = jnp.concatenate([sol[h * CHUNK_A:(h + 1) * CHUNK_A, DV_A:] for h in range(H_A)], 1)

    def finish(bb, r0, rows, c0, width, o):
        if not rev:
            o_ref[bb, r0:r0 + rows, c0:c0 + width] = o
            return
        o = o + of_ref[bb, r0:r0 + rows, c0:c0 + width]
        outs = []
        for hh in range(width // LANES):
            x = o[:, hh * LANES:(hh + 1) * LANES]
            if c0 < D_A:
                x = x * lax.rsqrt(jnp.mean(x * x, -1, keepdims=True) + 1e-6) * naw_ref[...]
            else:
                cb = c0 - D_A + hh * LANES
                x = _layer_norm(x, 1e-5) * gnw_ref[:, cb:cb + LANES] + gnb_ref[:, cb:cb + LANES]
            outs.append(x)
        o = outs[0] if len(outs) == 1 else jnp.concatenate(outs, 1)
        o_ref[bb, r0:r0 + rows, c0:c0 + width] = (o * gates_ref[bb, r0:r0 + rows, c0:c0 + width]).astype(BF16)

    per_step = -(-len(fillers) // nca)
    for c in range(nca):
        grp_items = [st[bb * nca + c] for bb in range(NS)]
        s_old = [sa_scr[it["bb"]] for it in grp_items]
        prods = []
        for it, s in zip(grp_items, s_old):
            q_dec = qkv_ref[it["bb"], it["r0"]:it["r0"] + CHUNK_A, 0:D_A] * jnp.exp(it["gc5"])
            pr = []
            for pp in range(H_A // 2):
                lo = pp * 2 * DV_A
                z = jnp.zeros((DK_A, DV_A), F32)
                bds = jnp.concatenate([jnp.concatenate([s[:, lo:lo + DV_A], z], 1),
                                       jnp.concatenate([z, s[:, lo + DV_A:lo + 2 * DV_A]], 1)], 0)
                lhs = jnp.concatenate([it["w"][:, lo:lo + 2 * DK_A], q_dec[:, lo:lo + 2 * DK_A]], 0)
                pr.append(_dot(lhs, bds))
            prods.append(jnp.concatenate(pr, 1))
        for fill in fillers[c * per_step:(c + 1) * per_step]:
            fill()
        v_new = [it["u"] - pr[:CHUNK_A] for it, pr in zip(grp_items, prods)]
        vbd = [_head_blockdiag(v, bdk_mask) for v in v_new]
        both = []
        for it, vb_ in zip(grp_items, vbd):
            k_dec = (qkv_ref[it["bb"], it["r0"]:it["r0"] + CHUNK_A, D_A:2 * D_A]
                     * jnp.exp(it["tot5"] - it["gc5"]))
            kd_stack = jnp.concatenate([k_dec[:, h * DK_A:(h + 1) * DK_A] for h in range(H_A)], 0)
            lhs = jnp.concatenate([kd_stack.T, it["qk"]], 0)
            both.append(_dot(lhs, vb_))
        for it, s, x in zip(grp_items, s_old, both):
            sa_scr[it["bb"]] = s * jnp.exp(it["tot5"]) + x[:DK_A]
        for it, pr, x in zip(grp_items, prods, both):
            finish(it["bb"], it["r0"], CHUNK_A, 0, D_A, pr[CHUNK_A:] + x[DK_A:])

    for it in rt:
        bb, pp = it["bb"], it["pp"]
        s = sr_scr[bb, pp]
        o = it["inner"] + _dot(it["qx"], s)
        sr_scr[bb, pp] = s * it["g_chunk"] + it["ds"]
        finish(bb, it["r0"], CHUNK_B, D_A + 2 * pp * DV_B, 2 * DV_B, o)

    if emit_state:
        @pl.when(j == nblk - 1)
        def _():
            for bb in range(NS):
                for h in range(H_A):
                    sd_ref[bb, 0, 0, h] = sa_scr[bb, :, h * DV_A:(h + 1) * DV_A]
                for h in range(H_B):
                    lo = (h % 2) * DK_B
                    sr_ref[bb, 0, 0, h] = sr_scr[bb, h // 2, lo:lo + DK_B, (h % 2) * DV_B:(h % 2 + 1) * DV_B]


def _mixer(rev, qkv, qkb, vb, gb, o_fwd, gates, naw, gnw, gnb, sd0, sr0, emit_state):
    bt, t, _ = qkv.shape
    NS = SEQ_PER_STEP
    TB = TIME_BLOCK
    nblk = t // TB
    has_init = sd0 is not None
    d = 1 if rev else 0
    tb_of = (lambda j: nblk - 1 - j) if rev else (lambda j: j)
    tok = lambda w: pl.BlockSpec((NS, TB, w), lambda b, j: (b, tb_of(j), 0))
    in_specs = [tok(QKV_W), tok(QKB_W), tok(D_B), tok(LANES)]
    args = [qkv, qkb, vb, gb]
    if rev:
        in_specs += [tok(D_A + D_B), tok(D_A + D_B), _resident((1, DV_A)), _resident((1, D_B)), _resident((1, D_B))]
        args += [o_fwd, gates, naw, gnw, gnb]
    if has_init:
        in_specs.append(pl.BlockSpec((NS, 1, 1, H_A, DK_A, DV_A), lambda b, j: (b, 0, d, 0, 0, 0)))
        in_specs.append(pl.BlockSpec((NS, 1, 1, H_B, DK_B, DV_B), lambda b, j: (b, 0, d, 0, 0, 0)))
        args += [sd0, sr0]
    out_specs = [tok(D_A + D_B)]
    out_shape = [jax.ShapeDtypeStruct((bt, t, D_A + D_B), BF16 if rev else F32)]
    if emit_state:
        out_specs.append(pl.BlockSpec((NS, 1, 1, H_A, DK_A, DV_A), lambda b, j: (b, 0, 0, 0, 0, 0)))
        out_specs.append(pl.BlockSpec((NS, 1, 1, H_B, DK_B, DV_B), lambda b, j: (b, 0, 0, 0, 0, 0)))
        out_shape.append(jax.ShapeDtypeStruct((bt, DEPTH, 1, H_A, DK_A, DV_A), F32))
        out_shape.append(jax.ShapeDtypeStruct((bt, DEPTH, 1, H_B, DK_B, DV_B), F32))
    scratch = [pltpu.VMEM((NS, DK_A, H_A * DV_A), F32),
               pltpu.VMEM((NS, H_B // 2, LANES, 2 * DV_B), F32)]
    return pl.pallas_call(
        functools.partial(_mixer_kernel, rev, has_init, emit_state, nblk),
        grid=(bt // NS, nblk),
        in_specs=in_specs,
        out_specs=out_specs,
        out_shape=out_shape,
        scratch_shapes=scratch,
        compiler_params=pltpu.CompilerParams(dimension_semantics=("arbitrary", "arbitrary"),
                                             vmem_limit_bytes=VMEM_LIMIT),
        name="mixer_bwd" if rev else "mixer_fwd",
    )(*args)


def _rope_tables(t):
    rows = t // GRID_W
    r = np.repeat(np.arange(rows, dtype=np.float32), GRID_W)
    col = np.tile(np.arange(GRID_W, dtype=np.float32), rows)
    nf = DK_B // 4
    inv = (np.float32(ROPE_BASE) ** (-np.arange(nf, dtype=np.float32) / np.float32(nf))).astype(np.float32)
    ang = jnp.asarray(np.concatenate([r[:, None] * inv, col[:, None] * inv], -1).astype(np.float32))
    cos, sin = jnp.cos(ang), jnp.sin(ang)
    zero = jnp.zeros_like(sin)
    cos_t = jnp.tile(cos, (1, 4))
    sin_up = jnp.tile(jnp.concatenate([-sin, zero], -1), (1, 2))
    sin_dn = jnp.tile(jnp.concatenate([zero, sin], -1), (1, 2))
    return jnp.stack([cos_t, sin_up, sin_dn], 0)


def kernel(x_prompt, x_sample, c, state_delta, state_ret, c_ctx, w_mod, b_mod, w_in, conv_w, a_log, dt_bias,
           norm_a_w, gn_w, gn_b, w_o, ln1_w, ln1_b, w_ff1, b_ff1, w_ff2, b_ff2, ln2_w, ln2_b):
    assert w_mod.shape[0] == DEPTH == 1
    n_dec = c.shape[0]
    rows = -(-(1 + n_dec) // SUBLANES) * SUBLANES
    cond = jnp.zeros((rows, D_MODEL), F32).at[0].set(c_ctx).at[1:1 + n_dec].set(c)
    mod = _modulation(cond, w_mod[0], b_mod[0])
    mod3 = mod.reshape(rows, 1, 6 * D_MODEL)

    w = w_in[0]
    c_ab = QKV_W + D_A
    w_in_p = (jnp.zeros((D_MODEL, P_W), BF16)
              .at[:, :c_ab].set(w[:, :c_ab].astype(BF16))
              .at[:, c_ab:COL_AB].set(w[:, c_ab + 2 * N_AB:].astype(BF16))
              .at[:, COL_AB:COL_AB + 2 * N_AB].set(w[:, c_ab:c_ab + 2 * N_AB].astype(BF16)))
    convw = jnp.zeros((SUBLANES, QKV_W), F32).at[:CONV_K].set(conv_w[0])
    par = (jnp.zeros((SUBLANES, LANES), F32).at[0, :N_AB].set(a_log[0].reshape(-1))
           .at[1, :N_AB].set(dt_bias[0].reshape(-1)))
    naw = norm_a_w[0].reshape(1, DV_A)
    gnw = gn_w[0].reshape(1, D_B)
    gnb = gn_b[0].reshape(1, D_B)
    wo = w_o[0].astype(BF16)
    w1 = w_ff1[0].astype(BF16)
    w2 = w_ff2[0].astype(BF16)
    row = lambda v: v[0].reshape(1, -1)

    def trunk(x, row0, row_stride, rope, sd0, sr0, emit_state, tm):
        qkv, qkb, vb, gates, gb = _projection(x, mod3, row0, row_stride, w_in_p, convw, par, rope, tm)
        fwd = _mixer(False, qkv, qkb, vb, gb, None, None, None, None, None, sd0, sr0, emit_state)
        bwd = _mixer(True, qkv, qkb, vb, gb, fwd[0], gates, naw, gnw, gnb, sd0, sr0, emit_state)
        o = bwd[0]
        if row_stride == 0:
            y = _post(o.reshape(1, -1, D_A + D_B), x.reshape(1, -1, D_MODEL), mod3, row0, 0, wo, row(ln1_w),
                      row(ln1_b), w1, row(b_ff1), w2, row(b_ff2), row(ln2_w), row(ln2_b), POST_TILE).reshape(x.shape)
        else:
            y = _post(o, x, mod3, row0, row_stride, wo, row(ln1_w), row(ln1_b), w1, row(b_ff1), w2, row(b_ff2),
                      row(ln2_w), row(ln2_b), POST_TILE)
        states = [jnp.concatenate([f, b], 2) for f, b in zip(fwd[1:], bwd[1:])]
        return y, states

    y_prompt, (new_sd, new_sr) = trunk(x_prompt, 0, 0, None, None, None, True, 256)
    y_sample, _ = trunk(x_sample, 1, 1, _rope_tables(x_sample.shape[1]), state_delta, state_ret, False, 512)
    return y_prompt, y_sample, new_sd, new_sr
```

```python
import functools
import math

import jax
import jax.numpy as jnp
import numpy as np
from jax import lax
from jax.experimental import pallas as pl
from jax.experimental.pallas import tpu as pltpu

F32 = jnp.float32
BF16 = jnp.bfloat16

D_MODEL = 1024
H_A, DK_A, DV_A = 4, 128, 128
D_A = H_A * DV_A
CONV_K = 5
CHUNK_A = 64
H_B, DK_B, DV_B = 4, 64, 128
D_B = H_B * DV_B
CHUNK_B = 128
GRID_W = 64
ROPE_BASE = 10000.0
D_FF = 4 * D_MODEL
DEPTH = 1
ALPHA = (2.0 * DEPTH) ** 0.25

LANES = 128
SUBLANES = 8
VMEM_LIMIT = 56 * 1024 * 1024

QKV_W = 3 * D_A
COL_GATE_A = QKV_W
COL_QK_B = COL_GATE_A + D_A
QKB_W = 2 * H_B * DK_B
COL_V_B = COL_QK_B + QKB_W
COL_GATE_B = COL_V_B + D_B
COL_AB = COL_GATE_B + D_B
P_W = COL_AB + LANES
N_AB = 2 * H_A

TIME_BLOCK = 256
HALO = SUBLANES
PROJ_GROUP = 256
assert CONV_K == 5 and (CONV_K - 1) // 2 <= HALO


def _dot(a, b):
    return jnp.dot(a.astype(BF16), b.astype(BF16), preferred_element_type=F32)


def _dot_nt(a, b):
    return lax.dot_general(a.astype(BF16), b.astype(BF16), (((1,), (1,)), ((), ())),
                           preferred_element_type=F32)


def _dot_tn(a, b):
    return lax.dot_general(a.astype(BF16), b.astype(BF16), (((0,), (0,)), ((), ())),
                           preferred_element_type=F32)


def _split3(x):
    hi = x.astype(BF16)
    r = x - hi.astype(F32)
    mid = r.astype(BF16)
    lo = (r - mid.astype(F32)).astype(BF16)
    return hi, mid, lo


def _silu(x):
    h = 0.5 * x
    return h + h * jnp.tanh(h)


def _layer_norm(x, eps):
    mu = jnp.mean(x, -1, keepdims=True)
    xc = x - mu
    var = jnp.mean(xc * xc, -1, keepdims=True)
    return xc * lax.rsqrt(var + eps)


def _resident(shape):
    return pl.BlockSpec(shape, lambda *_: (0,) * len(shape), pipeline_mode=pl.Buffered(1))


def _mod_kernel(c_ref, w_ref, b_ref, o_ref):
    o_ref[...] = _dot(_silu(c_ref[...]), w_ref[...]) + b_ref[...]


def _modulation(cond, w_mod, b_mod):
    rows = cond.shape[0]
    n = w_mod.shape[1]
    bn = 1536
    return pl.pallas_call(
        _mod_kernel,
        grid=(n // bn,),
        in_specs=[pl.BlockSpec((rows, D_MODEL), lambda j: (0, 0)),
                  pl.BlockSpec((D_MODEL, bn), lambda j: (0, j)),
                  pl.BlockSpec((1, bn), lambda j: (0, j))],
        out_specs=pl.BlockSpec((rows, bn), lambda j: (0, j)),
        out_shape=jax.ShapeDtypeStruct((rows, n), F32),
        compiler_params=pltpu.CompilerParams(dimension_semantics=("arbitrary",)),
        name="modulation",
    )(cond, w_mod, b_mod.reshape(1, n))


def _proj_kernel(use_rope, tm, nt, *refs):
    if use_rope:
        (x_ref, xp_ref, xn_ref, mod_ref, w_ref, convw_ref, par_ref, rope_ref,
         qkv_ref, qkb_ref, vb_ref, gates_ref, gb_ref) = refs
    else:
        (x_ref, xp_ref, xn_ref, mod_ref, w_ref, convw_ref, par_ref,
         qkv_ref, qkb_ref, vb_ref, gates_ref, gb_ref) = refs
    i = pl.program_id(1)
    m = mod_ref[0]
    sh1 = m[:, 0:D_MODEL]
    sc1 = m[:, D_MODEL:2 * D_MODEL]

    def modulated(x):
        return _layer_norm(x, 1e-6) * (1.0 + sc1) + sh1

    hb = jnp.concatenate([modulated(xp_ref[0]), modulated(x_ref[0]), modulated(xn_ref[0])], 0).astype(BF16)
    rows = tm + 2 * HALO

    def project(c0, width):
        return jnp.dot(hb, w_ref[:, c0:c0 + width], preferred_element_type=F32)

    def conv_silu(pc, c0):
        xe = jnp.concatenate([jnp.where(i > 0, pc[0:HALO], 0.0), pc[HALO:HALO + tm],
                              jnp.where(i < nt - 1, pc[HALO + tm:], 0.0)], 0)
        z = [xe * convw_ref[kk:kk + 1, c0:c0 + pc.shape[1]] for kk in range(CONV_K)]
        back = pltpu.roll(z[1] + pltpu.roll(z[0], 1, 0), 1, 0)
        ahead = pltpu.roll(z[3] + pltpu.roll(z[4], rows - 1, 0), rows - 1, 0)
        return _silu((z[2] + back + ahead)[HALO:HALO + tm])

    def l2n(x):
        return x * lax.rsqrt(jnp.sum(x * x, -1, keepdims=True) + 1e-6)

    def finish_qkv(pc, c0):
        y = conv_silu(pc, c0)
        for hh in range(pc.shape[1] // DK_A):
            c = c0 + hh * DK_A
            x = y[:, hh * DK_A:(hh + 1) * DK_A]
            if c < D_A:
                x = l2n(x) * (DK_A ** -0.5)
            elif c < 2 * D_A:
                x = l2n(x)
            qkv_ref[0, :, c:c + DK_A] = x

    def finish_rest(pc, c0):
        pc = pc[HALO:HALO + tm]
        w = pc.shape[1]
        if c0 < COL_QK_B:
            gates_ref[0, :, c0 - COL_GATE_A:c0 - COL_GATE_A + w] = _silu(pc)
        elif c0 < COL_V_B:
            for hh in range(w // LANES):
                x = pc[:, hh * LANES:(hh + 1) * LANES]
                if use_rope:
                    x = (x * rope_ref[0] + pltpu.roll(x, LANES - DK_B // 2, 1) * rope_ref[1]
                         + pltpu.roll(x, DK_B // 2, 1) * rope_ref[2])
                c = c0 - COL_QK_B + hh * LANES
                if c < H_B * DK_B:
                    x = x * (DK_B ** -0.5)
                qkb_ref[0, :, c:c + LANES] = x
        elif c0 < COL_GATE_B:
            vb_ref[0, :, c0 - COL_V_B:c0 - COL_V_B + w] = pc.astype(BF16)
        elif c0 < COL_AB:
            gates_ref[0, :, D_A + c0 - COL_GATE_B:D_A + c0 - COL_GATE_B + w] = _silu(pc)
        else:
            z = pc + par_ref[1:2, :]
            softplus = jnp.maximum(z, 0.0) + jnp.log1p(jnp.exp(-jnp.abs(z)))
            g_all = -jnp.exp(par_ref[0:1, :]) * softplus
            beta_all = 1.0 / (1.0 + jnp.exp(-pc))
            lane = lax.broadcasted_iota(jnp.int32, (tm, LANES), 1)
            gb_ref[0] = jnp.where(lane < N_AB, g_all, jnp.where(lane < 2 * N_AB, beta_all, 0.0))

    heavy = [(finish_qkv, c) for c in range(0, QKV_W, PROJ_GROUP)]
    light = [(finish_rest, c) for c in range(QKV_W, P_W, PROJ_GROUP)]
    order = []
    while heavy or light:
        if heavy:
            order.append(heavy.pop(0))
        if light:
            order.append(light.pop(0))
    previous = None
    for fn, c0 in order:
        pc = project(c0, min(PROJ_GROUP, P_W - c0))
        if previous is not None:
            previous[0](previous[1], previous[2])
        previous = (fn, pc, c0)
    previous[0](previous[1], previous[2])


def _projection(x, mod3, row0, row_stride, w_in_p, convw, par, rope, tm):
    bt, t, _ = x.shape
    nt = t // tm
    hpt = tm // HALO
    nh = t // HALO
    use_rope = rope is not None
    in_specs = [pl.BlockSpec((1, tm, D_MODEL), lambda b, i: (b, i, 0)),
                pl.BlockSpec((1, HALO, D_MODEL), lambda b, i: (b, jnp.maximum(i * hpt - 1, 0), 0)),
                pl.BlockSpec((1, HALO, D_MODEL), lambda b, i: (b, jnp.minimum((i + 1) * hpt, nh - 1), 0)),
                pl.BlockSpec((1, 1, 6 * D_MODEL), lambda b, i: (row0 + row_stride * b, 0, 0)),
                _resident((D_MODEL, P_W)),
                _resident((SUBLANES, QKV_W)),
                _resident((SUBLANES, LANES))]
    args = [x, x, x, mod3, w_in_p, convw, par]
    if use_rope:
        in_specs.append(pl.BlockSpec((3, tm, LANES), lambda b, i: (0, i, 0)))
        args.append(rope)
    widths = (QKV_W, QKB_W, D_B, D_A + D_B, LANES)
    dtypes = (F32, F32, BF16, F32, F32)
    return pl.pallas_call(
        functools.partial(_proj_kernel, use_rope, tm, nt),
        grid=(bt, nt),
        in_specs=in_specs,
        out_specs=[pl.BlockSpec((1, tm, w), lambda b, i: (b, i, 0)) for w in widths],
        out_shape=[jax.ShapeDtypeStruct((bt, t, w), dt) for w, dt in zip(widths, dtypes)],
        compiler_params=pltpu.CompilerParams(dimension_semantics=("arbitrary", "arbitrary"),
                                             vmem_limit_bytes=VMEM_LIMIT),
        name="projection",
    )(*args)


POST_SPLIT = 2
POST_TILE = 512


def _post_kernel(o_ref, x_ref, mod_ref, wo_ref, ln1w_ref, ln1b_ref, w1_ref, b1_ref, w2_ref, b2_ref,
                 ln2w_ref, ln2b_ref, y_ref):
    m = mod_ref[0]
    g1 = m[:, 2 * D_MODEL:3 * D_MODEL]
    sh2 = m[:, 3 * D_MODEL:4 * D_MODEL]
    sc2 = m[:, 4 * D_MODEL:5 * D_MODEL]
    g2 = m[:, 5 * D_MODEL:6 * D_MODEL]
    rows = o_ref.shape[1] // POST_SPLIT
    parts = [slice(r * rows, (r + 1) * rows) for r in range(POST_SPLIT)]
    ys = [jnp.dot(o_ref[0, sl], wo_ref[...], preferred_element_type=F32) for sl in parts]
    x1s = [_layer_norm(ALPHA * x_ref[0, sl] + g1 * y, 1e-6) * ln1w_ref[...] + ln1b_ref[...]
           for sl, y in zip(parts, ys)]
    hs = [(_layer_norm(x1, 1e-6) * (1.0 + sc2) + sh2).astype(BF16) for x1 in x1s]
    acts = [jnp.square(jnp.maximum(jnp.dot(h, w1_ref[...], preferred_element_type=F32) + b1_ref[...], 0.0))
            .astype(BF16) for h in hs]
    fs = [jnp.dot(a, w2_ref[...], preferred_element_type=F32) + b2_ref[...] for a in acts]
    for sl, x1, f in zip(parts, x1s, fs):
        y_ref[0, sl] = _layer_norm(ALPHA * x1 + g2 * f, 1e-6) * ln2w_ref[...] + ln2b_ref[...]


def _post(o, x, mod3, row0, row_stride, wo, ln1w, ln1b, w1, b1, w2, b2, ln2w, ln2b, tm):
    bt, t, _ = x.shape
    return pl.pallas_call(
        _post_kernel,
        grid=(bt, t // tm),
        in_specs=[pl.BlockSpec((1, tm, D_MODEL), lambda b, i: (b, i, 0)),
                  pl.BlockSpec((1, tm, D_MODEL), lambda b, i: (b, i, 0)),
                  pl.BlockSpec((1, 1, 6 * D_MODEL), lambda b, i: (row0 + row_stride * b, 0, 0)),
                  _resident((D_MODEL, D_MODEL)),
                  _resident((1, D_MODEL)),
                  _resident((1, D_MODEL)),
                  _resident((D_MODEL, D_FF)),
                  _resident((1, D_FF)),
                  _resident((D_FF, D_MODEL)),
                  _resident((1, D_MODEL)),
                  _resident((1, D_MODEL)),
                  _resident((1, D_MODEL))],
        out_specs=pl.BlockSpec((1, tm, D_MODEL), lambda b, i: (b, i, 0)),
        out_shape=jax.ShapeDtypeStruct((bt, t, D_MODEL), F32),
        compiler_params=pltpu.CompilerParams(dimension_semantics=("arbitrary", "arbitrary"),
                                             vmem_limit_bytes=VMEM_LIMIT),
        name="post",
    )(o, x, mod3, wo, ln1w, ln1b, w1, b1, w2, b2, ln2w, ln2b)


_LOG_GAMMA = [math.log1p(-(2.0 ** (-5.0 - h))) for h in range(H_B)]

SOLVE_BASE = 8
SEQ_PER_STEP = 4
CAT_W = H_A * CHUNK_A

def _head_blockdiag(x, mask):
    return jnp.where(mask, jnp.concatenate([x] * H_A, 0), 0.0).astype(BF16)


def _cat_inverses(ms, bd_mask):
    i = lax.broadcasted_iota(jnp.int32, (CHUNK_A, CAT_W), 0)
    j = lax.broadcasted_iota(jnp.int32, (CHUNK_A, CAT_W), 1) % CHUNK_A
    same = lambda s: (i // s) == (j // s)
    eye = jnp.where(i == j, 1.0, 0.0)
    mul = lambda a, bd: jnp.dot(a.astype(BF16), bd, preferred_element_type=F32)
    ps = [jnp.where(same(SOLVE_BASE), -m, 0.0) for m in ms]
    ts = [eye + p for p in ps]
    bds = [_head_blockdiag(p, bd_mask) for p in ps]
    ps = [mul(p, bd) for p, bd in zip(ps, bds)]
    bds = [_head_blockdiag(p, bd_mask) for p in ps]
    for _ in range(int(math.log2(SOLVE_BASE)) - 2):
        both = [mul(jnp.concatenate([t, p], 0), bd) for t, p, bd in zip(ts, ps, bds)]
        ts = [t + x[:CHUNK_A] for t, x in zip(ts, both)]
        ps = [x[CHUNK_A:] for x in both]
        bds = [_head_blockdiag(p, bd_mask) for p in ps]
    ts = [t + mul(t, bd) for t, bd in zip(ts, bds)]
    s = SOLVE_BASE
    while s < CHUNK_A:
        sel = same(2 * s) & jnp.logical_not(same(s))
        bds = [_head_blockdiag(t, bd_mask) for t in ts]
        ys = [mul(jnp.where(sel, m, 0.0), bd) for m, bd in zip(ms, bds)]
        bds = [_head_blockdiag(y, bd_mask) for y in ys]
        ts = [t - mul(t, bd) for t, bd in zip(ts, bds)]
        s *= 2
    return ts


def _mixer_kernel(rev, has_init, emit_state, nblk, *refs):
    NS = SEQ_PER_STEP
    TB = TIME_BLOCK
    d = 1 if rev else 0
    j = pl.program_id(1)
    qkv_ref, qkb_ref, vb_ref, gb_ref = refs[:4]
    pos_ = 4
    if rev:
        of_ref, gates_ref, naw_ref, gnw_ref, gnb_ref = refs[pos_:pos_ + 5]
        pos_ += 5
    if has_init:
        sd0_ref, sr0_ref = refs[pos_:pos_ + 2]
        pos_ += 2
    o_ref = refs[pos_]
    pos_ += 1
    if emit_state:
        sd_ref, sr_ref = refs[pos_:pos_ + 2]
        pos_ += 2
    sa_scr, sr_scr = refs[pos_:pos_ + 2]

    @pl.when(j == 0)
    def _():
        if has_init:
            for bb in range(NS):
                for h in range(H_A):
                    sa_scr[bb, :, h * DV_A:(h + 1) * DV_A] = sd0_ref[bb, 0, 0, h]
                sr_scr[bb] = jnp.zeros(sr_scr.shape[1:], F32)
                for h in range(H_B):
                    lo = (h % 2) * DK_B
                    sr_scr[bb, h // 2, lo:lo + DK_B, (h % 2) * DV_B:(h % 2 + 1) * DV_B] = sr0_ref[bb, 0, 0, h]
        else:
            sa_scr[...] = jnp.zeros(sa_scr.shape, F32)
            sr_scr[...] = jnp.zeros(sr_scr.shape, F32)

    ri = lax.broadcasted_iota(jnp.int32, (TB, TB), 0)
    ci = lax.broadcasted_iota(jnp.int32, (TB, TB), 1)
    cum = jnp.where(((ri // CHUNK_A) == (ci // CHUNK_A)) & ((ci >= ri) if rev else (ci <= ri)),
                    1.0, 0.0).astype(BF16)
    er = lax.broadcasted_iota(jnp.int32, (LANES, CAT_W), 0)
    ec = lax.broadcasted_iota(jnp.int32, (LANES, CAT_W), 1)
    esel_c = jnp.where(er == d * H_A + ec // CHUNK_A, 1.0, 0.0).astype(BF16)
    lane_tb = lax.broadcasted_iota(jnp.int32, (TB, LANES), 1)
    i64 = lax.broadcasted_iota(jnp.int32, (CHUNK_A, CAT_W), 0)
    j64 = lax.broadcasted_iota(jnp.int32, (CHUNK_A, CAT_W), 1) % CHUNK_A
    tri = (i64 <= j64) if rev else (i64 >= j64)
    strict = (i64 < j64) if rev else (i64 > j64)
    bd_mask = (lax.broadcasted_iota(jnp.int32, (CAT_W, CAT_W), 0) // CHUNK_A
               == lax.broadcasted_iota(jnp.int32, (CAT_W, CAT_W), 1) // CHUNK_A)
    bdk_mask = (lax.broadcasted_iota(jnp.int32, (CAT_W, D_A), 0) // CHUNK_A
                == lax.broadcasted_iota(jnp.int32, (CAT_W, D_A), 1) // DK_A)
    lane_head = lax.broadcasted_iota(jnp.int32, (CHUNK_A, CAT_W), 1) // CHUNK_A
    nca = TB // CHUNK_A
    order_a = list(range(nca - 1, -1, -1)) if rev else list(range(nca))

    st = []
    for bb in range(NS):
        gbk = gb_ref[bb]
        g3 = jnp.dot(cum, jnp.concatenate(_split3(gbk), 1), preferred_element_type=F32)
        gc = (g3[:, 0:LANES] + g3[:, LANES:2 * LANES]) + g3[:, 2 * LANES:]
        rt3 = 2 * SUBLANES
        t3 = lax.dot_general(jnp.concatenate(_split3(gbk.T[0:rt3]), 0), cum, (((1,), (1,)), ((), ())),
                             preferred_element_type=F32)
        gct = (t3[0:rt3] + t3[rt3:2 * rt3]) + t3[2 * rt3:]
        gcb = jnp.where(lane_tb < N_AB, gc, gbk)
        s3 = jnp.dot(jnp.concatenate(_split3(gcb), 0), esel_c, preferred_element_type=F32)
        gcs = (s3[0:TB] + s3[TB:2 * TB]) + s3[2 * TB:]
        gc5a = jnp.concatenate([jnp.broadcast_to(gcb[:, d * H_A + h:d * H_A + h + 1], (TB, DK_A))
                                for h in range(H_A)], 1)
        b5a = jnp.concatenate([jnp.broadcast_to(gcb[:, N_AB + d * H_A + h:N_AB + d * H_A + h + 1], (TB, DK_A))
                               for h in range(H_A)], 1)
        for n in order_a:
            r0 = n * CHUNK_A
            last = r0 if rev else r0 + CHUNK_A - 1
            it = dict(bb=bb, r0=r0, gc5=gc5a[r0:r0 + CHUNK_A], tot5=gc5a[last:last + 1], b5=b5a[r0:r0 + CHUNK_A])
            q = qkv_ref[bb, r0:r0 + CHUNK_A, 0:D_A]
            k = qkv_ref[bb, r0:r0 + CHUNK_A, D_A:2 * D_A]
            gcc = gcs[r0:r0 + CHUNK_A]
            gcr = jnp.concatenate([gct[d * H_A + h:d * H_A + h + 1, r0:r0 + CHUNK_A] for h in range(H_A)], 1)
            decay = jnp.where(tri, jnp.exp(jnp.where(tri, gcc - gcr, 0.0)), 0.0)
            mq = _dot_nt(jnp.concatenate([k * it["b5"], q], 0), _head_blockdiag(k, bdk_mask))
            it["m"] = jnp.where(strict, mq[:CHUNK_A] * decay, 0.0)
            it["qk"] = jnp.where(tri, mq[CHUNK_A:] * decay, 0.0)
            st.append(it)
    lane = lax.broadcasted_iota(jnp.int32, (CHUNK_B, LANES), 1)
    pi = lax.broadcasted_iota(jnp.int32, (CHUNK_B, CHUNK_B), 0)
    pj = lax.broadcasted_iota(jnp.int32, (CHUNK_B, CHUNK_B), 1)
    pos = lax.broadcasted_iota(jnp.int32, (CHUNK_B, 1), 0)
    if rev:
        pi, pj, pos = CHUNK_B - 1 - pi, CHUNK_B - 1 - pj, CHUNK_B - 1 - pos
    pdiff = (pi - pj).astype(F32)
    posf = pos.astype(F32)
    first_half = (lane // DK_B) == 0
    own_block = (lax.broadcasted_iota(jnp.int32, (LANES, 2 * DV_B), 0) // DK_B
                 == lax.broadcasted_iota(jnp.int32, (LANES, 2 * DV_B), 1) // DV_B)
    ncb = TB // CHUNK_B
    order_b = range(ncb - 1, -1, -1) if rev else range(ncb)
    rt = []
    for pp in range(H_B // 2):
        lgs = [_LOG_GAMMA[H_B - 1 - h] if rev else _LOG_GAMMA[h] for h in (2 * pp, 2 * pp + 1)]
        d_mask = jnp.concatenate([jnp.where(pdiff >= 0, jnp.exp(lg * jnp.maximum(pdiff, 0.0)), 0.0) for lg in lgs], 1)
        xi = jnp.where(first_half, jnp.exp(lgs[0] * (posf + 1.0)), jnp.exp(lgs[1] * (posf + 1.0)))
        zeta = jnp.where(first_half, jnp.exp(lgs[0] * (CHUNK_B - 1.0 - posf)), jnp.exp(lgs[1] * (CHUNK_B - 1.0 - posf)))
        col_head = lax.broadcasted_iota(jnp.int32, (1, 2 * DV_B), 1) // DV_B
        g_chunk = jnp.where(col_head == 0, math.exp(lgs[0] * CHUNK_B), math.exp(lgs[1] * CHUNK_B))
        for bb in range(NS):
            for n in order_b:
                rt.append(dict(bb=bb, r0=n * CHUNK_B, pp=pp, g_chunk=g_chunk, d_mask=d_mask, xi=xi, zeta=zeta))

    def retention_independent(it):
        bb, r0, pp = it["bb"], it["r0"], it["pp"]
        q = qkb_ref[bb, r0:r0 + CHUNK_B, pp * LANES:(pp + 1) * LANES]
        k = qkb_ref[bb, r0:r0 + CHUNK_B, H_B * DK_B + pp * LANES:H_B * DK_B + (pp + 1) * LANES]
        v0 = vb_ref[bb, r0:r0 + CHUNK_B, 2 * pp * DV_B:(2 * pp + 1) * DV_B]
        v1 = vb_ref[bb, r0:r0 + CHUNK_B, (2 * pp + 1) * DV_B:(2 * pp + 2) * DV_B]
        zb = jnp.zeros((CHUNK_B, DV_B), BF16)
        ksplit = jnp.concatenate([jnp.where(first_half, k, 0.0), jnp.where(first_half, 0.0, k)], 0)
        scores = _dot_nt(q, ksplit) * it["d_mask"]
        v_bd = jnp.concatenate([jnp.concatenate([v0, zb], 1), jnp.concatenate([zb, v1], 1)], 0)
        it["inner"] = jnp.dot(scores.astype(BF16), v_bd, preferred_element_type=F32)
        it["qx"] = q * it["xi"]
        ds = lax.dot_general((k * it["zeta"]).astype(BF16), jnp.concatenate([v0, v1], 1),
                             (((0,), (0,)), ((), ())), preferred_element_type=F32)
        it["ds"] = jnp.where(own_block, ds, 0.0)

    fillers = [functools.partial(retention_independent, it) for it in rt]
    inv = _cat_inverses([it["m"] for it in st], bd_mask)
    for it, t in zip(st, inv):
        bb, r0 = it["bb"], it["r0"]
        k = qkv_ref[bb, r0:r0 + CHUNK_A, D_A:2 * D_A]
        v = qkv_ref[bb, r0:r0 + CHUNK_A, 2 * D_A:3 * D_A]
        vb5 = v * it["b5"]
        kbe5 = (k * it["b5"]) * jnp.exp(it["gc5"])
        rhs = jnp.concatenate(
            [jnp.concatenate([vb5[:, h * DV_A:(h + 1) * DV_A], kbe5[:, h * DK_A:(h + 1) * DK_A]], 1)
             for h in range(H_A)], 0).astype(BF16)
        lhs = jnp.concatenate([jnp.where(lane_head == h, t, 0.0) for h in range(H_A)], 0)
        sol = jnp.dot(lhs.astype(BF16), rhs, preferred_element_type=F32)
        it["u"] = jnp.concatenate([sol[h * CHUNK_A:(h + 1) * CHUNK_A, :DV_A] for h in range(H_A)], 1)
        it["w"] = jnp.concatenate([sol[h * CHUNK_A:(h + 1) * CHUNK_A, DV_A:] for h in range(H_A)], 1)

    def finish(bb, r0, rows, c0, width, o):
        if not rev:
            o_ref[bb, r0:r0 + rows, c0:c0 + width] = o
            return
        o = o + of_ref[bb, r0:r0 + rows, c0:c0 + width]
        outs = []
        for hh in range(width // LANES):
            x = o[:, hh * LANES:(hh + 1) * LANES]
            if c0 < D_A:
                x = x * lax.rsqrt(jnp.mean(x * x, -1, keepdims=True) + 1e-6) * naw_ref[...]
            else:
                cb = c0 - D_A + hh * LANES
                x = _layer_norm(x, 1e-5) * gnw_ref[:, cb:cb + LANES] + gnb_ref[:, cb:cb + LANES]
            outs.append(x)
        o = outs[0] if len(outs) == 1 else jnp.concatenate(outs, 1)
        o_ref[bb, r0:r0 + rows, c0:c0 + width] = (o * gates_ref[bb, r0:r0 + rows, c0:c0 + width]).astype(BF16)

    per_step = -(-len(fillers) // nca)
    for c in range(nca):
        grp_items = [st[bb * nca + c] for bb in range(NS)]
        s_old = [sa_scr[it["bb"]] for it in grp_items]
        prods = []
        for it, s in zip(grp_items, s_old):
            q_dec = qkv_ref[it["bb"], it["r0"]:it["r0"] + CHUNK_A, 0:D_A] * jnp.exp(it["gc5"])
            pr = []
            for pp in range(H_A // 2):
                lo = pp * 2 * DV_A
                z = jnp.zeros((DK_A, DV_A), F32)
                bds = jnp.concatenate([jnp.concatenate([s[:, lo:lo + DV_A], z], 1),
                                       jnp.concatenate([z, s[:, lo + DV_A:lo + 2 * DV_A]], 1)], 0)
                lhs = jnp.concatenate([it["w"][:, lo:lo + 2 * DK_A], q_dec[:, lo:lo + 2 * DK_A]], 0)
                pr.append(_dot(lhs, bds))
            prods.append(jnp.concatenate(pr, 1))
        for fill in fillers[c * per_step:(c + 1) * per_step]:
            fill()
        v_new = [it["u"] - pr[:CHUNK_A] for it, pr in zip(grp_items, prods)]
        vbd = [_head_blockdiag(v, bdk_mask) for v in v_new]
        both = []
        for it, vb_ in zip(grp_items, vbd):
            k_dec = (qkv_ref[it["bb"], it["r0"]:it["r0"] + CHUNK_A, D_A:2 * D_A]
                     * jnp.exp(it["tot5"] - it["gc5"]))
            kd_stack = jnp.concatenate([k_dec[:, h * DK_A:(h + 1) * DK_A] for h in range(H_A)], 0)
            lhs = jnp.concatenate([kd_stack.T, it["qk"]], 0)
            both.append(_dot(lhs, vb_))
        for it, s, x in zip(grp_items, s_old, both):
            sa_scr[it["bb"]] = s * jnp.exp(it["tot5"]) + x[:DK_A]
        for it, pr, x in zip(grp_items, prods, both):
            finish(it["bb"], it["r0"], CHUNK_A, 0, D_A, pr[CHUNK_A:] + x[DK_A:])

    for it in rt:
        bb, pp = it["bb"], it["pp"]
        s = sr_scr[bb, pp]
        o = it["inner"] + _dot(it["qx"], s)
        sr_scr[bb, pp] = s * it["g_chunk"] + it["ds"]
        finish(bb, it["r0"], CHUNK_B, D_A + 2 * pp * DV_B, 2 * DV_B, o)

    if emit_state:
        @pl.when(j == nblk - 1)
        def _():
            for bb in range(NS):
                for h in range(H_A):
                    sd_ref[bb, 0, 0, h] = sa_scr[bb, :, h * DV_A:(h + 1) * DV_A]
                for h in range(H_B):
                    lo = (h % 2) * DK_B
                    sr_ref[bb, 0, 0, h] = sr_scr[bb, h // 2, lo:lo + DK_B, (h % 2) * DV_B:(h % 2 + 1) * DV_B]


def _mixer(rev, qkv, qkb, vb, gb, o_fwd, gates, naw, gnw, gnb, sd0, sr0, emit_state):
    bt, t, _ = qkv.shape
    NS = SEQ_PER_STEP
    TB = TIME_BLOCK
    nblk = t // TB
    has_init = sd0 is not None
    d = 1 if rev else 0
    tb_of = (lambda j: nblk - 1 - j) if rev else (lambda j: j)
    tok = lambda w: pl.BlockSpec((NS, TB, w), lambda b, j: (b, tb_of(j), 0))
    in_specs = [tok(QKV_W), tok(QKB_W), tok(D_B), tok(LANES)]
    args = [qkv, qkb, vb, gb]
    if rev:
        in_specs += [tok(D_A + D_B), tok(D_A + D_B), _resident((1, DV_A)), _resident((1, D_B)), _resident((1, D_B))]
        args += [o_fwd, gates, naw, gnw, gnb]
    if has_init:
        in_specs.append(pl.BlockSpec((NS, 1, 1, H_A, DK_A, DV_A), lambda b, j: (b, 0, d, 0, 0, 0)))
        in_specs.append(pl.BlockSpec((NS, 1, 1, H_B, DK_B, DV_B), lambda b, j: (b, 0, d, 0, 0, 0)))
        args += [sd0, sr0]
    out_specs = [tok(D_A + D_B)]
    out_shape = [jax.ShapeDtypeStruct((bt, t, D_A + D_B), BF16 if rev else F32)]
    if emit_state:
        out_specs.append(pl.BlockSpec((NS, 1, 1, H_A, DK_A, DV_A), lambda b, j: (b, 0, 0, 0, 0, 0)))
        out_specs.append(pl.BlockSpec((NS, 1, 1, H_B, DK_B, DV_B), lambda b, j: (b, 0, 0, 0, 0, 0)))
        out_shape.append(jax.ShapeDtypeStruct((bt, DEPTH, 1, H_A, DK_A, DV_A), F32))
        out_shape.append(jax.ShapeDtypeStruct((bt, DEPTH, 1, H_B, DK_B, DV_B), F32))
    scratch = [pltpu.VMEM((NS, DK_A, H_A * DV_A), F32),
               pltpu.VMEM((NS, H_B // 2, LANES, 2 * DV_B), F32)]
    return pl.pallas_call(
        functools.partial(_mixer_kernel, rev, has_init, emit_state, nblk),
        grid=(bt // NS, nblk),
        in_specs=in_specs,
        out_specs=out_specs,
        out_shape=out_shape,
        scratch_shapes=scratch,
        compiler_params=pltpu.CompilerParams(dimension_semantics=("arbitrary", "arbitrary"),
                                             vmem_limit_bytes=VMEM_LIMIT),
        name="mixer_bwd" if rev else "mixer_fwd",
    )(*args)


def _rope_tables(t):
    rows = t // GRID_W
    r = np.repeat(np.arange(rows, dtype=np.float32), GRID_W)
    col = np.tile(np.arange(GRID_W, dtype=np.float32), rows)
    nf = DK_B // 4
    inv = (np.float32(ROPE_BASE) ** (-np.arange(nf, dtype=np.float32) / np.float32(nf))).astype(np.float32)
    ang = jnp.asarray(np.concatenate([r[:, None] * inv, col[:, None] * inv], -1).astype(np.float32))
    cos, sin = jnp.cos(ang), jnp.sin(ang)
    zero = jnp.zeros_like(sin)
    cos_t = jnp.tile(cos, (1, 4))
    sin_up = jnp.tile(jnp.concatenate([-sin, zero], -1), (1, 2))
    sin_dn = jnp.tile(jnp.concatenate([zero, sin], -1), (1, 2))
    return jnp.stack([cos_t, sin_up, sin_dn], 0)


def kernel(x_prompt, x_sample, c, state_delta, state_ret, c_ctx, w_mod, b_mod, w_in, conv_w, a_log, dt_bias,
           norm_a_w, gn_w, gn_b, w_o, ln1_w, ln1_b, w_ff1, b_ff1, w_ff2, b_ff2, ln2_w, ln2_b):
    assert w_mod.shape[0] == DEPTH == 1
    n_dec = c.shape[0]
    rows = -(-(1 + n_dec) // SUBLANES) * SUBLANES
    cond = jnp.zeros((rows, D_MODEL), F32).at[0].set(c_ctx).at[1:1 + n_dec].set(c)
    mod = _modulation(cond, w_mod[0], b_mod[0])
    mod3 = mod.reshape(rows, 1, 6 * D_MODEL)

    w = w_in[0]
    c_ab = QKV_W + D_A
    w_in_p = (jnp.zeros((D_MODEL, P_W), BF16)
              .at[:, :c_ab].set(w[:, :c_ab].astype(BF16))
              .at[:, c_ab:COL_AB].set(w[:, c_ab + 2 * N_AB:].astype(BF16))
              .at[:, COL_AB:COL_AB + 2 * N_AB].set(w[:, c_ab:c_ab + 2 * N_AB].astype(BF16)))
    convw = jnp.zeros((SUBLANES, QKV_W), F32).at[:CONV_K].set(conv_w[0])
    par = (jnp.zeros((SUBLANES, LANES), F32).at[0, :N_AB].set(a_log[0].reshape(-1))
           .at[1, :N_AB].set(dt_bias[0].reshape(-1)))
    naw = norm_a_w[0].reshape(1, DV_A)
    gnw = gn_w[0].reshape(1, D_B)
    gnb = gn_b[0].reshape(1, D_B)
    wo = w_o[0].astype(BF16)
    w1 = w_ff1[0].astype(BF16)
    w2 = w_ff2[0].astype(BF16)
    row = lambda v: v[0].reshape(1, -1)

    def trunk(x, row0, row_stride, rope, sd0, sr0, emit_state, tm):
        qkv, qkb, vb, gates, gb = _projection(x, mod3, row0, row_stride, w_in_p, convw, par, rope, tm)
        fwd = _mixer(False, qkv, qkb, vb, gb, None, None, None, None, None, sd0, sr0, emit_state)
        bwd = _mixer(True, qkv, qkb, vb, gb, fwd[0], gates, naw, gnw, gnb, sd0, sr0, emit_state)
        o = bwd[0]
        if row_stride == 0:
            y = _post(o.reshape(1, -1, D_A + D_B), x.reshape(1, -1, D_MODEL), mod3, row0, 0, wo, row(ln1_w),
                      row(ln1_b), w1, row(b_ff1), w2, row(b_ff2), row(ln2_w), row(ln2_b), POST_TILE).reshape(x.shape)
        else:
            y = _post(o, x, mod3, row0, row_stride, wo, row(ln1_w), row(ln1_b), w1, row(b_ff1), w2, row(b_ff2),
                      row(ln2_w), row(ln2_b), POST_TILE)
        states = [jnp.concatenate([f, b], 2) for f, b in zip(fwd[1:], bwd[1:])]
        return y, states

    y_prompt, (new_sd, new_sr) = trunk(x_prompt, 0, 0, None, None, None, True, 256)
    y_sample, _ = trunk(x_sample, 1, 1, _rope_tables(x_sample.shape[1]), state_delta, state_ret, False, 512)
    return y_prompt, y_sample, new_sd, new_sr
```
